```python
import math
import jax
import jax.numpy as jnp
from jax import lax
import numpy as np

D_MODEL = 1024
BATCH = 16
SEQ = 4096
DEPTH = 4

ATTN_HEADS = 8
HEAD_DIM = 64
ATTN_WIDTH = ATTN_HEADS * HEAD_DIM
CONV_WIDTH = D_MODEL - ATTN_WIDTH
CONV_K = 3
ROPE_DIM = HEAD_DIM // 4
ROPE_THETA = 500000.0
DILATED_PAIRS = ((128, 1), (512, 4), (2048, 16))
SSM_GROUP = 16
SSM_GROUPS = D_MODEL // SSM_GROUP
SSM_STATE = 64
D_FF = ((8 * D_MODEL // 3 + 255) // 256) * 256
N_EXPERTS = 8
TOP_K = 2
D_FF_EXPERT = 7 * D_MODEL // 2
RMS_EPS = 1e-6
N_EVEN = (DEPTH + 1) // 2
N_ODD = DEPTH // 2
EVEN_SPLITS = (ATTN_WIDTH, 2 * ATTN_WIDTH, 3 * ATTN_WIDTH,
               3 * ATTN_WIDTH + CONV_WIDTH, 3 * ATTN_WIDTH + 2 * CONV_WIDTH)

kernel_name = 'hybrid_dilated_conv_s5_moe_trunk'


def _rmsnorm_f32(x, g):
    xf = x.astype(jnp.float32)
    return xf * lax.rsqrt(jnp.mean(xf * xf, axis=-1, keepdims=True) + RMS_EPS) * g.astype(jnp.float32)


def _ada_norm(x, g, shift, scale):
    y = _rmsnorm_f32(x, g) * (1.0 + scale[:, None, :].astype(jnp.float32)) + shift[:, None, :].astype(jnp.float32)
    return y.astype(x.dtype)


def _rope_tables(positions, dtype):
    inv = ROPE_THETA ** (-jnp.arange(0, ROPE_DIM, 2, dtype=jnp.float32) / ROPE_DIM)
    ang = positions.astype(jnp.float32)[..., None] * inv
    return jnp.cos(ang)[:, :, None, :].astype(dtype), jnp.sin(ang)[:, :, None, :].astype(dtype)


def _partial_rope(t, cos, sin):
    half = ROPE_DIM // 2
    t1 = t[..., :half]
    t2 = t[..., half:ROPE_DIM]
    return jnp.concatenate([t1 * cos - t2 * sin, t2 * cos + t1 * sin, t[..., ROPE_DIM:]], axis=-1)


def _dilated_branch(q, k, v, window, dil):
    bsz, seq, nh, hd = q.shape
    blk = window // dil
    span = blk * dil
    s_pad = -(-seq // span) * span
    nb = s_pad // span

    def to_blocks(t):
        t = jnp.pad(t, ((0, 0), (0, s_pad - seq), (0, 0), (0, 0)))
        return t.reshape(bsz, nb, blk, dil, nh, hd)

    def with_prev(t):
        prev = jnp.pad(t[:, :-1], ((0, 0), (1, 0), (0, 0), (0, 0), (0, 0), (0, 0)))
        return jnp.concatenate([prev, t], axis=2)

    qb = to_blocks(q)
    kc = with_prev(to_blocks(k))
    vc = with_prev(to_blocks(v)).astype(jnp.float32)
    s = jnp.einsum('bnqrhe,bnkrhe->bnrhqk', qb, kc).astype(jnp.float32)
    qi = jnp.arange(blk)[:, None]
    kj = jnp.arange(2 * blk)[None, :]
    dist = blk + qi - kj
    band = (dist >= 0) & (dist <= window // dil)
    has_prev = (jnp.arange(nb)[:, None, None] > 0) | (kj[None] >= blk)
    mask = band[None] & has_prev
    s = jnp.where(mask[:, None, None], s, -jnp.inf)
    m = jnp.max(s, axis=-1, keepdims=True)
    p = jnp.exp(s - m)
    l = jnp.sum(p, axis=-1)
    o = jnp.einsum('bnrhqk,bnkrhe->bnqrhe', p, vc)
    l_t = jnp.transpose(l, (0, 1, 4, 2, 3))
    lse = jnp.transpose(m[..., 0], (0, 1, 4, 2, 3)) + jnp.log(l_t)
    o = (o / l_t[..., None]).reshape(bsz, s_pad, nh, hd)[:, :seq]
    lse = lse.reshape(bsz, s_pad, nh)[:, :seq]
    return o, lse


def _even_mixer(h, cos, sin, w_in, conv_w, w_out):
    bsz, seq, _ = h.shape
    proj = h @ w_in
    q, k, v, b_gate, c_gate, xin = jnp.split(proj, EVEN_SPLITS, axis=-1)
    q = _partial_rope(q.reshape(bsz, seq, ATTN_HEADS, HEAD_DIM), cos, sin) * (HEAD_DIM ** -0.5)
    k = _partial_rope(k.reshape(bsz, seq, ATTN_HEADS, HEAD_DIM), cos, sin)
    v = v.reshape(bsz, seq, ATTN_HEADS, HEAD_DIM)
    outs, lses = zip(*[_dilated_branch(q, k, v, w, d) for (w, d) in DILATED_PAIRS])
    wts = jax.nn.softmax(jnp.stack(lses, axis=0), axis=0)
    a_out = jnp.sum(wts[..., None] * jnp.stack(outs, axis=0), axis=0)
    a_out = a_out.astype(h.dtype).reshape(bsz, seq, ATTN_WIDTH)
    u = c_gate * xin
    conv = lax.conv_general_dilated(u, conv_w[:, None, :], window_strides=(1,),
                                    padding=((CONV_K - 1, 0),),
                                    dimension_numbers=('NWC', 'WIO', 'NWC'),
                                    feature_group_count=CONV_WIDTH)
    b_out = b_gate * conv
    return jnp.concatenate([a_out, b_out], axis=-1) @ w_out


def _diag_scan(a_re, a_im, b_re, b_im):
    def combine(left, right):
        ar_l, ai_l, br_l, bi_l = left
        ar_r, ai_r, br_r, bi_r = right
        return (ar_r * ar_l - ai_r * ai_l,
                ar_r * ai_l + ai_r * ar_l,
                ar_r * br_l - ai_r * bi_l + br_r,
                ar_r * bi_l + ai_r * br_l + bi_r)
    _, _, x_re, x_im = lax.associative_scan(combine, (a_re, a_im, b_re, b_im), axis=0)
    return x_re, x_im


def _s5_mixer(h, a_re, a_im, log_step, b_re, b_im, c_re, c_im, d_skip, glu_w, glu_b):
    bsz, seq, d = h.shape
    f32 = jnp.float32
    hf = h.astype(f32)
    u = hf.reshape(bsz, seq, SSM_GROUPS, SSM_GROUP)
    a_re = a_re.astype(f32)
    a_im = a_im.astype(f32)
    dt = jnp.exp(log_step.astype(f32))[:, None]
    mag = jnp.exp(a_re * dt)
    abar_re = mag * jnp.cos(a_im * dt)
    abar_im = mag * jnp.sin(a_im * dt)
    den = a_re * a_re + a_im * a_im
    nr = abar_re - 1.0
    f_re = (nr * a_re + abar_im * a_im) / den
    f_im = (abar_im * a_re - nr * a_im) / den
    b_re = b_re.astype(f32)
    b_im = b_im.astype(f32)
    bb_re = f_re[..., None] * b_re - f_im[..., None] * b_im
    bb_im = f_re[..., None] * b_im + f_im[..., None] * b_re
    bu_re = jnp.einsum('gpc,bsgc->bsgp', bb_re, u)
    bu_im = jnp.einsum('gpc,bsgc->bsgp', bb_im, u)
    a_seq_re = jnp.broadcast_to(abar_re, (seq, SSM_GROUPS, SSM_STATE))
    a_seq_im = jnp.broadcast_to(abar_im, (seq, SSM_GROUPS, SSM_STATE))
    x_re, x_im = jax.vmap(_diag_scan, in_axes=(None, None, 0, 0))(a_seq_re, a_seq_im, bu_re, bu_im)
    y = (jnp.einsum('gcp,bsgp->bsgc', c_re.astype(f32), x_re)
         - jnp.einsum('gcp,bsgp->bsgc', c_im.astype(f32), x_im))
    y = y.reshape(bsz, seq, d) + d_skip.astype(f32) * hf
    g = jax.nn.gelu(y).astype(h.dtype)
    val, gate = jnp.split(g @ glu_w + glu_b, 2, axis=-1)
    return val * jax.nn.sigmoid(gate)


def _swiglu(h, w_gate, w_up, w_down):
    return (jax.nn.silu(h @ w_gate) * (h @ w_up)) @ w_down


def _moe_swiglu(h, router_w, router_b, w_gate, w_up, w_down):
    bsz, seq, d = h.shape
    hf = h.reshape(bsz * seq, d)
    logits = (hf @ router_w + router_b).astype(jnp.float32)
    probs = jax.nn.softmax(logits, axis=-1)
    top_p, top_i = lax.top_k(probs, TOP_K)
    top_p = top_p / jnp.sum(top_p, axis=-1, keepdims=True)
    gates = jnp.sum(jax.nn.one_hot(top_i, N_EXPERTS, dtype=jnp.float32) * top_p[..., None], axis=1)
    gates = gates.astype(h.dtype)
    out = jnp.zeros_like(hf)
    for e in range(N_EXPERTS):
        out = out + gates[:, e:e + 1] * _swiglu(hf, w_gate[e], w_up[e], w_down[e])
    return out.reshape(bsz, seq, d)


def setup_inputs(seed: int = 0) -> dict:
    key = jax.random.key(seed)
    ks = jax.random.split(key, 32)
    f32 = jnp.float32
    D = D_MODEL

    def nrm(k, shape, scale):
        return jax.random.normal(k, shape, f32) * scale

    x = nrm(ks[0], (BATCH, SEQ, D), 1.0)
    c = nrm(ks[1], (BATCH, D), 1.0)
    offs = jax.random.randint(ks[2], (BATCH, 1), 0, 1024, dtype=jnp.int32)
    positions = offs + jnp.arange(SEQ, dtype=jnp.int32)[None, :]
    mod_w = nrm(ks[3], (DEPTH, D, 6 * D), D ** -0.5)
    mod_b = nrm(ks[4], (DEPTH, 6 * D), 0.02)
    norm_mix_g = 1.0 + nrm(ks[5], (DEPTH, D), 0.02)
    norm_ffn_g = 1.0 + nrm(ks[6], (DEPTH, D), 0.02)
    ev_w_in = nrm(ks[7], (N_EVEN, D, 3 * ATTN_WIDTH + 3 * CONV_WIDTH), D ** -0.5)
    ev_conv_w = nrm(ks[8], (N_EVEN, CONV_K, CONV_WIDTH), CONV_K ** -0.5)
    ev_w_out = nrm(ks[9], (N_EVEN, D, D), D ** -0.5)
    ffn_w_gate = nrm(ks[10], (N_EVEN, D, D_FF), D ** -0.5)
    ffn_w_up = nrm(ks[11], (N_EVEN, D, D_FF), D ** -0.5)
    ffn_w_down = nrm(ks[12], (N_EVEN, D_FF, D), D_FF ** -0.5)
    n_idx = jnp.arange(SSM_STATE, dtype=f32)
    ssm_a_re = -0.5 + nrm(ks[13], (N_ODD, SSM_GROUPS, SSM_STATE), 0.01)
    ssm_a_im = math.pi * n_idx + nrm(ks[14], (N_ODD, SSM_GROUPS, SSM_STATE), 0.01)
    ssm_log_step = jax.random.uniform(ks[15], (N_ODD, SSM_GROUPS), f32, math.log(1e-3), math.log(1e-1))
    ssm_b_re = nrm(ks[16], (N_ODD, SSM_GROUPS, SSM_STATE, SSM_GROUP), (2 * SSM_GROUP) ** -0.5)
    ssm_b_im = nrm(ks[17], (N_ODD, SSM_GROUPS, SSM_STATE, SSM_GROUP), (2 * SSM_GROUP) ** -0.5)
    ssm_c_re = nrm(ks[18], (N_ODD, SSM_GROUPS, SSM_GROUP, SSM_STATE), SSM_STATE ** -0.5)
    ssm_c_im = nrm(ks[19], (N_ODD, SSM_GROUPS, SSM_GROUP, SSM_STATE), SSM_STATE ** -0.5)
    ssm_d = nrm(ks[20], (N_ODD, D), 1.0)
    glu_w = nrm(ks[21], (N_ODD, D, 2 * D), D ** -0.5)
    glu_b = nrm(ks[22], (N_ODD, 2 * D), 0.02)
    moe_router_w = nrm(ks[23], (N_ODD, D, N_EXPERTS), D ** -0.5)
    moe_router_b = nrm(ks[24], (N_ODD, N_EXPERTS), 0.01)
    moe_w_gate = nrm(ks[25], (N_ODD, N_EXPERTS, D, D_FF_EXPERT), D ** -0.5)
    moe_w_up = nrm(ks[26], (N_ODD, N_EXPERTS, D, D_FF_EXPERT), D ** -0.5)
    moe_w_down = nrm(ks[27], (N_ODD, N_EXPERTS, D_FF_EXPERT, D), D_FF_EXPERT ** -0.5)
    final_norm_g = 1.0 + nrm(ks[28], (D,), 0.02)
    return {'x': x, 'c': c, 'positions': positions, 'mod_w': mod_w, 'mod_b': mod_b,
            'norm_mix_g': norm_mix_g, 'norm_ffn_g': norm_ffn_g,
            'ev_w_in': ev_w_in, 'ev_conv_w': ev_conv_w, 'ev_w_out': ev_w_out,
            'ffn_w_gate': ffn_w_gate, 'ffn_w_up': ffn_w_up, 'ffn_w_down': ffn_w_down,
            'ssm_a_re': ssm_a_re, 'ssm_a_im': ssm_a_im, 'ssm_log_step': ssm_log_step,
            'ssm_b_re': ssm_b_re, 'ssm_b_im': ssm_b_im, 'ssm_c_re': ssm_c_re, 'ssm_c_im': ssm_c_im,
            'ssm_d': ssm_d, 'glu_w': glu_w, 'glu_b': glu_b,
            'moe_router_w': moe_router_w, 'moe_router_b': moe_router_b,
            'moe_w_gate': moe_w_gate, 'moe_w_up': moe_w_up, 'moe_w_down': moe_w_down,
            'final_norm_g': final_norm_g}


def reference(x, c, positions, mod_w, mod_b, norm_mix_g, norm_ffn_g,
              ev_w_in, ev_conv_w, ev_w_out, ffn_w_gate, ffn_w_up, ffn_w_down,
              ssm_a_re, ssm_a_im, ssm_log_step, ssm_b_re, ssm_b_im, ssm_c_re, ssm_c_im,
              ssm_d, glu_w, glu_b, moe_router_w, moe_router_b,
              moe_w_gate, moe_w_up, moe_w_down, final_norm_g):
    cos, sin = _rope_tables(positions, x.dtype)
    cond = jax.nn.silu(c)
    for layer in range(DEPTH):
        mod = cond @ mod_w[layer] + mod_b[layer]
        sh_m, sc_m, g_m, sh_f, sc_f, g_f = jnp.split(mod, 6, axis=-1)
        i = layer // 2
        h = _ada_norm(x, norm_mix_g[layer], sh_m, sc_m)
        if layer % 2 == 0:
            mix = _even_mixer(h, cos, sin, ev_w_in[i], ev_conv_w[i], ev_w_out[i])
        else:
            mix = _s5_mixer(h, ssm_a_re[i], ssm_a_im[i], ssm_log_step[i], ssm_b_re[i], ssm_b_im[i],
                            ssm_c_re[i], ssm_c_im[i], ssm_d[i], glu_w[i], glu_b[i])
        x = x + g_m[:, None, :] * mix
        h = _ada_norm(x, norm_ffn_g[layer], sh_f, sc_f)
        if layer % 2 == 0:
            ff = _swiglu(h, ffn_w_gate[i], ffn_w_up[i], ffn_w_down[i])
        else:
            ff = _moe_swiglu(h, moe_router_w[i], moe_router_b[i], moe_w_gate[i], moe_w_up[i], moe_w_down[i])
        x = x + g_f[:, None, :] * ff
    return _rmsnorm_f32(x, final_norm_g).astype(x.dtype)
```

```python
import functools
import math

import jax
import jax.numpy as jnp
from jax import lax
from jax.experimental import pallas as pl
from jax.experimental.pallas import tpu as pltpu

F32 = jnp.float32
BF16 = jnp.bfloat16

ATTN_HEADS = 8
HEAD_DIM = 64
ATTN_WIDTH = ATTN_HEADS * HEAD_DIM
ROPE_DIM = HEAD_DIM // 4
ROPE_THETA = 500000.0
DILATED_PAIRS = ((128, 1), (512, 4), (2048, 16))
ATTN_BLOCK = 128
SSM_GROUP = 16
SSM_STATE = 64
SSM_CHUNK = 16
N_EXPERTS = 8
RMS_EPS = 1e-6

LANES = 128
VMEM_LIMIT_BYTES = 56 * 1024 * 1024

ROW_TILE = 512
MOE_TILE = 512
COMBINE_TILE = 256
MOE_FF_CHUNKS = 4
ATTN_GROUP = 4
SSM_GROUPS_PER_STEP = 2


def _params(n_axes, vmem=VMEM_LIMIT_BYTES):
    return pltpu.CompilerParams(
        dimension_semantics=("arbitrary",) * n_axes, vmem_limit_bytes=vmem)


def _dot(a, b):
    return jnp.dot(a, b, preferred_element_type=F32)


def _sigmoid(x):
    return 1.0 / (1.0 + jnp.exp(-x))


def _split_bf16(x):
    hi = x.astype(BF16)
    lo = (x - hi.astype(F32)).astype(BF16)
    return hi, lo


def _ada_norm(x, g, shift, scale):
    ms = jnp.mean(x * x, axis=-1, keepdims=True)
    return x * lax.rsqrt(ms + RMS_EPS) * g * (1.0 + scale) + shift


def _mod_kernel(c_ref, w_ref, b_ref, o_ref):
    c = c_ref[...]
    cond = c * _sigmoid(c)
    ch, cl = _split_bf16(cond)
    wh, wl = _split_bf16(w_ref[...])
    o_ref[...] = _dot(ch, wh) + _dot(cl, wh) + _dot(ch, wl) + b_ref[...]


def _modulation(c, mod_w, mod_b):
    depth, d, n = mod_w.shape
    bsz = c.shape[0]
    tn = 1024
    return pl.pallas_call(
        _mod_kernel,
        grid=(depth, n // tn),
        in_specs=[pl.BlockSpec((bsz, d), lambda l, j: (0, 0)),
                  pl.BlockSpec((None, d, tn), lambda l, j: (l, 0, j)),
                  pl.BlockSpec((None, 1, tn), lambda l, j: (l, 0, j))],
        out_specs=pl.BlockSpec((None, bsz, tn), lambda l, j: (l, 0, j)),
        out_shape=jax.ShapeDtypeStruct((depth, bsz, n), F32),
        compiler_params=_params(2),
        name="modulation",
    )(c, mod_w, mod_b.reshape(depth, 1, n))


def _rope_kernel(pos_ref, inv_ref, cos_ref, sin_ref):
    ang = pos_ref[...].astype(F32) * inv_ref[...]
    cos_ref[...] = jnp.cos(ang)
    sin_ref[...] = jnp.sin(ang)


def _rope_tables(positions):
    bsz, seq = positions.shape
    inv = ROPE_THETA ** (-jnp.arange(0, ROPE_DIM, 2, dtype=F32) / ROPE_DIM)
    lane = jnp.arange(LANES) % HEAD_DIM
    inv_lane = jnp.where(lane < ROPE_DIM, inv[lane % (ROPE_DIM // 2)], 0.0).reshape(1, LANES)
    tm = ROW_TILE
    spec = pl.BlockSpec((None, tm, LANES), lambda b, i: (b, i, 0))
    return pl.pallas_call(
        _rope_kernel,
        grid=(bsz, seq // tm),
        in_specs=[pl.BlockSpec((None, tm, 1), lambda b, i: (b, i, 0)),
                  pl.BlockSpec((1, LANES), lambda b, i: (0, 0))],
        out_specs=[spec, spec],
        out_shape=[jax.ShapeDtypeStruct((bsz, seq, LANES), F32)] * 2,
        compiler_params=_params(2),
        name="rope_tables",
    )(positions.reshape(bsz, seq, 1), inv_lane)


def _inproj_kernel(x_ref, g_ref, sh_ref, sc_ref, w_ref, cos_ref, sin_ref, cw_ref,
                   q_ref, k_ref, v_ref, bo_ref, ubuf):
    tm = x_ref.shape[0]
    aw = ATTN_WIDTH
    cwid = cw_ref.shape[1]
    h = _ada_norm(x_ref[...], g_ref[...], sh_ref[...], sc_ref[...]).astype(BF16)
    proj = _dot(h, w_ref[...])

    cos = cos_ref[...]
    sin = sin_ref[...]
    lane = lax.broadcasted_iota(jnp.int32, (tm, LANES), 1) % HEAD_DIM
    first_half = lane < ROPE_DIM // 2

    def rope(t):
        rot = jnp.where(first_half,
                        -pltpu.roll(t, LANES - ROPE_DIM // 2, 1),
                        pltpu.roll(t, ROPE_DIM // 2, 1))
        return t * cos + rot * sin

    for j in range(aw // LANES):
        sl = slice(j * LANES, (j + 1) * LANES)
        q_ref[:, sl] = (rope(proj[:, sl]) * (HEAD_DIM ** -0.5)).astype(BF16)
        k_ref[:, sl] = rope(proj[:, aw + j * LANES: aw + (j + 1) * LANES]).astype(BF16)
    v_ref[...] = proj[:, 2 * aw:3 * aw].astype(BF16)

    b_gate = proj[:, 3 * aw:3 * aw + cwid]
    c_gate = proj[:, 3 * aw + cwid:3 * aw + 2 * cwid]
    xin = proj[:, 3 * aw + 2 * cwid:]
    u = c_gate * xin

    @pl.when(pl.program_id(1) == 0)
    def _():
        ubuf[0:8, :] = jnp.zeros((8, cwid), F32)

    ubuf[8:, :] = u
    conv = (cw_ref[0:1, :] * ubuf[6:6 + tm, :] + cw_ref[1:2, :] * ubuf[7:7 + tm, :]
            + cw_ref[2:3, :] * u)
    bo_ref[...] = (b_gate * conv).astype(BF16)
    ubuf[0:8, :] = ubuf[tm:tm + 8, :]


def _even_inproj(x, g, shift, scale, w_in, cos, sin, conv_w):
    bsz, seq, d = x.shape
    n = w_in.shape[1]
    cwid = conv_w.shape[1]
    tm = ROW_TILE
    row = lambda w: pl.BlockSpec((None, tm, w), lambda b, i: (b, i, 0))
    per_b = pl.BlockSpec((None, 1, d), lambda b, i: (b, 0, 0))
    return pl.pallas_call(
        _inproj_kernel,
        grid=(bsz, seq // tm),
        in_specs=[row(d), pl.BlockSpec((1, d), lambda b, i: (0, 0)), per_b, per_b,
                  pl.BlockSpec((d, n), lambda b, i: (0, 0)),
                  row(LANES), row(LANES),
                  pl.BlockSpec(conv_w.shape, lambda b, i: (0, 0))],
        out_specs=[row(ATTN_WIDTH)] * 3 + [row(cwid)],
        out_shape=[jax.ShapeDtypeStruct((bsz, seq, ATTN_WIDTH), BF16)] * 3
        + [jax.ShapeDtypeStruct((bsz, seq, cwid), BF16)],
        scratch_shapes=[pltpu.VMEM((tm + 8, cwid), F32)],
        compiler_params=_params(2),
        name="even_inproj",
    )(x, g.reshape(1, d), shift.reshape(bsz, 1, d), scale.reshape(bsz, 1, d),
      w_in, cos, sin, conv_w)


def _attn_kernel(*refs, first, last, gb):
    q_ref, kc_ref, vc_ref, kp_ref, vp_ref = refs[:5]
    refs = refs[5:]
    if not first:
        acc_in, ml_in = refs[:2]
        refs = refs[2:]
    if last:
        (a_out,) = refs
    else:
        acc_out, ml_out = refs

    blk = ATTN_BLOCK
    n0 = pl.program_id(1) * gb
    row = lax.broadcasted_iota(jnp.int32, (blk, 2 * blk), 0)
    col = lax.broadcasted_iota(jnp.int32, (blk, 2 * blk), 1)
    band = (col >= row) & (col <= row + blk)
    lane = lax.broadcasted_iota(jnp.int32, (blk, LANES), 1)
    lane_kv = lax.broadcasted_iota(jnp.int32, (2 * blk, LANES), 1)
    neg_inf = jnp.float32(-jnp.inf)

    for i in range(gb):
        if i == 0:
            k_prev, v_prev = kp_ref[...], vp_ref[...]
            valid = band & ((col >= blk) | (n0 > 0))
        else:
            k_prev, v_prev = kc_ref[i - 1], vc_ref[i - 1]
            valid = band
        kk = jnp.concatenate([k_prev, kc_ref[i]], axis=0)
        vv = jnp.concatenate([v_prev, vc_ref[i]], axis=0)
        q = q_ref[i]
        ml_old = None if first else ml_in[i]
        ml_new = jnp.zeros((blk, LANES), F32)
        for j in range(ATTN_WIDTH // LANES):
            sl = slice(j * LANES, (j + 1) * LANES)
            qj, kj, vj = q[:, sl], kk[:, sl], vv[:, sl]
            pv, alpha, linv = [], [], []
            for hh in range(LANES // HEAD_DIM):
                head = j * (LANES // HEAD_DIM) + hh
                in_head = (lane // HEAD_DIM) == hh
                in_head_kv = (lane_kv // HEAD_DIM) == hh
                qm = jnp.where(in_head, qj, jnp.zeros_like(qj))
                s = lax.dot_general(qm, kj, (((1,), (1,)), ((), ())),
                                    preferred_element_type=F32)
                s = jnp.where(valid, s, neg_inf)
                m_new = jnp.max(s, axis=1, keepdims=True)
                if not first:
                    m_old = ml_old[:, head:head + 1]
                    l_old = ml_old[:, ATTN_HEADS + head:ATTN_HEADS + head + 1]
                    m_new = jnp.maximum(m_old, m_new)
                    a = jnp.exp(m_old - m_new)
                    alpha.append(a)
                p = jnp.exp(s - m_new)
                l_new = jnp.sum(p, axis=1, keepdims=True)
                if not first:
                    l_new = a * l_old + l_new
                vm = jnp.where(in_head_kv, vj, jnp.zeros_like(vj))
                pv.append(_dot(p.astype(BF16), vm))
                linv.append(1.0 / l_new)
                ml_new = jnp.where(lane == head, m_new, ml_new)
                ml_new = jnp.where(lane == ATTN_HEADS + head, l_new, ml_new)
            acc = pv[0] + pv[1]
            if not first:
                acc = jnp.where(lane < HEAD_DIM, alpha[0], alpha[1]) * acc_in[i, :, sl] + acc
            if last:
                a_out[i, :, sl] = (acc * jnp.where(lane < HEAD_DIM, linv[0], linv[1])).astype(BF16)
            else:
                acc_out[i, :, sl] = acc
        if not last:
            ml_out[i] = ml_new


def _attn_branch(q, k, v, state, dil, first, last):
    bsz, seq, aw = q.shape
    blk = ATTN_BLOCK
    nb = seq // (blk * dil)
    gb = min(ATTN_GROUP, nb)
    view = lambda t, w: t.reshape(bsz, nb, blk, dil * w)
    cur = lambda w: pl.BlockSpec((None, gb, blk, w), lambda b, n, r: (b, n, 0, r))
    prev = pl.BlockSpec((None, None, blk, aw),
                        lambda b, n, r: (b, jnp.maximum(n * gb - 1, 0), 0, r))
    ins = [view(q, aw), view(k, aw), view(v, aw), view(k, aw), view(v, aw)]
    in_specs = [cur(aw), cur(aw), cur(aw), prev, prev]
    if not first:
        acc, ml = state
        ins += [view(acc, aw), view(ml, LANES)]
        in_specs += [cur(aw), cur(LANES)]
    if last:
        out_specs = [cur(aw)]
        out_shape = [jax.ShapeDtypeStruct((bsz, nb, blk, dil * aw), BF16)]
    else:
        out_specs = [cur(aw), cur(LANES)]
        out_shape = [jax.ShapeDtypeStruct((bsz, nb, blk, dil * aw), F32),
                     jax.ShapeDtypeStruct((bsz, nb, blk, dil * LANES), F32)]
    outs = pl.pallas_call(
        functools.partial(_attn_kernel, first=first, last=last, gb=gb),
        grid=(bsz, nb // gb, dil),
        in_specs=in_specs, out_specs=out_specs, out_shape=out_shape,
        compiler_params=_params(3),
        name=f"attn_dil{dil}",
    )(*ins)
    if last:
        return outs[0].reshape(bsz, seq, aw)
    return outs[0].reshape(bsz, seq, aw), outs[1].reshape(bsz, seq, LANES)


def _dilated_attention(q, k, v):
    state = None
    for idx, (window, dil) in enumerate(DILATED_PAIRS):
        assert window // dil == ATTN_BLOCK and q.shape[1] % window == 0
        state = _attn_branch(q, k, v, state, dil, idx == 0, idx == len(DILATED_PAIRS) - 1)
    return state


def _outproj_kernel(a_ref, b_ref, x_ref, wa_ref, wb_ref, gm_ref, g_ref, sh_ref, sc_ref,
                    x1_ref, h_ref):
    mix = _dot(a_ref[...], wa_ref[...]) + _dot(b_ref[...], wb_ref[...])
    x1 = x_ref[...] + gm_ref[...] * mix
    x1_ref[...] = x1
    h_ref[...] = _ada_norm(x1, g_ref[...], sh_ref[...], sc_ref[...]).astype(BF16)


def _even_outproj(a, b, x, w_out, gate_m, g, shift, scale):
    bsz, seq, d = x.shape
    aw, bw = a.shape[2], b.shape[2]
    tm = ROW_TILE
    row = lambda w: pl.BlockSpec((None, tm, w), lambda bi, i: (bi, i, 0))
    per_b = pl.BlockSpec((None, 1, d), lambda bi, i: (bi, 0, 0))
    full = lambda s: pl.BlockSpec(s, lambda bi, i: (0, 0))
    pb = lambda t: t.reshape(bsz, 1, d)
    return pl.pallas_call(
        _outproj_kernel,
        grid=(bsz, seq // tm),
        in_specs=[row(aw), row(bw), row(d), full((aw, d)), full((bw, d)),
                  per_b, full((1, d)), per_b, per_b],
        out_specs=[row(d), row(d)],
        out_shape=[jax.ShapeDtypeStruct((bsz, seq, d), F32),
                   jax.ShapeDtypeStruct((bsz, seq, d), BF16)],
        compiler_params=_params(2),
        name="even_outproj",
    )(a, b, x, w_out[:aw], w_out[aw:], pb(gate_m), g.reshape(1, d), pb(shift), pb(scale))


def _ffn_kernel(h_ref, x_ref, wg_ref, wu_ref, wd_ref, gf_ref, o_ref):
    h = h_ref[...]
    g = _dot(h, wg_ref[...])
    u = _dot(h, wu_ref[...])
    act = (g * _sigmoid(g) * u).astype(BF16)
    o_ref[...] = x_ref[...] + gf_ref[...] * _dot(act, wd_ref[...])


def _dense_ffn(h, x, w_gate, w_up, w_down, gate_f):
    bsz, seq, d = x.shape
    ff = w_gate.shape[1]
    tm = ROW_TILE
    row = pl.BlockSpec((None, tm, d), lambda b, i: (b, i, 0))
    full = lambda s: pl.BlockSpec(s, lambda b, i: (0, 0), pipeline_mode=pl.Buffered(1))
    return pl.pallas_call(
        _ffn_kernel,
        grid=(bsz, seq // tm),
        in_specs=[row, row, full((d, ff)), full((d, ff)), full((ff, d)),
                  pl.BlockSpec((None, 1, d), lambda b, i: (b, 0, 0))],
        out_specs=row,
        out_shape=jax.ShapeDtypeStruct((bsz, seq, d), F32),
        compiler_params=_params(2),
        name="dense_ffn",
    )(h, x, w_gate, w_up, w_down, gate_f.reshape(bsz, 1, d))


def _norm_kernel(x_ref, g_ref, sh_ref, sc_ref, h_ref):
    h_ref[...] = _ada_norm(x_ref[...], g_ref[...], sh_ref[...], sc_ref[...]).astype(BF16)


def _ada_norm_call(x, g, shift, scale):
    bsz, seq, d = x.shape
    tm = ROW_TILE
    row = pl.BlockSpec((None, tm, d), lambda b, i: (b, i, 0))
    per_b = pl.BlockSpec((None, 1, d), lambda b, i: (b, 0, 0))
    return pl.pallas_call(
        _norm_kernel,
        grid=(bsz, seq // tm),
        in_specs=[row, pl.BlockSpec((1, d), lambda b, i: (0, 0)), per_b, per_b],
        out_specs=row,
        out_shape=jax.ShapeDtypeStruct((bsz, seq, d), BF16),
        compiler_params=_params(2),
        name="ada_norm",
    )(x, g.reshape(1, d), shift.reshape(bsz, 1, d), scale.reshape(bsz, 1, d))


def _ssm_operators(a_re, a_im, log_step, b_re, b_im, c_re, c_im):
    L = SSM_CHUNK
    a_re, a_im = a_re.astype(F32), a_im.astype(F32)
    dt = jnp.exp(log_step.astype(F32))[:, None]
    mag = jnp.exp(a_re * dt)
    abar_re = mag * jnp.cos(a_im * dt)
    abar_im = mag * jnp.sin(a_im * dt)
    den = a_re * a_re + a_im * a_im
    nr = abar_re - 1.0
    f_re = (nr * a_re + abar_im * a_im) / den
    f_im = (abar_im * a_re - nr * a_im) / den
    b_re, b_im = b_re.astype(F32), b_im.astype(F32)
    bb_re = f_re[..., None] * b_re - f_im[..., None] * b_im
    bb_im = f_re[..., None] * b_im + f_im[..., None] * b_re
    c_re, c_im = c_re.astype(F32), c_im.astype(F32)

    def step(carry, _):
        pr, pi = carry
        nxt = (pr * abar_re - pi * abar_im, pr * abar_im + pi * abar_re)
        return nxt, carry
    (pl_re, pl_im), (pw_re, pw_im) = lax.scan(
        step, (jnp.ones_like(abar_re), jnp.zeros_like(abar_re)), None, length=L)
    pw_re = jnp.concatenate([pw_re, pl_re[None]], axis=0)
    pw_im = jnp.concatenate([pw_im, pl_im[None]], axis=0)

    hi = lax.Precision.HIGHEST
    cp_re = c_re[None] * pw_re[:L, :, None, :] - c_im[None] * pw_im[:L, :, None, :]
    cp_im = c_re[None] * pw_im[:L, :, None, :] + c_im[None] * pw_re[:L, :, None, :]
    w = (jnp.einsum('tgcp,gpd->tgdc', cp_re, bb_re, precision=hi)
         - jnp.einsum('tgcp,gpd->tgdc', cp_im, bb_im, precision=hi))
    s_idx = jnp.arange(L)[:, None]
    t_idx = jnp.arange(L)[None, :]
    tau = t_idx - s_idx
    toep = jnp.where((tau >= 0)[:, :, None, None, None], w[jnp.clip(tau, 0, L - 1)], 0.0)
    n_g, n_c = a_re.shape[0], b_re.shape[2]
    toep = toep.transpose(2, 0, 3, 1, 4).reshape(n_g, L * n_c, L * n_c)

    rp_re, rp_im = pw_re[:L][::-1], pw_im[:L][::-1]
    so_re = rp_re[..., None] * bb_re[None] - rp_im[..., None] * bb_im[None]
    so_im = rp_re[..., None] * bb_im[None] + rp_im[..., None] * bb_re[None]
    to_rows = lambda t: t.transpose(1, 0, 3, 2).reshape(n_g, L * n_c, -1)
    s_out = jnp.concatenate([to_rows(so_re), to_rows(so_im)], axis=-1)
    s_out_sw = jnp.concatenate([to_rows(so_im), to_rows(so_re)], axis=-1)

    qp_re, qp_im = pw_re[1:], pw_im[1:]
    ci_re = c_re[None] * qp_re[:, :, None, :] - c_im[None] * qp_im[:, :, None, :]
    ci_im = c_re[None] * qp_im[:, :, None, :] + c_im[None] * qp_re[:, :, None, :]
    to_cols = lambda t: t.transpose(1, 3, 0, 2).reshape(n_g, -1, L * n_c)
    c_in = jnp.concatenate([to_cols(ci_re), -to_cols(ci_im)], axis=1)

    al_re, al_im = pw_re[L], pw_im[L]
    a1 = jnp.concatenate([al_re, al_re], axis=-1)[:, None, :]
    a2 = jnp.concatenate([-al_im, al_im], axis=-1)[:, None, :]
    return (toep.astype(BF16), s_out.astype(BF16), s_out_sw.astype(BF16),
            c_in.astype(BF16), a1, a2)


def _ssm_kernel(u_ref, t_ref, s_ref, ssw_ref, cin_ref, a1_ref, a2_ref, y_ref,
                st_ref, stsw_ref, xs_ref, *, bsz):
    n_chunks = u_ref.shape[1] // bsz
    for gi in range(u_ref.shape[0]):
        u = u_ref[gi]
        st_ref[...] = _dot(u, s_ref[gi])
        stsw_ref[...] = _dot(u, ssw_ref[gi])
        a1 = jnp.broadcast_to(a1_ref[gi], (bsz, 2 * SSM_STATE))
        a2 = jnp.broadcast_to(a2_ref[gi], (bsz, 2 * SSM_STATE))

        def body(k, carry):
            x, z = carry
            rows = pl.ds(pl.multiple_of(k * bsz, bsz), bsz)
            xs_ref[rows, :] = x
            x_new = x * a1 + z * a2 + st_ref[rows, :]
            z_new = z * a1 - x * a2 + stsw_ref[rows, :]
            return x_new, z_new

        zero = jnp.zeros((bsz, 2 * SSM_STATE), F32)
        lax.fori_loop(0, n_chunks, body, (zero, zero), unroll=8)
        y = _dot(u, t_ref[gi]) + _dot(xs_ref[...].astype(BF16), cin_ref[gi])
        y_ref[gi] = y.astype(BF16)


def _ssm_scan(h, ops):
    bsz, seq, d = h.shape
    toep, s_out, s_out_sw, c_in, a1, a2 = ops
    n_g = toep.shape[0]
    L, C = SSM_CHUNK, SSM_GROUP
    n_k = seq // L
    rows = n_k * bsz
    u = h.reshape(bsz, n_k, L, n_g, C).transpose(3, 1, 0, 2, 4).reshape(n_g, rows, L * C)
    gs = SSM_GROUPS_PER_STEP
    blk = lambda s: pl.BlockSpec((gs,) + s, lambda g: (g, 0, 0))
    y = pl.pallas_call(
        functools.partial(_ssm_kernel, bsz=bsz),
        grid=(n_g // gs,),
        in_specs=[blk((rows, L * C)), blk((L * C, L * C)), blk((L * C, 2 * SSM_STATE)),
                  blk((L * C, 2 * SSM_STATE)), blk((2 * SSM_STATE, L * C)),
                  blk((1, 2 * SSM_STATE)), blk((1, 2 * SSM_STATE))],
        out_specs=blk((rows, L * C)),
        out_shape=jax.ShapeDtypeStruct((n_g, rows, L * C), BF16),
        scratch_shapes=[pltpu.VMEM((rows, 2 * SSM_STATE), F32)] * 3,
        compiler_params=_params(1),
        name="ssm_scan",
    )(u, toep, s_out, s_out_sw, c_in, a1, a2)
    return y.reshape(n_g, n_k, bsz, L, C).transpose(2, 1, 3, 0, 4).reshape(bsz, seq, d)


def _gelu_tanh(x):
    return 0.5 * x * (1.0 + jnp.tanh(math.sqrt(2.0 / math.pi) * (x + 0.044715 * (x * x * x))))


def _ssm_out_kernel(y_ref, h_ref, x_ref, d_ref, w_ref, b_ref, gm_ref, g_ref, sh_ref, sc_ref,
                    rwh_ref, rwl_ref, rb_ref, x1_ref, h2_ref, route_ref):
    d = x_ref.shape[1]
    y = y_ref[...].astype(F32) + d_ref[...] * h_ref[...].astype(F32)
    z = _dot(_gelu_tanh(y).astype(BF16), w_ref[...]) + b_ref[...]
    mix = z[:, :d] * _sigmoid(z[:, d:])
    x1 = x_ref[...] + gm_ref[...] * mix
    x1_ref[...] = x1
    h2 = _ada_norm(x1, g_ref[...], sh_ref[...], sc_ref[...])
    h2_ref[...] = h2

    hh, hl = _split_bf16(h2)
    logits = _dot(hh, rwh_ref[...]) + _dot(hl, rwh_ref[...]) + _dot(hh, rwl_ref[...]) + rb_ref[...]
    lane = lax.broadcasted_iota(jnp.int32, logits.shape, 1)
    lane_f = lane.astype(F32)
    neg_inf = jnp.float32(-jnp.inf)
    logits = jnp.where(lane < N_EXPERTS, logits, neg_inf)
    m1 = jnp.max(logits, axis=1, keepdims=True)
    i1 = jnp.min(jnp.where(logits == m1, lane_f, float(LANES)), axis=1, keepdims=True)
    rest = jnp.where(lane_f == i1, neg_inf, logits)
    m2 = jnp.max(rest, axis=1, keepdims=True)
    i2 = jnp.min(jnp.where(rest == m2, lane_f, float(LANES)), axis=1, keepdims=True)
    e2 = jnp.exp(m2 - m1)
    g1 = 1.0 / (1.0 + e2)
    g2 = e2 / (1.0 + e2)
    route = jnp.where(lane == 0, i1, jnp.where(lane == 1, i2,
                      jnp.where(lane == 2, g1, jnp.where(lane == 3, g2, 0.0))))
    route_ref[...] = route


def _ssm_out(y, h, x, d_skip, glu_w, glu_b, gate_m, g, shift, scale, router_w, router_b):
    bsz, seq, d = x.shape
    tm = ROW_TILE
    row = lambda w: pl.BlockSpec((None, tm, w), lambda b, i: (b, i, 0))
    per_b = pl.BlockSpec((None, 1, d), lambda b, i: (b, 0, 0))
    full = lambda s: pl.BlockSpec(s, lambda b, i: (0, 0))
    pb = lambda t: t.reshape(bsz, 1, d)
    rw = jnp.zeros((d, LANES), F32).at[:, :N_EXPERTS].set(router_w)
    rwh = rw.astype(BF16)
    rwl = (rw - rwh.astype(F32)).astype(BF16)
    rb = jnp.zeros((1, LANES), F32).at[0, :N_EXPERTS].set(router_b)
    return pl.pallas_call(
        _ssm_out_kernel,
        grid=(bsz, seq // tm),
        in_specs=[row(d), row(d), row(d), full((1, d)), full((d, 2 * d)), full((1, 2 * d)),
                  per_b, full((1, d)), per_b, per_b,
                  full((d, LANES)), full((d, LANES)), full((1, LANES))],
        out_specs=[row(d), row(d), row(LANES)],
        out_shape=[jax.ShapeDtypeStruct((bsz, seq, d), F32),
                   jax.ShapeDtypeStruct((bsz, seq, d), F32),
                   jax.ShapeDtypeStruct((bsz, seq, LANES), F32)],
        compiler_params=_params(2),
        name="ssm_out",
    )(y, h, x, d_skip.reshape(1, d), glu_w, glu_b.reshape(1, 2 * d), pb(gate_m),
      g.reshape(1, d), pb(shift), pb(scale), rwh, rwl, rb)


def _routing_tables(route, tm):
    n = route.shape[0]
    e_flat = route[:, :2].astype(jnp.int32).reshape(-1)
    onehot = (e_flat[:, None] == jnp.arange(N_EXPERTS)[None, :]).astype(jnp.int32)
    incl = jnp.cumsum(onehot, axis=0)
    counts = incl[-1]
    rank = jnp.sum((incl - onehot) * onehot, axis=1)
    padded = ((counts + tm - 1) // tm) * tm
    ends = jnp.cumsum(padded)
    starts = ends - padded
    dest = starts[e_flat] + rank
    n_tiles = (2 * n) // tm + N_EXPERTS
    src = jnp.zeros((n_tiles * tm,), jnp.int32).at[dest].set(jnp.arange(2 * n, dtype=jnp.int32) // 2)
    tile_start = jnp.arange(n_tiles, dtype=jnp.int32) * tm
    tile_expert = jnp.sum((tile_start[:, None] >= ends[None, :]).astype(jnp.int32), axis=1)
    tile_valid = (tile_start < ends[-1]).astype(jnp.int32)
    tile_expert = jnp.minimum(tile_expert, N_EXPERTS - 1)
    dest = dest.reshape(n, 2)
    return src, tile_expert, tile_valid, dest[:, 0], dest[:, 1]


def _moe_kernel(te_ref, tv_ref, src_ref, h_hbm, wg_ref, wu_ref, wd_ref, y_ref, hbuf, sem):
    t = pl.program_id(0)
    tm = hbuf.shape[0]
    ff = wg_ref.shape[1]
    fc = ff // MOE_FF_CHUNKS

    def row_copy(r, src_row):
        return pltpu.make_async_copy(h_hbm.at[pl.ds(src_row, 1)], hbuf.at[pl.ds(r, 1)], sem)

    @pl.when(tv_ref[t] > 0)
    def _():
        def issue(r, c):
            row_copy(r, src_ref[r]).start()
            return c
        lax.fori_loop(0, tm, issue, 0, unroll=8)

        def wait(r, c):
            row_copy(r, 0).wait()
            return c
        lax.fori_loop(0, tm, wait, 0, unroll=8)

        h = hbuf[...].astype(BF16)
        acc = jnp.zeros(y_ref.shape, F32)
        for c in range(MOE_FF_CHUNKS):
            g = _dot(h, wg_ref[:, c * fc:(c + 1) * fc])
            u = _dot(h, wu_ref[:, c * fc:(c + 1) * fc])
            act = (g * _sigmoid(g) * u).astype(BF16)
            acc = acc + _dot(act, wd_ref[c * fc:(c + 1) * fc, :])
        y_ref[...] = acc

    @pl.when(tv_ref[t] == 0)
    def _():
        y_ref[...] = jnp.zeros(y_ref.shape, F32)


def _moe_experts(h2, src, tile_expert, tile_valid, w_gate, w_up, w_down):
    n, d = h2.shape
    ff = w_gate.shape[2]
    tm = MOE_TILE
    n_tiles = tile_expert.shape[0]
    wspec = lambda s: pl.BlockSpec((None,) + s, lambda t, te, tv: (te[t], 0, 0),
                                   pipeline_mode=pl.Buffered(1))
    grid_spec = pltpu.PrefetchScalarGridSpec(
        num_scalar_prefetch=2,
        grid=(n_tiles,),
        in_specs=[pl.BlockSpec((tm,), lambda t, te, tv: (t,), memory_space=pltpu.SMEM),
                  pl.BlockSpec(memory_space=pl.ANY),
                  wspec((d, ff)), wspec((d, ff)), wspec((ff, d))],
        out_specs=pl.BlockSpec((tm, d), lambda t, te, tv: (t, 0)),
        scratch_shapes=[pltpu.VMEM((tm, d), F32), pltpu.SemaphoreType.DMA(())],
    )
    return pl.pallas_call(
        _moe_kernel,
        grid_spec=grid_spec,
        out_shape=jax.ShapeDtypeStruct((n_tiles * tm, d), F32),
        compiler_params=_params(1),
        name="moe_experts",
    )(tile_expert, tile_valid, src, h2, w_gate, w_up, w_down)


def _combine_kernel(p1_ref, p2_ref, y_hbm, route_ref, x_ref, gf_ref, fg_ref, o_ref, ybuf, sem,
                    *, final):
    tm = x_ref.shape[0]

    def row_copy(slot, r, src_row):
        return pltpu.make_async_copy(y_hbm.at[pl.ds(src_row, 1)], ybuf.at[slot, pl.ds(r, 1)], sem)

    def issue(r, c):
        row_copy(0, r, p1_ref[r]).start()
        row_copy(1, r, p2_ref[r]).start()
        return c
    lax.fori_loop(0, tm, issue, 0, unroll=8)

    def wait(r, c):
        row_copy(0, r, 0).wait()
        row_copy(1, r, 0).wait()
        return c
    lax.fori_loop(0, tm, wait, 0, unroll=8)

    route = route_ref[...]
    ff = route[:, 2:3] * ybuf[0] + route[:, 3:4] * ybuf[1]
    x2 = x_ref[...] + gf_ref[...] * ff
    if final:
        ms = jnp.mean(x2 * x2, axis=-1, keepdims=True)
        x2 = x2 * lax.rsqrt(ms + RMS_EPS) * fg_ref[...]
    o_ref[...] = x2


def _moe_combine(y_sorted, pos1, pos2, route, x1, gate_f, final_g, final):
    bsz, seq, d = x1.shape
    n = bsz * seq
    tm = COMBINE_TILE
    per_batch = seq // tm
    smem = pl.BlockSpec((tm,), lambda i: (i,), memory_space=pltpu.SMEM)
    out = pl.pallas_call(
        functools.partial(_combine_kernel, final=final),
        grid=(n // tm,),
        in_specs=[smem, smem, pl.BlockSpec(memory_space=pl.ANY),
                  pl.BlockSpec((tm, LANES), lambda i: (i, 0)),
                  pl.BlockSpec((tm, d), lambda i: (i, 0)),
                  pl.BlockSpec((None, 1, d), lambda i: (i // per_batch, 0, 0)),
                  pl.BlockSpec((1, d), lambda i: (0, 0))],
        out_specs=pl.BlockSpec((tm, d), lambda i: (i, 0)),
        out_shape=jax.ShapeDtypeStruct((n, d), F32),
        scratch_shapes=[pltpu.VMEM((2, tm, d), F32), pltpu.SemaphoreType.DMA(())],
        compiler_params=_params(1),
        name="moe_combine",
    )(pos1, pos2, y_sorted, route.reshape(n, LANES), x1.reshape(n, d),
      gate_f.reshape(bsz, 1, d), final_g.reshape(1, d))
    return out.reshape(bsz, seq, d)


def _final_norm_kernel(x_ref, g_ref, o_ref):
    x = x_ref[...]
    ms = jnp.mean(x * x, axis=-1, keepdims=True)
    o_ref[...] = x * lax.rsqrt(ms + RMS_EPS) * g_ref[...]


def _final_norm(x, g):
    bsz, seq, d = x.shape
    tm = ROW_TILE
    row = pl.BlockSpec((None, tm, d), lambda b, i: (b, i, 0))
    return pl.pallas_call(
        _final_norm_kernel,
        grid=(bsz, seq // tm),
        in_specs=[row, pl.BlockSpec((1, d), lambda b, i: (0, 0))],
        out_specs=row,
        out_shape=jax.ShapeDtypeStruct((bsz, seq, d), F32),
        compiler_params=_params(2),
        name="final_norm",
    )(x, g.reshape(1, d))


def _even_layer(x, mods, norm_mix_g, norm_ffn_g, cos, sin, w_in, conv_w, w_out,
                w_gate, w_up, w_down):
    sh_m, sc_m, g_m, sh_f, sc_f, g_f = mods
    q, k, v, b_out = _even_inproj(x, norm_mix_g, sh_m, sc_m, w_in.astype(BF16), cos, sin, conv_w)
    a_out = _dilated_attention(q, k, v)
    x1, h2 = _even_outproj(a_out, b_out, x, w_out.astype(BF16), g_m, norm_ffn_g, sh_f, sc_f)
    return _dense_ffn(h2, x1, w_gate.astype(BF16), w_up.astype(BF16), w_down.astype(BF16), g_f)


def _odd_layer(x, mods, norm_mix_g, norm_ffn_g, ssm, d_skip, glu_w, glu_b,
               router_w, router_b, w_gate, w_up, w_down, final_g, final):
    sh_m, sc_m, g_m, sh_f, sc_f, g_f = mods
    bsz, seq, d = x.shape
    h = _ada_norm_call(x, norm_mix_g, sh_m, sc_m)
    y = _ssm_scan(h, _ssm_operators(*ssm))
    x1, h2, route = _ssm_out(y, h, x, d_skip, glu_w.astype(BF16), glu_b, g_m,
                             norm_ffn_g, sh_f, sc_f, router_w, router_b)
    route = route.reshape(bsz * seq, LANES)
    src, tile_expert, tile_valid, pos1, pos2 = _routing_tables(route, MOE_TILE)
    y_sorted = _moe_experts(h2.reshape(bsz * seq, d), src, tile_expert, tile_valid,
                            w_gate.astype(BF16), w_up.astype(BF16), w_down.astype(BF16))
    return _moe_combine(y_sorted, pos1, pos2, route, x1, g_f, final_g, final)


def kernel(x, c, positions, mod_w, mod_b, norm_mix_g, norm_ffn_g, ev_w_in, ev_conv_w, ev_w_out, ffn_w_gate, ffn_w_up, ffn_w_down, ssm_a_re, ssm_a_im, ssm_log_step, ssm_b_re, ssm_b_im, ssm_c_re, ssm_c_im, ssm_d, glu_w, glu_b, moe_router_w, moe_router_b, moe_w_gate, moe_w_up, moe_w_down, final_norm_g):
    depth = mod_w.shape[0]
    d = x.shape[2]
    mod = _modulation(c, mod_w, mod_b)
    cos, sin = _rope_tables(positions)
    for layer in range(depth):
        mods = [mod[layer, :, j * d:(j + 1) * d] for j in range(6)]
        i = layer // 2
        if layer % 2 == 0:
            x = _even_layer(x, mods, norm_mix_g[layer], norm_ffn_g[layer], cos, sin,
                            ev_w_in[i], ev_conv_w[i], ev_w_out[i],
                            ffn_w_gate[i], ffn_w_up[i], ffn_w_down[i])
            if layer == depth - 1:
                x = _final_norm(x, final_norm_g)
        else:
            ssm = (ssm_a_re[i], ssm_a_im[i], ssm_log_step[i], ssm_b_re[i], ssm_b_im[i],
                   ssm_c_re[i], ssm_c_im[i])
            x = _odd_layer(x, mods, norm_mix_g[layer], norm_ffn_g[layer], ssm, ssm_d[i],
                           glu_w[i], glu_b[i], moe_router_w[i], moe_router_b[i],
                           moe_w_gate[i], moe_w_up[i], moe_w_down[i],
                           final_norm_g, layer == depth - 1)
    return x
```

```python
import functools
import math

import jax
import jax.numpy as jnp
from jax import lax
from jax.experimental import pallas as pl
from jax.experimental.pallas import tpu as pltpu

F32 = jnp.float32
BF16 = jnp.bfloat16

ATTN_HEADS = 8
HEAD_DIM = 64
ATTN_WIDTH = ATTN_HEADS * HEAD_DIM
ROPE_DIM = HEAD_DIM // 4
ROPE_THETA = 500000.0
DILATED_PAIRS = ((128, 1), (512, 4), (2048, 16))
ATTN_BLOCK = 128
SSM_GROUP = 16
SSM_STATE = 64
SSM_CHUNK = 16
N_EXPERTS = 8
RMS_EPS = 1e-6

LANES = 128
VMEM_LIMIT_BYTES = 56 * 1024 * 1024

ROW_TILE = 512
MOE_TILE = 512
COMBINE_TILE = 256
MOE_FF_CHUNKS = 2
ATTN_GROUP = 4
SSM_CHUNKS_PER_STEP = 64
SSM_LANE_GROUPS = LANES // SSM_GROUP


def _params(n_axes, vmem=VMEM_LIMIT_BYTES):
    return pltpu.CompilerParams(
        dimension_semantics=("arbitrary",) * n_axes, vmem_limit_bytes=vmem)


def _dot(a, b):
    return jnp.dot(a, b, preferred_element_type=F32)


def _sigmoid(x):
    return 1.0 / (1.0 + jnp.exp(-x))


def _split_bf16(x):
    hi = x.astype(BF16)
    lo = (x - hi.astype(F32)).astype(BF16)
    return hi, lo


def _ada_norm(x, g, shift, scale):
    ms = jnp.mean(x * x, axis=-1, keepdims=True)
    return x * lax.rsqrt(ms + RMS_EPS) * g * (1.0 + scale) + shift


def _mod_kernel(c_ref, w_ref, b_ref, o_ref):
    c = c_ref[...]
    cond = c * _sigmoid(c)
    ch, cl = _split_bf16(cond)
    wh, wl = _split_bf16(w_ref[...])
    o_ref[...] = _dot(ch, wh) + _dot(cl, wh) + _dot(ch, wl) + b_ref[...]


def _modulation(c, mod_w, mod_b):
    depth, d, n = mod_w.shape
    bsz = c.shape[0]
    tn = 1024
    return pl.pallas_call(
        _mod_kernel,
        grid=(depth, n // tn),
        in_specs=[pl.BlockSpec((bsz, d), lambda l, j: (0, 0)),
                  pl.BlockSpec((None, d, tn), lambda l, j: (l, 0, j)),
                  pl.BlockSpec((None, 1, tn), lambda l, j: (l, 0, j))],
        out_specs=pl.BlockSpec((None, bsz, tn), lambda l, j: (l, 0, j)),
        out_shape=jax.ShapeDtypeStruct((depth, bsz, n), F32),
        compiler_params=_params(2),
        name="modulation",
    )(c, mod_w, mod_b.reshape(depth, 1, n))


def _rope_kernel(pos_ref, inv_ref, cos_ref, sin_ref):
    ang = pos_ref[...].astype(F32) * inv_ref[...]
    cos_ref[...] = jnp.cos(ang)
    sin_ref[...] = jnp.sin(ang)


def _rope_tables(positions):
    bsz, seq = positions.shape
    inv = ROPE_THETA ** (-jnp.arange(0, ROPE_DIM, 2, dtype=F32) / ROPE_DIM)
    lane = jnp.arange(LANES) % HEAD_DIM
    inv_lane = jnp.where(lane < ROPE_DIM, inv[lane % (ROPE_DIM // 2)], 0.0).reshape(1, LANES)
    tm = ROW_TILE
    spec = pl.BlockSpec((None, tm, LANES), lambda b, i: (b, i, 0))
    return pl.pallas_call(
        _rope_kernel,
        grid=(bsz, seq // tm),
        in_specs=[pl.BlockSpec((None, tm, 1), lambda b, i: (b, i, 0)),
                  pl.BlockSpec((1, LANES), lambda b, i: (0, 0))],
        out_specs=[spec, spec],
        out_shape=[jax.ShapeDtypeStruct((bsz, seq, LANES), F32)] * 2,
        compiler_params=_params(2),
        name="rope_tables",
    )(positions.reshape(bsz, seq, 1), inv_lane)


def _inproj_kernel(x_ref, g_ref, sh_ref, sc_ref, w_ref, cos_ref, sin_ref, cw_ref,
                   q_ref, k_ref, v_ref, bo_ref, ubuf):
    tm = x_ref.shape[0]
    aw = ATTN_WIDTH
    cwid = cw_ref.shape[1]
    h = _ada_norm(x_ref[...], g_ref[...], sh_ref[...], sc_ref[...]).astype(BF16)
    proj = _dot(h, w_ref[...])

    cos = cos_ref[...]
    sin = sin_ref[...]
    lane = lax.broadcasted_iota(jnp.int32, (tm, LANES), 1) % HEAD_DIM
    first_half = lane < ROPE_DIM // 2

    def rope(t):
        rot = jnp.where(first_half,
                        -pltpu.roll(t, LANES - ROPE_DIM // 2, 1),
                        pltpu.roll(t, ROPE_DIM // 2, 1))
        return t * cos + rot * sin

    for j in range(aw // LANES):
        sl = slice(j * LANES, (j + 1) * LANES)
        q_ref[:, sl] = (rope(proj[:, sl]) * (HEAD_DIM ** -0.5)).astype(BF16)
        k_ref[:, sl] = rope(proj[:, aw + j * LANES: aw + (j + 1) * LANES]).astype(BF16)
    v_ref[...] = proj[:, 2 * aw:3 * aw].astype(BF16)

    b_gate = proj[:, 3 * aw:3 * aw + cwid]
    c_gate = proj[:, 3 * aw + cwid:3 * aw + 2 * cwid]
    xin = proj[:, 3 * aw + 2 * cwid:]
    u = c_gate * xin

    @pl.when(pl.program_id(1) == 0)
    def _():
        ubuf[0:8, :] = jnp.zeros((8, cwid), F32)

    ubuf[8:, :] = u
    conv = (cw_ref[0:1, :] * ubuf[6:6 + tm, :] + cw_ref[1:2, :] * ubuf[7:7 + tm, :]
            + cw_ref[2:3, :] * u)
    bo_ref[...] = (b_gate * conv).astype(BF16)
    ubuf[0:8, :] = ubuf[tm:tm + 8, :]


def _even_inproj(x, g, shift, scale, w_in, cos, sin, conv_w):
    bsz, seq, d = x.shape
    n = w_in.shape[1]
    cwid = conv_w.shape[1]
    tm = ROW_TILE
    row = lambda w: pl.BlockSpec((None, tm, w), lambda b, i: (b, i, 0))
    per_b = pl.BlockSpec((None, 1, d), lambda b, i: (b, 0, 0))
    return pl.pallas_call(
        _inproj_kernel,
        grid=(bsz, seq // tm),
        in_specs=[row(d), pl.BlockSpec((1, d), lambda b, i: (0, 0)), per_b, per_b,
                  pl.BlockSpec((d, n), lambda b, i: (0, 0)),
                  row(LANES), row(LANES),
                  pl.BlockSpec(conv_w.shape, lambda b, i: (0, 0))],
        out_specs=[row(ATTN_WIDTH)] * 3 + [row(cwid)],
        out_shape=[jax.ShapeDtypeStruct((bsz, seq, ATTN_WIDTH), BF16)] * 3
        + [jax.ShapeDtypeStruct((bsz, seq, cwid), BF16)],
        scratch_shapes=[pltpu.VMEM((tm + 8, cwid), F32)],
        compiler_params=_params(2),
        name="even_inproj",
    )(x, g.reshape(1, d), shift.reshape(bsz, 1, d), scale.reshape(bsz, 1, d),
      w_in, cos, sin, conv_w)


def _attn_kernel(*refs, first, last, gb):
    q_ref, kc_ref, vc_ref, kp_ref, vp_ref = refs[:5]
    refs = refs[5:]
    if not first:
        acc_in, ml_in = refs[:2]
        refs = refs[2:]
    if last:
        (a_out,) = refs
    else:
        acc_out, ml_out = refs

    blk = ATTN_BLOCK
    n0 = pl.program_id(1) * gb
    row = lax.broadcasted_iota(jnp.int32, (blk, 2 * blk), 0)
    col = lax.broadcasted_iota(jnp.int32, (blk, 2 * blk), 1)
    band = (col >= row) & (col <= row + blk)
    lane = lax.broadcasted_iota(jnp.int32, (blk, LANES), 1)
    lane_kv = lax.broadcasted_iota(jnp.int32, (2 * blk, LANES), 1)
    neg_inf = jnp.float32(-jnp.inf)

    for i in range(gb):
        if i == 0:
            k_prev, v_prev = kp_ref[...], vp_ref[...]
            valid = band & ((col >= blk) | (n0 > 0))
        else:
            k_prev, v_prev = kc_ref[i - 1], vc_ref[i - 1]
            valid = band
        kk = jnp.concatenate([k_prev, kc_ref[i]], axis=0)
        vv = jnp.concatenate([v_prev, vc_ref[i]], axis=0)
        q = q_ref[i]
        ml_old = None if first else ml_in[i]
        ml_new = jnp.zeros((blk, LANES), F32)
        for j in range(ATTN_WIDTH // LANES):
            sl = slice(j * LANES, (j + 1) * LANES)
            qj, kj, vj = q[:, sl], kk[:, sl], vv[:, sl]
            pv, alpha, linv = [], [], []
            for hh in range(LANES // HEAD_DIM):
                head = j * (LANES // HEAD_DIM) + hh
                in_head = (lane // HEAD_DIM) == hh
                in_head_kv = (lane_kv // HEAD_DIM) == hh
                qm = jnp.where(in_head, qj, jnp.zeros_like(qj))
                s = lax.dot_general(qm, kj, (((1,), (1,)), ((), ())),
                                    preferred_element_type=F32)
                s = jnp.where(valid, s, neg_inf)
                m_new = jnp.max(s, axis=1, keepdims=True)
                if not first:
                    m_old = ml_old[:, head:head + 1]
                    l_old = ml_old[:, ATTN_HEADS + head:ATTN_HEADS + head + 1]
                    m_new = jnp.maximum(m_old, m_new)
                    a = jnp.exp(m_old - m_new)
                    alpha.append(a)
                p = jnp.exp(s - m_new)
                l_new = jnp.sum(p, axis=1, keepdims=True)
                if not first:
                    l_new = a * l_old + l_new
                vm = jnp.where(in_head_kv, vj, jnp.zeros_like(vj))
                pv.append(_dot(p.astype(BF16), vm))
                linv.append(1.0 / l_new)
                ml_new = jnp.where(lane == head, m_new, ml_new)
                ml_new = jnp.where(lane == ATTN_HEADS + head, l_new, ml_new)
            acc = pv[0] + pv[1]
            if not first:
                acc = jnp.where(lane < HEAD_DIM, alpha[0], alpha[1]) * acc_in[i, :, sl] + acc
            if last:
                a_out[i, :, sl] = (acc * jnp.where(lane < HEAD_DIM, linv[0], linv[1])).astype(BF16)
            else:
                acc_out[i, :, sl] = acc
        if not last:
            ml_out[i] = ml_new


def _attn_branch(q, k, v, state, dil, first, last):
    bsz, seq, aw = q.shape
    blk = ATTN_BLOCK
    nb = seq // (blk * dil)
    gb = min(ATTN_GROUP, nb)
    view = lambda t, w: t.reshape(bsz, nb, blk, dil * w)
    cur = lambda w: pl.BlockSpec((None, gb, blk, w), lambda b, n, r: (b, n, 0, r))
    prev = pl.BlockSpec((None, None, blk, aw),
                        lambda b, n, r: (b, jnp.maximum(n * gb - 1, 0), 0, r))
    ins = [view(q, aw), view(k, aw), view(v, aw), view(k, aw), view(v, aw)]
    in_specs = [cur(aw), cur(aw), cur(aw), prev, prev]
    if not first:
        acc, ml = state
        ins += [view(acc, aw), view(ml, LANES)]
        in_specs += [cur(aw), cur(LANES)]
    if last:
        out_specs = [cur(aw)]
        out_shape = [jax.ShapeDtypeStruct((bsz, nb, blk, dil * aw), BF16)]
    else:
        out_specs = [cur(aw), cur(LANES)]
        out_shape = [jax.ShapeDtypeStruct((bsz, nb, blk, dil * aw), F32),
                     jax.ShapeDtypeStruct((bsz, nb, blk, dil * LANES), F32)]
    outs = pl.pallas_call(
        functools.partial(_attn_kernel, first=first, last=last, gb=gb),
        grid=(bsz, nb // gb, dil),
        in_specs=in_specs, out_specs=out_specs, out_shape=out_shape,
        compiler_params=_params(3),
        name=f"attn_dil{dil}",
    )(*ins)
    if last:
        return outs[0].reshape(bsz, seq, aw)
    return outs[0].reshape(bsz, seq, aw), outs[1].reshape(bsz, seq, LANES)


def _dilated_attention(q, k, v):
    state = None
    for idx, (window, dil) in enumerate(DILATED_PAIRS):
        assert window // dil == ATTN_BLOCK and q.shape[1] % window == 0
        state = _attn_branch(q, k, v, state, dil, idx == 0, idx == len(DILATED_PAIRS) - 1)
    return state


def _outproj_kernel(a_ref, b_ref, x_ref, wa_ref, wb_ref, gm_ref, g_ref, sh_ref, sc_ref,
                    x1_ref, h_ref):
    mix = _dot(a_ref[...], wa_ref[...]) + _dot(b_ref[...], wb_ref[...])
    x1 = x_ref[...] + gm_ref[...] * mix
    x1_ref[...] = x1
    h_ref[...] = _ada_norm(x1, g_ref[...], sh_ref[...], sc_ref[...]).astype(BF16)


def _even_outproj(a, b, x, w_out, gate_m, g, shift, scale):
    bsz, seq, d = x.shape
    aw, bw = a.shape[2], b.shape[2]
    tm = ROW_TILE
    row = lambda w: pl.BlockSpec((None, tm, w), lambda bi, i: (bi, i, 0))
    per_b = pl.BlockSpec((None, 1, d), lambda bi, i: (bi, 0, 0))
    full = lambda s: pl.BlockSpec(s, lambda bi, i: (0, 0))
    pb = lambda t: t.reshape(bsz, 1, d)
    return pl.pallas_call(
        _outproj_kernel,
        grid=(bsz, seq // tm),
        in_specs=[row(aw), row(bw), row(d), full((aw, d)), full((bw, d)),
                  per_b, full((1, d)), per_b, per_b],
        out_specs=[row(d), row(d)],
        out_shape=[jax.ShapeDtypeStruct((bsz, seq, d), F32),
                   jax.ShapeDtypeStruct((bsz, seq, d), BF16)],
        compiler_params=_params(2),
        name="even_outproj",
    )(a, b, x, w_out[:aw], w_out[aw:], pb(gate_m), g.reshape(1, d), pb(shift), pb(scale))


def _ffn_kernel(h_ref, x_ref, wg_ref, wu_ref, wd_ref, gf_ref, o_ref):
    h = h_ref[...]
    g = _dot(h, wg_ref[...])
    u = _dot(h, wu_ref[...])
    act = (g * _sigmoid(g) * u).astype(BF16)
    o_ref[...] = x_ref[...] + gf_ref[...] * _dot(act, wd_ref[...])


def _dense_ffn(h, x, w_gate, w_up, w_down, gate_f):
    bsz, seq, d = x.shape
    ff = w_gate.shape[1]
    tm = ROW_TILE
    row = pl.BlockSpec((None, tm, d), lambda b, i: (b, i, 0))
    full = lambda s: pl.BlockSpec(s, lambda b, i: (0, 0), pipeline_mode=pl.Buffered(1))
    return pl.pallas_call(
        _ffn_kernel,
        grid=(bsz, seq // tm),
        in_specs=[row, row, full((d, ff)), full((d, ff)), full((ff, d)),
                  pl.BlockSpec((None, 1, d), lambda b, i: (b, 0, 0))],
        out_specs=row,
        out_shape=jax.ShapeDtypeStruct((bsz, seq, d), F32),
        compiler_params=_params(2),
        name="dense_ffn",
    )(h, x, w_gate, w_up, w_down, gate_f.reshape(bsz, 1, d))


def _norm_kernel(x_ref, g_ref, sh_ref, sc_ref, h_ref):
    h_ref[...] = _ada_norm(x_ref[...], g_ref[...], sh_ref[...], sc_ref[...]).astype(BF16)


def _ada_norm_call(x, g, shift, scale):
    bsz, seq, d = x.shape
    tm = ROW_TILE
    row = pl.BlockSpec((None, tm, d), lambda b, i: (b, i, 0))
    per_b = pl.BlockSpec((None, 1, d), lambda b, i: (b, 0, 0))
    return pl.pallas_call(
        _norm_kernel,
        grid=(bsz, seq // tm),
        in_specs=[row, pl.BlockSpec((1, d), lambda b, i: (0, 0)), per_b, per_b],
        out_specs=row,
        out_shape=jax.ShapeDtypeStruct((bsz, seq, d), BF16),
        compiler_params=_params(2),
        name="ada_norm",
    )(x, g.reshape(1, d), shift.reshape(bsz, 1, d), scale.reshape(bsz, 1, d))


def _ssm_operators(a_re, a_im, log_step, b_re, b_im, c_re, c_im):
    L = SSM_CHUNK
    a_re, a_im = a_re.astype(F32), a_im.astype(F32)
    dt = jnp.exp(log_step.astype(F32))[:, None]
    mag = jnp.exp(a_re * dt)
    abar_re = mag * jnp.cos(a_im * dt)
    abar_im = mag * jnp.sin(a_im * dt)
    den = a_re * a_re + a_im * a_im
    nr = abar_re - 1.0
    f_re = (nr * a_re + abar_im * a_im) / den
    f_im = (abar_im * a_re - nr * a_im) / den
    b_re, b_im = b_re.astype(F32), b_im.astype(F32)
    bb_re = f_re[..., None] * b_re - f_im[..., None] * b_im
    bb_im = f_re[..., None] * b_im + f_im[..., None] * b_re
    c_re, c_im = c_re.astype(F32), c_im.astype(F32)

    def step(carry, _):
        pr, pi = carry
        nxt = (pr * abar_re - pi * abar_im, pr * abar_im + pi * abar_re)
        return nxt, carry
    (pl_re, pl_im), (pw_re, pw_im) = lax.scan(
        step, (jnp.ones_like(abar_re), jnp.zeros_like(abar_re)), None, length=L)
    pw_re = jnp.concatenate([pw_re, pl_re[None]], axis=0)
    pw_im = jnp.concatenate([pw_im, pl_im[None]], axis=0)

    hi = lax.Precision.HIGHEST
    cp_re = c_re[None] * pw_re[:L, :, None, :] - c_im[None] * pw_im[:L, :, None, :]
    cp_im = c_re[None] * pw_im[:L, :, None, :] + c_im[None] * pw_re[:L, :, None, :]
    w = (jnp.einsum('tgcp,gpd->tgdc', cp_re, bb_re, precision=hi)
         - jnp.einsum('tgcp,gpd->tgdc', cp_im, bb_im, precision=hi))
    s_idx = jnp.arange(L)[:, None]
    t_idx = jnp.arange(L)[None, :]
    tau = t_idx - s_idx
    toep = jnp.where((tau >= 0)[:, :, None, None, None], w[jnp.clip(tau, 0, L - 1)], 0.0)
    n_g, n_c = a_re.shape[0], b_re.shape[2]
    toep = toep.transpose(2, 0, 3, 1, 4).reshape(n_g, L * n_c, L * n_c)

    rp_re, rp_im = pw_re[:L][::-1], pw_im[:L][::-1]
    so_re = rp_re[..., None] * bb_re[None] - rp_im[..., None] * bb_im[None]
    so_im = rp_re[..., None] * bb_im[None] + rp_im[..., None] * bb_re[None]
    to_rows = lambda t: t.transpose(1, 0, 3, 2).reshape(n_g, L * n_c, -1)
    s_out = jnp.concatenate([to_rows(so_re), to_rows(so_im)], axis=-1)
    s_out_sw = jnp.concatenate([to_rows(so_im), to_rows(so_re)], axis=-1)

    qp_re, qp_im = pw_re[1:], pw_im[1:]
    ci_re = c_re[None] * qp_re[:, :, None, :] - c_im[None] * qp_im[:, :, None, :]
    ci_im = c_re[None] * qp_im[:, :, None, :] + c_im[None] * qp_re[:, :, None, :]
    to_cols = lambda t: t.transpose(1, 3, 0, 2).reshape(n_g, -1, L * n_c)
    c_in = jnp.concatenate([to_cols(ci_re), -to_cols(ci_im)], axis=1)

    al_re, al_im = pw_re[L], pw_im[L]
    a1 = jnp.concatenate([al_re, al_re], axis=-1)[:, None, :]
    a2 = jnp.concatenate([-al_im, al_im], axis=-1)[:, None, :]
    return (toep.astype(BF16), s_out.astype(BF16), s_out_sw.astype(BF16),
            c_in.astype(BF16), a1, a2)


def _piece_transpose(arrs):
    n = len(arrs)
    piece = lax.broadcasted_iota(jnp.int32, (1, LANES), 1) // SSM_GROUP
    arrs = list(arrs)
    dist = n // 2
    while dist >= 1:
        keep = (piece & dist) == 0
        for i in range(n):
            if i & dist == 0:
                a, b = arrs[i], arrs[i + dist]
                arrs[i] = jnp.where(keep, a, pltpu.roll(b, dist * SSM_GROUP, 1))
                arrs[i + dist] = jnp.where(keep, pltpu.roll(a, LANES - dist * SSM_GROUP, 1), b)
        dist //= 2
    return arrs


def _ssm_kernel(u_ref, t_ref, s_ref, ssw_ref, cin_ref, a1_ref, a2_ref, y_ref,
                st_ref, stsw_ref, xs_ref, xc_ref, zc_ref):
    kt, L, bsz, _ = u_ref.shape
    rows = kt * bsz
    n_lg = SSM_LANE_GROUPS

    @pl.when(pl.program_id(1) == 0)
    def _():
        xc_ref[...] = jnp.zeros(xc_ref.shape, F32)
        zc_ref[...] = jnp.zeros(zc_ref.shape, F32)

    zs = [u_ref[:, s].reshape(rows, LANES).astype(F32) for s in range(L)]
    lo = _piece_transpose(zs[:n_lg])
    hi = _piece_transpose(zs[n_lg:])
    ys = []
    for gl in range(n_lg):
        v = jnp.concatenate([lo[gl], hi[gl]], axis=1).astype(BF16)
        st_ref[...] = _dot(v, s_ref[gl])
        stsw_ref[...] = _dot(v, ssw_ref[gl])
        a1 = jnp.broadcast_to(a1_ref[gl], (bsz, 2 * SSM_STATE))
        a2 = jnp.broadcast_to(a2_ref[gl], (bsz, 2 * SSM_STATE))

        def body(k, carry):
            x, z = carry
            r = pl.ds(pl.multiple_of(k * bsz, bsz), bsz)
            xs_ref[r, :] = x
            x_new = x * a1 + z * a2 + st_ref[r, :]
            z_new = z * a1 - x * a2 + stsw_ref[r, :]
            return x_new, z_new

        x, z = lax.fori_loop(0, kt, body, (xc_ref[gl], zc_ref[gl]), unroll=8)
        xc_ref[gl] = x
        zc_ref[gl] = z
        ys.append(_dot(v, t_ref[gl]) + _dot(xs_ref[...].astype(BF16), cin_ref[gl]))
    out_lo = _piece_transpose([y[:, :LANES] for y in ys])
    out_hi = _piece_transpose([y[:, LANES:] for y in ys])
    for t in range(n_lg):
        y_ref[:, t] = out_lo[t].astype(BF16).reshape(kt, bsz, LANES)
        y_ref[:, n_lg + t] = out_hi[t].astype(BF16).reshape(kt, bsz, LANES)


def _ssm_scan(h, ops):
    bsz, seq, d = h.shape
    toep, s_out, s_out_sw, c_in, a1, a2 = ops
    L, C, n_lg = SSM_CHUNK, SSM_GROUP, SSM_LANE_GROUPS
    assert L == 2 * n_lg
    n_k = seq // L
    kt = min(SSM_CHUNKS_PER_STEP, n_k)
    hv = h.transpose(1, 0, 2).reshape(n_k, L, bsz, d)
    act = pl.BlockSpec((kt, L, bsz, LANES), lambda o, k: (k, 0, 0, o))
    wblk = lambda s: pl.BlockSpec((n_lg,) + s, lambda o, k: (o, 0, 0))
    y = pl.pallas_call(
        _ssm_kernel,
        grid=(d // LANES, n_k // kt),
        in_specs=[act, wblk((L * C, L * C)), wblk((L * C, 2 * SSM_STATE)),
                  wblk((L * C, 2 * SSM_STATE)), wblk((2 * SSM_STATE, L * C)),
                  wblk((1, 2 * SSM_STATE)), wblk((1, 2 * SSM_STATE))],
        out_specs=act,
        out_shape=jax.ShapeDtypeStruct((n_k, L, bsz, d), BF16),
        scratch_shapes=[pltpu.VMEM((kt * bsz, 2 * SSM_STATE), F32)] * 3
        + [pltpu.VMEM((n_lg, bsz, 2 * SSM_STATE), F32)] * 2,
        compiler_params=_params(2),
        name="ssm_scan",
    )(hv, toep, s_out, s_out_sw, c_in, a1, a2)
    return y.reshape(seq, bsz, d).transpose(1, 0, 2)


def _gelu_tanh(x):
    return 0.5 * x * (1.0 + jnp.tanh(math.sqrt(2.0 / math.pi) * (x + 0.044715 * (x * x * x))))


def _ssm_out_kernel(y_ref, h_ref, x_ref, d_ref, w_ref, b_ref, gm_ref, g_ref, sh_ref, sc_ref,
                    rwh_ref, rwl_ref, rb_ref, x1_ref, h2_ref, route_ref):
    d = x_ref.shape[1]
    y = y_ref[...].astype(F32) + d_ref[...] * h_ref[...].astype(F32)
    z = _dot(_gelu_tanh(y).astype(BF16), w_ref[...]) + b_ref[...]
    mix = z[:, :d] * _sigmoid(z[:, d:])
    x1 = x_ref[...] + gm_ref[...] * mix
    x1_ref[...] = x1
    h2 = _ada_norm(x1, g_ref[...], sh_ref[...], sc_ref[...])
    h2_ref[...] = h2

    hh, hl = _split_bf16(h2)
    logits = _dot(hh, rwh_ref[...]) + _dot(hl, rwh_ref[...]) + _dot(hh, rwl_ref[...]) + rb_ref[...]
    lane = lax.broadcasted_iota(jnp.int32, logits.shape, 1)
    lane_f = lane.astype(F32)
    neg_inf = jnp.float32(-jnp.inf)
    logits = jnp.where(lane < N_EXPERTS, logits, neg_inf)
    m1 = jnp.max(logits, axis=1, keepdims=True)
    i1 = jnp.min(jnp.where(logits == m1, lane_f, float(LANES)), axis=1, keepdims=True)
    rest = jnp.where(lane_f == i1, neg_inf, logits)
    m2 = jnp.max(rest, axis=1, keepdims=True)
    i2 = jnp.min(jnp.where(rest == m2, lane_f, float(LANES)), axis=1, keepdims=True)
    e2 = jnp.exp(m2 - m1)
    g1 = 1.0 / (1.0 + e2)
    g2 = e2 / (1.0 + e2)
    route = jnp.where(lane == 0, i1, jnp.where(lane == 1, i2,
                      jnp.where(lane == 2, g1, jnp.where(lane == 3, g2, 0.0))))
    route_ref[...] = route


def _ssm_out(y, h, x, d_skip, glu_w, glu_b, gate_m, g, shift, scale, router_w, router_b):
    bsz, seq, d = x.shape
    tm = ROW_TILE
    row = lambda w: pl.BlockSpec((None, tm, w), lambda b, i: (b, i, 0))
    per_b = pl.BlockSpec((None, 1, d), lambda b, i: (b, 0, 0))
    full = lambda s: pl.BlockSpec(s, lambda b, i: (0, 0))
    pb = lambda t: t.reshape(bsz, 1, d)
    rw = jnp.zeros((d, LANES), F32).at[:, :N_EXPERTS].set(router_w)
    rwh = rw.astype(BF16)
    rwl = (rw - rwh.astype(F32)).astype(BF16)
    rb = jnp.zeros((1, LANES), F32).at[0, :N_EXPERTS].set(router_b)
    return pl.pallas_call(
        _ssm_out_kernel,
        grid=(bsz, seq // tm),
        in_specs=[row(d), row(d), row(d), full((1, d)), full((d, 2 * d)), full((1, 2 * d)),
                  per_b, full((1, d)), per_b, per_b,
                  full((d, LANES)), full((d, LANES)), full((1, LANES))],
        out_specs=[row(d), row(d), row(LANES)],
        out_shape=[jax.ShapeDtypeStruct((bsz, seq, d), F32),
                   jax.ShapeDtypeStruct((bsz, seq, d), F32),
                   jax.ShapeDtypeStruct((bsz, seq, LANES), F32)],
        compiler_params=_params(2),
        name="ssm_out",
    )(y, h, x, d_skip.reshape(1, d), glu_w, glu_b.reshape(1, 2 * d), pb(gate_m),
      g.reshape(1, d), pb(shift), pb(scale), rwh, rwl, rb)


def _routing_tables(route, tm):
    n = route.shape[0]
    e_flat = route[:, :2].astype(jnp.int32).reshape(-1)
    onehot = (e_flat[:, None] == jnp.arange(N_EXPERTS)[None, :]).astype(jnp.int32)
    incl = jnp.cumsum(onehot, axis=0)
    counts = incl[-1]
    rank = jnp.sum((incl - onehot) * onehot, axis=1)
    padded = ((counts + tm - 1) // tm) * tm
    ends = jnp.cumsum(padded)
    starts = ends - padded
    dest = starts[e_flat] + rank
    n_tiles = (2 * n) // tm + N_EXPERTS
    src = jnp.zeros((n_tiles * tm,), jnp.int32).at[dest].set(
        jnp.arange(2 * n, dtype=jnp.int32) // 2, unique_indices=True, mode="promise_in_bounds")
    tile_start = jnp.arange(n_tiles, dtype=jnp.int32) * tm
    tile_expert = jnp.sum((tile_start[:, None] >= ends[None, :]).astype(jnp.int32), axis=1)
    tile_valid = (tile_start < ends[-1]).astype(jnp.int32)
    tile_expert = jnp.minimum(tile_expert, N_EXPERTS - 1)
    dest = dest.reshape(n, 2)
    return src, tile_expert, tile_valid, dest[:, 0], dest[:, 1]


def _moe_kernel(te_ref, tv_ref, src_ref, src_next_ref, h_hbm, wg_ref, wu_ref, wd_ref, y_ref,
                hbuf, sem):
    t = pl.program_id(0)
    n_t = pl.num_programs(0)
    tm = hbuf.shape[1]
    ff = wg_ref.shape[1]
    fc = ff // MOE_FF_CHUNKS
    slot = t % 2

    def row_copy(slot_, r, src_row):
        return pltpu.make_async_copy(h_hbm.at[pl.ds(src_row, 1)], hbuf.at[slot_, pl.ds(r, 1)],
                                     sem.at[slot_])

    def issue_tile(idx_ref, slot_):
        def issue(r, c):
            row_copy(slot_, r, idx_ref[r]).start()
            return c
        lax.fori_loop(0, tm, issue, 0, unroll=8)

    @pl.when((t == 0) & (tv_ref[0] > 0))
    def _():
        issue_tile(src_ref, 0)

    nxt = jnp.minimum(t + 1, n_t - 1)

    @pl.when((t + 1 < n_t) & (tv_ref[nxt] > 0))
    def _():
        issue_tile(src_next_ref, 1 - slot)

    @pl.when(tv_ref[t] > 0)
    def _():
        def wait(r, c):
            row_copy(slot, r, 0).wait()
            return c
        lax.fori_loop(0, tm, wait, 0, unroll=8)

        h = hbuf[slot].astype(BF16)
        acc = jnp.zeros(y_ref.shape, F32)
        for c in range(MOE_FF_CHUNKS):
            g = _dot(h, wg_ref[:, c * fc:(c + 1) * fc])
            u = _dot(h, wu_ref[:, c * fc:(c + 1) * fc])
            act = (g * _sigmoid(g) * u).astype(BF16)
            acc = acc + _dot(act, wd_ref[c * fc:(c + 1) * fc, :])
        y_ref[...] = acc

    @pl.when(tv_ref[t] == 0)
    def _():
        y_ref[...] = jnp.zeros(y_ref.shape, F32)


def _moe_experts(h2, src, tile_expert, tile_valid, w_gate, w_up, w_down):
    n, d = h2.shape
    ff = w_gate.shape[2]
    tm = MOE_TILE
    n_tiles = tile_expert.shape[0]
    wspec = lambda s: pl.BlockSpec((None,) + s, lambda t, te, tv: (te[t], 0, 0),
                                   pipeline_mode=pl.Buffered(1))
    grid_spec = pltpu.PrefetchScalarGridSpec(
        num_scalar_prefetch=2,
        grid=(n_tiles,),
        in_specs=[pl.BlockSpec((tm,), lambda t, te, tv: (t,), memory_space=pltpu.SMEM),
                  pl.BlockSpec((tm,), lambda t, te, tv: (jnp.minimum(t + 1, n_tiles - 1),),
                               memory_space=pltpu.SMEM),
                  pl.BlockSpec(memory_space=pl.ANY),
                  wspec((d, ff)), wspec((d, ff)), wspec((ff, d))],
        out_specs=pl.BlockSpec((tm, d), lambda t, te, tv: (t, 0)),
        scratch_shapes=[pltpu.VMEM((2, tm, d), F32), pltpu.SemaphoreType.DMA((2,))],
    )
    return pl.pallas_call(
        _moe_kernel,
        grid_spec=grid_spec,
        out_shape=jax.ShapeDtypeStruct((n_tiles * tm, d), F32),
        compiler_params=_params(1),
        name="moe_experts",
    )(tile_expert, tile_valid, src, src, h2, w_gate, w_up, w_down)


def _combine_kernel(p1_ref, p2_ref, y_hbm, route_ref, x_ref, gf_ref, fg_ref, o_ref, ybuf, sem,
                    *, final):
    tm = x_ref.shape[0]

    def row_copy(slot, r, src_row):
        return pltpu.make_async_copy(y_hbm.at[pl.ds(src_row, 1)], ybuf.at[slot, pl.ds(r, 1)], sem)

    def issue(r, c):
        row_copy(0, r, p1_ref[r]).start()
        row_copy(1, r, p2_ref[r]).start()
        return c
    lax.fori_loop(0, tm, issue, 0, unroll=8)

    def wait(r, c):
        row_copy(0, r, 0).wait()
        row_copy(1, r, 0).wait()
        return c
    lax.fori_loop(0, tm, wait, 0, unroll=8)

    route = route_ref[...]
    ff = route[:, 2:3] * ybuf[0] + route[:, 3:4] * ybuf[1]
    x2 = x_ref[...] + gf_ref[...] * ff
    if final:
        ms = jnp.mean(x2 * x2, axis=-1, keepdims=True)
        x2 = x2 * lax.rsqrt(ms + RMS_EPS) * fg_ref[...]
    o_ref[...] = x2


def _moe_combine(y_sorted, pos1, pos2, route, x1, gate_f, final_g, final):
    bsz, seq, d = x1.shape
    n = bsz * seq
    tm = COMBINE_TILE
    per_batch = seq // tm
    smem = pl.BlockSpec((tm,), lambda i: (i,), memory_space=pltpu.SMEM)
    out = pl.pallas_call(
        functools.partial(_combine_kernel, final=final),
        grid=(n // tm,),
        in_specs=[smem, smem, pl.BlockSpec(memory_space=pl.ANY),
                  pl.BlockSpec((tm, LANES), lambda i: (i, 0)),
                  pl.BlockSpec((tm, d), lambda i: (i, 0)),
                  pl.BlockSpec((None, 1, d), lambda i: (i // per_batch, 0, 0)),
                  pl.BlockSpec((1, d), lambda i: (0, 0))],
        out_specs=pl.BlockSpec((tm, d), lambda i: (i, 0)),
        out_shape=jax.ShapeDtypeStruct((n, d), F32),
        scratch_shapes=[pltpu.VMEM((2, tm, d), F32), pltpu.SemaphoreType.DMA(())],
        compiler_params=_params(1),
        name="moe_combine",
    )(pos1, pos2, y_sorted, route.reshape(n, LANES), x1.reshape(n, d),
      gate_f.reshape(bsz, 1, d), final_g.reshape(1, d))
    return out.reshape(bsz, seq, d)


def _final_norm_kernel(x_ref, g_ref, o_ref):
    x = x_ref[...]
    ms = jnp.mean(x * x, axis=-1, keepdims=True)
    o_ref[...] = x * lax.rsqrt(ms + RMS_EPS) * g_ref[...]


def _final_norm(x, g):
    bsz, seq, d = x.shape
    tm = ROW_TILE
    row = pl.BlockSpec((None, tm, d), lambda b, i: (b, i, 0))
    return pl.pallas_call(
        _final_norm_kernel,
        grid=(bsz, seq // tm),
        in_specs=[row, pl.BlockSpec((1, d), lambda b, i: (0, 0))],
        out_specs=row,
        out_shape=jax.ShapeDtypeStruct((bsz, seq, d), F32),
        compiler_params=_params(2),
        name="final_norm",
    )(x, g.reshape(1, d))


def _even_layer(x, mods, norm_mix_g, norm_ffn_g, cos, sin, w_in, conv_w, w_out,
                w_gate, w_up, w_down):
    sh_m, sc_m, g_m, sh_f, sc_f, g_f = mods
    q, k, v, b_out = _even_inproj(x, norm_mix_g, sh_m, sc_m, w_in.astype(BF16), cos, sin, conv_w)
    a_out = _dilated_attention(q, k, v)
    x1, h2 = _even_outproj(a_out, b_out, x, w_out.astype(BF16), g_m, norm_ffn_g, sh_f, sc_f)
    return _dense_ffn(h2, x1, w_gate.astype(BF16), w_up.astype(BF16), w_down.astype(BF16), g_f)


def _odd_layer(x, mods, norm_mix_g, norm_ffn_g, ssm, d_skip, glu_w, glu_b,
               router_w, router_b, w_gate, w_up, w_down, final_g, final):
    sh_m, sc_m, g_m, sh_f, sc_f, g_f = mods
    bsz, seq, d = x.shape
    h = _ada_norm_call(x, norm_mix_g, sh_m, sc_m)
    y = _ssm_scan(h, _ssm_operators(*ssm))
    x1, h2, route = _ssm_out(y, h, x, d_skip, glu_w.astype(BF16), glu_b, g_m,
                             norm_ffn_g, sh_f, sc_f, router_w, router_b)
    route = route.reshape(bsz * seq, LANES)
    src, tile_expert, tile_valid, pos1, pos2 = _routing_tables(route, MOE_TILE)
    y_sorted = _moe_experts(h2.reshape(bsz * seq, d), src, tile_expert, tile_valid,
                            w_gate.astype(BF16), w_up.astype(BF16), w_down.astype(BF16))
    return _moe_combine(y_sorted, pos1, pos2, route, x1, g_f, final_g, final)


def kernel(x, c, positions, mod_w, mod_b, norm_mix_g, norm_ffn_g, ev_w_in, ev_conv_w, ev_w_out, ffn_w_gate, ffn_w_up, ffn_w_down, ssm_a_re, ssm_a_im, ssm_log_step, ssm_b_re, ssm_b_im, ssm_c_re, ssm_c_im, ssm_d, glu_w, glu_b, moe_router_w, moe_router_b, moe_w_gate, moe_w_up, moe_w_down, final_norm_g):
    depth = mod_w.shape[0]
    d = x.shape[2]
    mod = _modulation(c, mod_w, mod_b)
    cos, sin = _rope_tables(positions)
    for layer in range(depth):
        mods = [mod[layer, :, j * d:(j + 1) * d] for j in range(6)]
        i = layer // 2
        if layer % 2 == 0:
            x = _even_layer(x, mods, norm_mix_g[layer], norm_ffn_g[layer], cos, sin,
                            ev_w_in[i], ev_conv_w[i], ev_w_out[i],
                            ffn_w_gate[i], ffn_w_up[i], ffn_w_down[i])
            if layer == depth - 1:
                x = _final_norm(x, final_norm_g)
        else:
            ssm = (ssm_a_re[i], ssm_a_im[i], ssm_log_step[i], ssm_b_re[i], ssm_b_im[i],
                   ssm_c_re[i], ssm_c_im[i])
            x = _odd_layer(x, mods, norm_mix_g[layer], norm_ffn_g[layer], ssm, ssm_d[i],
                           glu_w[i], glu_b[i], moe_router_w[i], moe_router_b[i],
                           moe_w_gate[i], moe_w_up[i], moe_w_down[i],
                           final_norm_g, layer == depth - 1)
    return x
```

```python
import functools
import math

import jax
import jax.numpy as jnp
from jax import lax
from jax.experimental import pallas as pl
from jax.experimental.pallas import tpu as pltpu

F32 = jnp.float32
BF16 = jnp.bfloat16

ATTN_HEADS = 8
HEAD_DIM = 64
ATTN_WIDTH = ATTN_HEADS * HEAD_DIM
ROPE_DIM = HEAD_DIM // 4
ROPE_THETA = 500000.0
DILATED_PAIRS = ((128, 1), (512, 4), (2048, 16))
ATTN_BLOCK = 128
SSM_GROUP = 16
SSM_STATE = 64
SSM_CHUNK = 16
N_EXPERTS = 8
RMS_EPS = 1e-6

LANES = 128
VMEM_LIMIT_BYTES = 56 * 1024 * 1024

ROW_TILE = 512
MOE_TILE = 512
COMBINE_TILE = 256
MOE_FF_CHUNKS = 2
ATTN_GROUP = 4
SSM_CHUNKS_PER_STEP = 64
SSM_LANE_GROUPS = LANES // SSM_GROUP


def _params(n_axes, vmem=VMEM_LIMIT_BYTES):
    return pltpu.CompilerParams(
        dimension_semantics=("arbitrary",) * n_axes, vmem_limit_bytes=vmem)


def _dot(a, b):
    return jnp.dot(a, b, preferred_element_type=F32)


def _sigmoid(x):
    return 1.0 / (1.0 + jnp.exp(-x))


def _split_bf16(x):
    hi = x.astype(BF16)
    lo = (x - hi.astype(F32)).astype(BF16)
    return hi, lo


def _ada_norm(x, g, shift, scale):
    ms = jnp.mean(x * x, axis=-1, keepdims=True)
    return x * lax.rsqrt(ms + RMS_EPS) * g * (1.0 + scale) + shift


def _mod_kernel(c_ref, w_ref, b_ref, o_ref):
    c = c_ref[...]
    cond = c * _sigmoid(c)
    ch, cl = _split_bf16(cond)
    wh, wl = _split_bf16(w_ref[...])
    o_ref[...] = _dot(ch, wh) + _dot(cl, wh) + _dot(ch, wl) + b_ref[...]


def _modulation(c, mod_w, mod_b):
    depth, d, n = mod_w.shape
    bsz = c.shape[0]
    tn = 1024
    return pl.pallas_call(
        _mod_kernel,
        grid=(depth, n // tn),
        in_specs=[pl.BlockSpec((bsz, d), lambda l, j: (0, 0)),
                  pl.BlockSpec((None, d, tn), lambda l, j: (l, 0, j)),
                  pl.BlockSpec((None, 1, tn), lambda l, j: (l, 0, j))],
        out_specs=pl.BlockSpec((None, bsz, tn), lambda l, j: (l, 0, j)),
        out_shape=jax.ShapeDtypeStruct((depth, bsz, n), F32),
        compiler_params=_params(2),
        name="modulation",
    )(c, mod_w, mod_b.reshape(depth, 1, n))


def _rope_kernel(pos_ref, inv_ref, cos_ref, sin_ref):
    ang = pos_ref[...].astype(F32) * inv_ref[...]
    cos_ref[...] = jnp.cos(ang)
    sin_ref[...] = jnp.sin(ang)


def _rope_tables(positions):
    bsz, seq = positions.shape
    inv = ROPE_THETA ** (-jnp.arange(0, ROPE_DIM, 2, dtype=F32) / ROPE_DIM)
    lane = jnp.arange(LANES) % HEAD_DIM
    inv_lane = jnp.where(lane < ROPE_DIM, inv[lane % (ROPE_DIM // 2)], 0.0).reshape(1, LANES)
    tm = ROW_TILE
    spec = pl.BlockSpec((None, tm, LANES), lambda b, i: (b, i, 0))
    return pl.pallas_call(
        _rope_kernel,
        grid=(bsz, seq // tm),
        in_specs=[pl.BlockSpec((None, tm, 1), lambda b, i: (b, i, 0)),
                  pl.BlockSpec((1, LANES), lambda b, i: (0, 0))],
        out_specs=[spec, spec],
        out_shape=[jax.ShapeDtypeStruct((bsz, seq, LANES), F32)] * 2,
        compiler_params=_params(2),
        name="rope_tables",
    )(positions.reshape(bsz, seq, 1), inv_lane)


def _inproj_kernel(x_ref, g_ref, sh_ref, sc_ref, w_ref, cos_ref, sin_ref, cw_ref, *rest):
    n_pat = len(DILATED_PAIRS)
    qkv_refs = [rest[3 * p:3 * p + 3] for p in range(n_pat)]
    bo_ref, ubuf, stage = rest[3 * n_pat:]
    tm = x_ref.shape[0]
    aw = ATTN_WIDTH
    cwid = cw_ref.shape[1]
    h = _ada_norm(x_ref[...], g_ref[...], sh_ref[...], sc_ref[...]).astype(BF16)
    proj = _dot(h, w_ref[...])

    cos = cos_ref[...]
    sin = sin_ref[...]
    lane = lax.broadcasted_iota(jnp.int32, (tm, LANES), 1) % HEAD_DIM
    first_half = lane < ROPE_DIM // 2

    def rope(t):
        rot = jnp.where(first_half,
                        -pltpu.roll(t, LANES - ROPE_DIM // 2, 1),
                        pltpu.roll(t, ROPE_DIM // 2, 1))
        return t * cos + rot * sin

    n_col = aw // LANES
    for j in range(n_col):
        sl = slice(j * LANES, (j + 1) * LANES)
        stage[0, j] = rope(proj[:, sl]) * (HEAD_DIM ** -0.5)
        stage[1, j] = rope(proj[:, aw + j * LANES: aw + (j + 1) * LANES])
        stage[2, j] = proj[:, 2 * aw + j * LANES: 2 * aw + (j + 1) * LANES]
    for (_, dil), refs in zip(DILATED_PAIRS, qkv_refs):
        for which, ref in enumerate(refs):
            for j in range(n_col):
                sl = slice(j * LANES, (j + 1) * LANES)
                if dil == 1:
                    ref[:, sl] = stage[which, j].astype(BF16)
                else:
                    for r in range(dil):
                        ref[r, :, sl] = stage[which, j, pl.ds(r, tm // dil, stride=dil), :].astype(BF16)

    b_gate = proj[:, 3 * aw:3 * aw + cwid]
    c_gate = proj[:, 3 * aw + cwid:3 * aw + 2 * cwid]
    xin = proj[:, 3 * aw + 2 * cwid:]
    u = c_gate * xin

    @pl.when(pl.program_id(1) == 0)
    def _():
        ubuf[0:8, :] = jnp.zeros((8, cwid), F32)

    ubuf[8:, :] = u
    conv = (cw_ref[0:1, :] * ubuf[6:6 + tm, :] + cw_ref[1:2, :] * ubuf[7:7 + tm, :]
            + cw_ref[2:3, :] * u)
    bo_ref[...] = (b_gate * conv).astype(BF16)
    ubuf[0:8, :] = ubuf[tm:tm + 8, :]


def _even_inproj(x, g, shift, scale, w_in, cos, sin, conv_w):
    bsz, seq, d = x.shape
    n = w_in.shape[1]
    cwid = conv_w.shape[1]
    tm = ROW_TILE
    aw = ATTN_WIDTH
    row = lambda w: pl.BlockSpec((None, tm, w), lambda b, i: (b, i, 0))
    per_b = pl.BlockSpec((None, 1, d), lambda b, i: (b, 0, 0))
    qkv_specs, qkv_shapes = [], []
    for window, dil in DILATED_PAIRS:
        assert window // dil == ATTN_BLOCK and seq % window == 0 and (dil == 1 or window % tm == 0)
        if dil == 1:
            spec, shape = row(aw), (bsz, seq, aw)
        else:
            per_span = window // tm
            spec = pl.BlockSpec((None, None, dil, tm // dil, aw),
                                lambda b, i, per_span=per_span: (b, i // per_span, 0, i % per_span, 0))
            shape = (bsz, seq // window, dil, ATTN_BLOCK, aw)
        qkv_specs += [spec] * 3
        qkv_shapes += [jax.ShapeDtypeStruct(shape, BF16)] * 3
    outs = pl.pallas_call(
        _inproj_kernel,
        grid=(bsz, seq // tm),
        in_specs=[row(d), pl.BlockSpec((1, d), lambda b, i: (0, 0)), per_b, per_b,
                  pl.BlockSpec((d, n), lambda b, i: (0, 0)),
                  row(LANES), row(LANES),
                  pl.BlockSpec(conv_w.shape, lambda b, i: (0, 0))],
        out_specs=qkv_specs + [row(cwid)],
        out_shape=qkv_shapes + [jax.ShapeDtypeStruct((bsz, seq, cwid), BF16)],
        scratch_shapes=[pltpu.VMEM((tm + 8, cwid), F32),
                        pltpu.VMEM((3, aw // LANES, tm, LANES), F32)],
        compiler_params=_params(2),
        name="even_inproj",
    )(x, g.reshape(1, d), shift.reshape(bsz, 1, d), scale.reshape(bsz, 1, d),
      w_in, cos, sin, conv_w)
    qkv = [outs[3 * p:3 * p + 3] for p in range(len(DILATED_PAIRS))]
    return qkv, outs[-1]


def _attn_item(q, kk, vv, valid, ml_old, acc_old):
    blk = ATTN_BLOCK
    lane = lax.broadcasted_iota(jnp.int32, (blk, LANES), 1)
    lane_kv = lax.broadcasted_iota(jnp.int32, (2 * blk, LANES), 1)
    neg_inf = jnp.float32(-jnp.inf)
    first = ml_old is None
    ml_new = jnp.zeros((blk, LANES), F32)
    accs, linvs = [], []
    for j in range(ATTN_WIDTH // LANES):
        sl = slice(j * LANES, (j + 1) * LANES)
        qj, kj, vj = q[:, sl], kk[:, sl], vv[:, sl]
        pv, alpha, linv = [], [], []
        for hh in range(LANES // HEAD_DIM):
            head = j * (LANES // HEAD_DIM) + hh
            in_head = (lane // HEAD_DIM) == hh
            in_head_kv = (lane_kv // HEAD_DIM) == hh
            qm = jnp.where(in_head, qj, jnp.zeros_like(qj))
            s = lax.dot_general(qm, kj, (((1,), (1,)), ((), ())), preferred_element_type=F32)
            s = jnp.where(valid, s, neg_inf)
            m_new = jnp.max(s, axis=1, keepdims=True)
            if not first:
                m_old = ml_old[:, head:head + 1]
                l_old = ml_old[:, ATTN_HEADS + head:ATTN_HEADS + head + 1]
                m_new = jnp.maximum(m_old, m_new)
                a = jnp.exp(m_old - m_new)
                alpha.append(a)
            p = jnp.exp(s - m_new)
            l_new = jnp.sum(p, axis=1, keepdims=True)
            if not first:
                l_new = a * l_old + l_new
            vm = jnp.where(in_head_kv, vj, jnp.zeros_like(vj))
            pv.append(_dot(p.astype(BF16), vm))
            linv.append(1.0 / l_new)
            ml_new = jnp.where(lane == head, m_new, ml_new)
            ml_new = jnp.where(lane == ATTN_HEADS + head, l_new, ml_new)
        acc = pv[0] + pv[1]
        if not first:
            acc = jnp.where(lane < HEAD_DIM, alpha[0], alpha[1]) * acc_old[j] + acc
        accs.append(acc)
        linvs.append(jnp.where(lane < HEAD_DIM, linv[0], linv[1]))
    return accs, ml_new, linvs


def _band_mask(has_prev):
    blk = ATTN_BLOCK
    row = lax.broadcasted_iota(jnp.int32, (blk, 2 * blk), 0)
    col = lax.broadcasted_iota(jnp.int32, (blk, 2 * blk), 1)
    band = (col >= row) & (col <= row + blk)
    return band, band & ((col >= blk) | has_prev)


def _col_blocks(ref_or_val):
    return [ref_or_val[:, j * LANES:(j + 1) * LANES] for j in range(ATTN_WIDTH // LANES)]


def _attn_wide_kernel(q_ref, kc_ref, vc_ref, kp_ref, vp_ref, acc_out, ml_out):
    _, valid = _band_mask(pl.program_id(1) > 0)
    for r in range(q_ref.shape[0]):
        kk = jnp.concatenate([kp_ref[r], kc_ref[r]], axis=0)
        vv = jnp.concatenate([vp_ref[r], vc_ref[r]], axis=0)
        accs, ml_new, _ = _attn_item(q_ref[r], kk, vv, valid, None, None)
        for j, a in enumerate(accs):
            acc_out[r, :, j * LANES:(j + 1) * LANES] = a
        ml_out[r] = ml_new


def _attn_mid_kernel(q_ref, kc_ref, vc_ref, kp_ref, vp_ref, accw_ref, mlw_ref, acc_out, ml_out,
                     tmp_acc, tmp_ml):
    dil = q_ref.shape[0]
    ratio = accw_ref.shape[0] // dil
    sub = accw_ref.shape[1]
    n_col = ATTN_WIDTH // LANES
    _, valid = _band_mask(pl.program_id(1) > 0)
    for r in range(dil):
        for qd in range(ratio):
            rows = pl.ds(qd, sub, stride=ratio)
            for j in range(n_col):
                tmp_acc[j, rows, :] = accw_ref[dil * qd + r, :, j * LANES:(j + 1) * LANES]
            tmp_ml[rows, :] = mlw_ref[dil * qd + r]
        kk = jnp.concatenate([kp_ref[r], kc_ref[r]], axis=0)
        vv = jnp.concatenate([vp_ref[r], vc_ref[r]], axis=0)
        accs, ml_new, _ = _attn_item(q_ref[r], kk, vv, valid, tmp_ml[...],
                                     [tmp_acc[j] for j in range(n_col)])
        rows = pl.ds(r, ATTN_BLOCK, stride=dil)
        for j, a in enumerate(accs):
            acc_out[j, rows, :] = a
        ml_out[rows, :] = ml_new


def _attn_last_kernel(q_ref, kc_ref, vc_ref, kp_ref, vp_ref, acc_in, ml_in, a_out):
    gb = q_ref.shape[0]
    band, valid0 = _band_mask(pl.program_id(1) > 0)
    for i in range(gb):
        if i == 0:
            k_prev, v_prev, valid = kp_ref[...], vp_ref[...], valid0
        else:
            k_prev, v_prev, valid = kc_ref[i - 1], vc_ref[i - 1], band
        kk = jnp.concatenate([k_prev, kc_ref[i]], axis=0)
        vv = jnp.concatenate([v_prev, vc_ref[i]], axis=0)
        rows = slice(i * ATTN_BLOCK, (i + 1) * ATTN_BLOCK)
        acc_old = [acc_in[j, rows, :] for j in range(ATTN_WIDTH // LANES)]
        accs, _, linvs = _attn_item(q_ref[i], kk, vv, valid, ml_in[i], acc_old)
        for j, (a, li) in enumerate(zip(accs, linvs)):
            a_out[i, :, j * LANES:(j + 1) * LANES] = (a * li).astype(BF16)


def _dilated_attention(qkv):
    (_, d1), (_, dm), (_, dw) = DILATED_PAIRS
    assert d1 == 1 and dw % dm == 0
    blk, aw = ATTN_BLOCK, ATTN_WIDTH
    (q1, k1, v1), (qm, km, vm), (qw, kw, vw) = qkv
    bsz, seq, _ = q1.shape

    nw = qw.shape[1]
    rpg = min(ATTN_GROUP, dw)
    cur = lambda w: pl.BlockSpec((None, None, rpg, blk, w), lambda b, n, g: (b, n, g, 0, 0))
    prev = pl.BlockSpec((None, None, rpg, blk, aw),
                        lambda b, n, g: (b, jnp.maximum(n - 1, 0), g, 0, 0))
    acc_w, ml_w = pl.pallas_call(
        _attn_wide_kernel,
        grid=(bsz, nw, dw // rpg),
        in_specs=[cur(aw), cur(aw), cur(aw), prev, prev],
        out_specs=[cur(aw), cur(LANES)],
        out_shape=[jax.ShapeDtypeStruct((bsz, nw, dw, blk, aw), F32),
                   jax.ShapeDtypeStruct((bsz, nw, dw, blk, LANES), F32)],
        compiler_params=_params(3),
        name=f"attn_dil{dw}",
    )(qw, kw, vw, kw, vw)

    nm = qm.shape[1]
    ratio = dw // dm
    sub = blk // ratio
    cur = pl.BlockSpec((None, None, dm, blk, aw), lambda b, n: (b, n, 0, 0, 0))
    prev = pl.BlockSpec((None, None, dm, blk, aw),
                        lambda b, n: (b, jnp.maximum(n - 1, 0), 0, 0, 0))
    wide = lambda w: pl.BlockSpec((None, None, dw, sub, w),
                                  lambda b, n: (b, n // ratio, 0, n % ratio, 0))
    n_col = aw // LANES
    acc, ml = pl.pallas_call(
        _attn_mid_kernel,
        grid=(bsz, nm),
        in_specs=[cur, cur, cur, prev, prev, wide(aw), wide(LANES)],
        out_specs=[pl.BlockSpec((None, n_col, dm * blk, LANES), lambda b, n: (b, 0, n, 0)),
                   pl.BlockSpec((None, dm * blk, LANES), lambda b, n: (b, n, 0))],
        out_shape=[jax.ShapeDtypeStruct((bsz, n_col, seq, LANES), F32),
                   jax.ShapeDtypeStruct((bsz, seq, LANES), F32)],
        scratch_shapes=[pltpu.VMEM((n_col, blk, LANES), F32), pltpu.VMEM((blk, LANES), F32)],
        compiler_params=_params(2),
        name=f"attn_dil{dm}",
    )(qm, km, vm, km, vm, acc_w, ml_w)

    nb = seq // blk
    gb = min(ATTN_GROUP, nb)
    view = lambda t: t.reshape(bsz, nb, blk, t.shape[-1])
    cur = lambda w: pl.BlockSpec((None, gb, blk, w), lambda b, n: (b, n, 0, 0))
    prev = pl.BlockSpec((None, None, blk, aw), lambda b, n: (b, jnp.maximum(n * gb - 1, 0), 0, 0))
    a_out = pl.pallas_call(
        _attn_last_kernel,
        grid=(bsz, nb // gb),
        in_specs=[cur(aw), cur(aw), cur(aw), prev, prev,
                  pl.BlockSpec((None, n_col, gb * blk, LANES), lambda b, n: (b, 0, n, 0)),
                  cur(LANES)],
        out_specs=cur(aw),
        out_shape=jax.ShapeDtypeStruct((bsz, nb, blk, aw), BF16),
        compiler_params=_params(2),
        name=f"attn_dil{d1}",
    )(view(q1), view(k1), view(v1), view(k1), view(v1), acc, view(ml))
    return a_out.reshape(bsz, seq, aw)


def _outproj_kernel(a_ref, b_ref, x_ref, wa_ref, wb_ref, gm_ref, g_ref, sh_ref, sc_ref,
                    x1_ref, h_ref):
    mix = _dot(a_ref[...], wa_ref[...]) + _dot(b_ref[...], wb_ref[...])
    x1 = x_ref[...] + gm_ref[...] * mix
    x1_ref[...] = x1
    h_ref[...] = _ada_norm(x1, g_ref[...], sh_ref[...], sc_ref[...]).astype(BF16)


def _even_outproj(a, b, x, w_out, gate_m, g, shift, scale):
    bsz, seq, d = x.shape
    aw, bw = a.shape[2], b.shape[2]
    tm = ROW_TILE
    row = lambda w: pl.BlockSpec((None, tm, w), lambda bi, i: (bi, i, 0))
    per_b = pl.BlockSpec((None, 1, d), lambda bi, i: (bi, 0, 0))
    full = lambda s: pl.BlockSpec(s, lambda bi, i: (0, 0))
    pb = lambda t: t.reshape(bsz, 1, d)
    return pl.pallas_call(
        _outproj_kernel,
        grid=(bsz, seq // tm),
        in_specs=[row(aw), row(bw), row(d), full((aw, d)), full((bw, d)),
                  per_b, full((1, d)), per_b, per_b],
        out_specs=[row(d), row(d)],
        out_shape=[jax.ShapeDtypeStruct((bsz, seq, d), F32),
                   jax.ShapeDtypeStruct((bsz, seq, d), BF16)],
        compiler_params=_params(2),
        name="even_outproj",
    )(a, b, x, w_out[:aw], w_out[aw:], pb(gate_m), g.reshape(1, d), pb(shift), pb(scale))


def _ffn_kernel(h_ref, x_ref, wg_ref, wu_ref, wd_ref, gf_ref, o_ref):
    h = h_ref[...]
    g = _dot(h, wg_ref[...])
    u = _dot(h, wu_ref[...])
    act = (g * _sigmoid(g) * u).astype(BF16)
    o_ref[...] = x_ref[...] + gf_ref[...] * _dot(act, wd_ref[...])


def _dense_ffn(h, x, w_gate, w_up, w_down, gate_f):
    bsz, seq, d = x.shape
    ff = w_gate.shape[1]
    tm = ROW_TILE
    row = pl.BlockSpec((None, tm, d), lambda b, i: (b, i, 0))
    full = lambda s: pl.BlockSpec(s, lambda b, i: (0, 0), pipeline_mode=pl.Buffered(1))
    return pl.pallas_call(
        _ffn_kernel,
        grid=(bsz, seq // tm),
        in_specs=[row, row, full((d, ff)), full((d, ff)), full((ff, d)),
                  pl.BlockSpec((None, 1, d), lambda b, i: (b, 0, 0))],
        out_specs=row,
        out_shape=jax.ShapeDtypeStruct((bsz, seq, d), F32),
        compiler_params=_params(2),
        name="dense_ffn",
    )(h, x, w_gate, w_up, w_down, gate_f.reshape(bsz, 1, d))


def _norm_kernel(x_ref, g_ref, sh_ref, sc_ref, h_ref):
    h_ref[...] = _ada_norm(x_ref[...], g_ref[...], sh_ref[...], sc_ref[...]).astype(BF16)


def _ada_norm_call(x, g, shift, scale):
    bsz, seq, d = x.shape
    tm = ROW_TILE
    row = pl.BlockSpec((None, tm, d), lambda b, i: (b, i, 0))
    per_b = pl.BlockSpec((None, 1, d), lambda b, i: (b, 0, 0))
    return pl.pallas_call(
        _norm_kernel,
        grid=(bsz, seq // tm),
        in_specs=[row, pl.BlockSpec((1, d), lambda b, i: (0, 0)), per_b, per_b],
        out_specs=row,
        out_shape=jax.ShapeDtypeStruct((bsz, seq, d), BF16),
        compiler_params=_params(2),
        name="ada_norm",
    )(x, g.reshape(1, d), shift.reshape(bsz, 1, d), scale.reshape(bsz, 1, d))


def _ssm_operators(a_re, a_im, log_step, b_re, b_im, c_re, c_im):
    L = SSM_CHUNK
    a_re, a_im = a_re.astype(F32), a_im.astype(F32)
    dt = jnp.exp(log_step.astype(F32))[:, None]
    mag = jnp.exp(a_re * dt)
    abar_re = mag * jnp.cos(a_im * dt)
    abar_im = mag * jnp.sin(a_im * dt)
    den = a_re * a_re + a_im * a_im
    nr = abar_re - 1.0
    f_re = (nr * a_re + abar_im * a_im) / den
    f_im = (abar_im * a_re - nr * a_im) / den
    b_re, b_im = b_re.astype(F32), b_im.astype(F32)
    bb_re = f_re[..., None] * b_re - f_im[..., None] * b_im
    bb_im = f_re[..., None] * b_im + f_im[..., None] * b_re
    c_re, c_im = c_re.astype(F32), c_im.astype(F32)

    def step(carry, _):
        pr, pi = carry
        nxt = (pr * abar_re - pi * abar_im, pr * abar_im + pi * abar_re)
        return nxt, carry
    (pl_re, pl_im), (pw_re, pw_im) = lax.scan(
        step, (jnp.ones_like(abar_re), jnp.zeros_like(abar_re)), None, length=L)
    pw_re = jnp.concatenate([pw_re, pl_re[None]], axis=0)
    pw_im = jnp.concatenate([pw_im, pl_im[None]], axis=0)

    hi = lax.Precision.HIGHEST
    cp_re = c_re[None] * pw_re[:L, :, None, :] - c_im[None] * pw_im[:L, :, None, :]
    cp_im = c_re[None] * pw_im[:L, :, None, :] + c_im[None] * pw_re[:L, :, None, :]
    w = (jnp.einsum('tgcp,gpd->tgdc', cp_re, bb_re, precision=hi)
         - jnp.einsum('tgcp,gpd->tgdc', cp_im, bb_im, precision=hi))
    s_idx = jnp.arange(L)[:, None]
    t_idx = jnp.arange(L)[None, :]
    tau = t_idx - s_idx
    toep = jnp.where((tau >= 0)[:, :, None, None, None], w[jnp.clip(tau, 0, L - 1)], 0.0)
    n_g, n_c = a_re.shape[0], b_re.shape[2]
    toep = toep.transpose(2, 0, 3, 1, 4).reshape(n_g, L * n_c, L * n_c)

    rp_re, rp_im = pw_re[:L][::-1], pw_im[:L][::-1]
    so_re = rp_re[..., None] * bb_re[None] - rp_im[..., None] * bb_im[None]
    so_im = rp_re[..., None] * bb_im[None] + rp_im[..., None] * bb_re[None]
    to_rows = lambda t: t.transpose(1, 0, 3, 2).reshape(n_g, L * n_c, -1)
    s_out = jnp.concatenate([to_rows(so_re), to_rows(so_im)], axis=-1)
    s_out_sw = jnp.concatenate([to_rows(so_im), to_rows(so_re)], axis=-1)

    qp_re, qp_im = pw_re[1:], pw_im[1:]
    ci_re = c_re[None] * qp_re[:, :, None, :] - c_im[None] * qp_im[:, :, None, :]
    ci_im = c_re[None] * qp_im[:, :, None, :] + c_im[None] * qp_re[:, :, None, :]
    to_cols = lambda t: t.transpose(1, 3, 0, 2).reshape(n_g, -1, L * n_c)
    c_in = jnp.concatenate([to_cols(ci_re), -to_cols(ci_im)], axis=1)

    al_re, al_im = pw_re[L], pw_im[L]
    a1 = jnp.concatenate([al_re, al_re], axis=-1)[:, None, :]
    a2 = jnp.concatenate([-al_im, al_im], axis=-1)[:, None, :]
    return (toep.astype(BF16), s_out.astype(BF16), s_out_sw.astype(BF16),
            c_in.astype(BF16), a1, a2)


def _piece_transpose(arrs):
    n = len(arrs)
    piece = lax.broadcasted_iota(jnp.int32, (1, LANES), 1) // SSM_GROUP
    arrs = list(arrs)
    dist = n // 2
    while dist >= 1:
        keep = (piece & dist) == 0
        for i in range(n):
            if i & dist == 0:
                a, b = arrs[i], arrs[i + dist]
                arrs[i] = jnp.where(keep, a, pltpu.roll(b, dist * SSM_GROUP, 1))
                arrs[i + dist] = jnp.where(keep, pltpu.roll(a, LANES - dist * SSM_GROUP, 1), b)
        dist //= 2
    return arrs


def _ssm_kernel(u_ref, t_ref, s_ref, ssw_ref, cin_ref, a1_ref, a2_ref, y_ref,
                st_ref, stsw_ref, xs_ref, xc_ref, zc_ref):
    kt, L, bsz, _ = u_ref.shape
    rows = kt * bsz
    n_lg = SSM_LANE_GROUPS

    @pl.when(pl.program_id(1) == 0)
    def _():
        xc_ref[...] = jnp.zeros(xc_ref.shape, F32)
        zc_ref[...] = jnp.zeros(zc_ref.shape, F32)

    zs = [u_ref[:, s].reshape(rows, LANES).astype(F32) for s in range(L)]
    lo = _piece_transpose(zs[:n_lg])
    hi = _piece_transpose(zs[n_lg:])
    ys = []
    for gl in range(n_lg):
        v = jnp.concatenate([lo[gl], hi[gl]], axis=1).astype(BF16)
        st_ref[...] = _dot(v, s_ref[gl])
        stsw_ref[...] = _dot(v, ssw_ref[gl])
        a1 = jnp.broadcast_to(a1_ref[gl], (bsz, 2 * SSM_STATE))
        a2 = jnp.broadcast_to(a2_ref[gl], (bsz, 2 * SSM_STATE))

        def body(k, carry):
            x, z = carry
            r = pl.ds(pl.multiple_of(k * bsz, bsz), bsz)
            xs_ref[r, :] = x
            x_new = x * a1 + z * a2 + st_ref[r, :]
            z_new = z * a1 - x * a2 + stsw_ref[r, :]
            return x_new, z_new

        x, z = lax.fori_loop(0, kt, body, (xc_ref[gl], zc_ref[gl]), unroll=8)
        xc_ref[gl] = x
        zc_ref[gl] = z
        ys.append(_dot(v, t_ref[gl]) + _dot(xs_ref[...].astype(BF16), cin_ref[gl]))
    out_lo = _piece_transpose([y[:, :LANES] for y in ys])
    out_hi = _piece_transpose([y[:, LANES:] for y in ys])
    for t in range(n_lg):
        y_ref[:, t] = out_lo[t].astype(BF16).reshape(kt, bsz, LANES)
        y_ref[:, n_lg + t] = out_hi[t].astype(BF16).reshape(kt, bsz, LANES)


def _ssm_scan(h, ops):
    bsz, seq, d = h.shape
    toep, s_out, s_out_sw, c_in, a1, a2 = ops
    L, C, n_lg = SSM_CHUNK, SSM_GROUP, SSM_LANE_GROUPS
    assert L == 2 * n_lg
    n_k = seq // L
    kt = min(SSM_CHUNKS_PER_STEP, n_k)
    hv = h.transpose(1, 0, 2).reshape(n_k, L, bsz, d)
    act = pl.BlockSpec((kt, L, bsz, LANES), lambda o, k: (k, 0, 0, o))
    wblk = lambda s: pl.BlockSpec((n_lg,) + s, lambda o, k: (o, 0, 0))
    y = pl.pallas_call(
        _ssm_kernel,
        grid=(d // LANES, n_k // kt),
        in_specs=[act, wblk((L * C, L * C)), wblk((L * C, 2 * SSM_STATE)),
                  wblk((L * C, 2 * SSM_STATE)), wblk((2 * SSM_STATE, L * C)),
                  wblk((1, 2 * SSM_STATE)), wblk((1, 2 * SSM_STATE))],
        out_specs=act,
        out_shape=jax.ShapeDtypeStruct((n_k, L, bsz, d), BF16),
        scratch_shapes=[pltpu.VMEM((kt * bsz, 2 * SSM_STATE), F32)] * 3
        + [pltpu.VMEM((n_lg, bsz, 2 * SSM_STATE), F32)] * 2,
        compiler_params=_params(2),
        name="ssm_scan",
    )(hv, toep, s_out, s_out_sw, c_in, a1, a2)
    return y.reshape(seq, bsz, d).transpose(1, 0, 2)


def _gelu_tanh(x):
    return 0.5 * x * (1.0 + jnp.tanh(math.sqrt(2.0 / math.pi) * (x + 0.044715 * (x * x * x))))


def _ssm_out_kernel(y_ref, h_ref, x_ref, d_ref, w_ref, b_ref, gm_ref, g_ref, sh_ref, sc_ref,
                    rwh_ref, rwl_ref, rb_ref, x1_ref, h2_ref, route_ref):
    d = x_ref.shape[1]
    y = y_ref[...].astype(F32) + d_ref[...] * h_ref[...].astype(F32)
    z = _dot(_gelu_tanh(y).astype(BF16), w_ref[...]) + b_ref[...]
    mix = z[:, :d] * _sigmoid(z[:, d:])
    x1 = x_ref[...] + gm_ref[...] * mix
    x1_ref[...] = x1
    h2 = _ada_norm(x1, g_ref[...], sh_ref[...], sc_ref[...])
    h2_ref[...] = h2

    hh, hl = _split_bf16(h2)
    logits = _dot(hh, rwh_ref[...]) + _dot(hl, rwh_ref[...]) + _dot(hh, rwl_ref[...]) + rb_ref[...]
    lane = lax.broadcasted_iota(jnp.int32, logits.shape, 1)
    lane_f = lane.astype(F32)
    neg_inf = jnp.float32(-jnp.inf)
    logits = jnp.where(lane < N_EXPERTS, logits, neg_inf)
    m1 = jnp.max(logits, axis=1, keepdims=True)
    i1 = jnp.min(jnp.where(logits == m1, lane_f, float(LANES)), axis=1, keepdims=True)
    rest = jnp.where(lane_f == i1, neg_inf, logits)
    m2 = jnp.max(rest, axis=1, keepdims=True)
    i2 = jnp.min(jnp.where(rest == m2, lane_f, float(LANES)), axis=1, keepdims=True)
    e2 = jnp.exp(m2 - m1)
    g1 = 1.0 / (1.0 + e2)
    g2 = e2 / (1.0 + e2)
    route = jnp.where(lane == 0, i1, jnp.where(lane == 1, i2,
                      jnp.where(lane == 2, g1, jnp.where(lane == 3, g2, 0.0))))
    route_ref[...] = route


def _ssm_out(y, h, x, d_skip, glu_w, glu_b, gate_m, g, shift, scale, router_w, router_b):
    bsz, seq, d = x.shape
    tm = ROW_TILE
    row = lambda w: pl.BlockSpec((None, tm, w), lambda b, i: (b, i, 0))
    per_b = pl.BlockSpec((None, 1, d), lambda b, i: (b, 0, 0))
    full = lambda s: pl.BlockSpec(s, lambda b, i: (0, 0))
    pb = lambda t: t.reshape(bsz, 1, d)
    rw = jnp.zeros((d, LANES), F32).at[:, :N_EXPERTS].set(router_w)
    rwh = rw.astype(BF16)
    rwl = (rw - rwh.astype(F32)).astype(BF16)
    rb = jnp.zeros((1, LANES), F32).at[0, :N_EXPERTS].set(router_b)
    return pl.pallas_call(
        _ssm_out_kernel,
        grid=(bsz, seq // tm),
        in_specs=[row(d), row(d), row(d), full((1, d)), full((d, 2 * d)), full((1, 2 * d)),
                  per_b, full((1, d)), per_b, per_b,
                  full((d, LANES)), full((d, LANES)), full((1, LANES))],
        out_specs=[row(d), row(d), row(LANES)],
        out_shape=[jax.ShapeDtypeStruct((bsz, seq, d), F32),
                   jax.ShapeDtypeStruct((bsz, seq, d), F32),
                   jax.ShapeDtypeStruct((bsz, seq, LANES), F32)],
        compiler_params=_params(2),
        name="ssm_out",
    )(y, h, x, d_skip.reshape(1, d), glu_w, glu_b.reshape(1, 2 * d), pb(gate_m),
      g.reshape(1, d), pb(shift), pb(scale), rwh, rwl, rb)


def _routing_tables(route, tm):
    n = route.shape[0]
    e_flat = route[:, :2].astype(jnp.int32).reshape(-1)
    onehot = (e_flat[:, None] == jnp.arange(N_EXPERTS)[None, :]).astype(jnp.int32)
    incl = jnp.cumsum(onehot, axis=0)
    counts = incl[-1]
    rank = jnp.sum((incl - onehot) * onehot, axis=1)
    padded = ((counts + tm - 1) // tm) * tm
    ends = jnp.cumsum(padded)
    starts = ends - padded
    dest = starts[e_flat] + rank
    n_tiles = (2 * n) // tm + N_EXPERTS
    src = jnp.zeros(((n_tiles + 1) * tm,), jnp.int32).at[dest].set(
        jnp.arange(2 * n, dtype=jnp.int32) // 2, unique_indices=True, mode="promise_in_bounds")
    tile_start = jnp.arange(n_tiles, dtype=jnp.int32) * tm
    tile_expert = jnp.sum((tile_start[:, None] >= ends[None, :]).astype(jnp.int32), axis=1)
    tile_expert = jnp.minimum(tile_expert, N_EXPERTS - 1)
    pos = jnp.concatenate([dest.reshape(n, 2), jnp.zeros((COMBINE_TILE, 2), jnp.int32)], axis=0)
    return src, tile_expert, pos[:, 0], pos[:, 1]


def _moe_kernel(te_ref, src_ref, src_next_ref, h_hbm, wg_ref, wu_ref, wd_ref, y_ref, hbuf, sem):
    t = pl.program_id(0)
    n_t = pl.num_programs(0)
    tm = hbuf.shape[1]
    ff = wg_ref.shape[1]
    fc = ff // MOE_FF_CHUNKS
    slot = t % 2

    def row_copy(slot_, r, src_row):
        return pltpu.make_async_copy(h_hbm.at[pl.ds(src_row, 1)], hbuf.at[slot_, pl.ds(r, 1)],
                                     sem.at[slot_])

    def wait_tile(slot_):
        def wait(r, c):
            row_copy(slot_, r, 0).wait()
            return c
        lax.fori_loop(0, tm, wait, 0, unroll=8)

    @pl.when(t == 0)
    def _():
        def issue(r, c):
            row_copy(0, r, src_ref[r]).start()
            return c
        lax.fori_loop(0, tm, issue, 0, unroll=8)

    for r in range(tm):
        row_copy(1 - slot, r, src_next_ref[r]).start()
    wait_tile(slot)

    h = hbuf[slot].astype(BF16)
    acc = jnp.zeros(y_ref.shape, F32)
    for c in range(MOE_FF_CHUNKS):
        g = _dot(h, wg_ref[:, c * fc:(c + 1) * fc])
        u = _dot(h, wu_ref[:, c * fc:(c + 1) * fc])
        act = (g * _sigmoid(g) * u).astype(BF16)
        acc = acc + _dot(act, wd_ref[c * fc:(c + 1) * fc, :])
    y_ref[...] = acc

    @pl.when(t == n_t - 1)
    def _():
        wait_tile(1 - slot)


def _moe_experts(h2, src, tile_expert, w_gate, w_up, w_down):
    n, d = h2.shape
    ff = w_gate.shape[2]
    tm = MOE_TILE
    n_tiles = tile_expert.shape[0]
    wspec = lambda s: pl.BlockSpec((None,) + s, lambda t, te: (te[t], 0, 0),
                                   pipeline_mode=pl.Buffered(1))
    grid_spec = pltpu.PrefetchScalarGridSpec(
        num_scalar_prefetch=1,
        grid=(n_tiles,),
        in_specs=[pl.BlockSpec((tm,), lambda t, te: (t,), memory_space=pltpu.SMEM),
                  pl.BlockSpec((tm,), lambda t, te: (t + 1,), memory_space=pltpu.SMEM),
                  pl.BlockSpec(memory_space=pl.ANY),
                  wspec((d, ff)), wspec((d, ff)), wspec((ff, d))],
        out_specs=pl.BlockSpec((tm, d), lambda t, te: (t, 0)),
        scratch_shapes=[pltpu.VMEM((2, tm, d), F32), pltpu.SemaphoreType.DMA((2,))],
    )
    return pl.pallas_call(
        _moe_kernel,
        grid_spec=grid_spec,
        out_shape=jax.ShapeDtypeStruct((n_tiles * tm, d), F32),
        compiler_params=_params(1),
        name="moe_experts",
    )(tile_expert, src, src, h2, w_gate, w_up, w_down)


def _combine_kernel(p1_ref, p2_ref, p1n_ref, p2n_ref, y_hbm, route_ref, x_ref, gf_ref, fg_ref,
                    o_ref, ybuf, sem, *, final):
    t = pl.program_id(0)
    n_t = pl.num_programs(0)
    tm = x_ref.shape[0]
    slot = t % 2

    def row_copy(slot_, which, r, src_row):
        return pltpu.make_async_copy(y_hbm.at[pl.ds(src_row, 1)],
                                     ybuf.at[slot_, which, pl.ds(r, 1)], sem.at[slot_])

    def wait_tile(slot_):
        def wait(r, c):
            row_copy(slot_, 0, r, 0).wait()
            row_copy(slot_, 1, r, 0).wait()
            return c
        lax.fori_loop(0, tm, wait, 0, unroll=8)

    @pl.when(t == 0)
    def _():
        def issue(r, c):
            row_copy(0, 0, r, p1_ref[r]).start()
            row_copy(0, 1, r, p2_ref[r]).start()
            return c
        lax.fori_loop(0, tm, issue, 0, unroll=8)

    for r in range(tm):
        row_copy(1 - slot, 0, r, p1n_ref[r]).start()
        row_copy(1 - slot, 1, r, p2n_ref[r]).start()
    wait_tile(slot)

    @pl.when(t == n_t - 1)
    def _():
        wait_tile(1 - slot)

    route = route_ref[...]
    ff = route[:, 2:3] * ybuf[slot, 0] + route[:, 3:4] * ybuf[slot, 1]
    x2 = x_ref[...] + gf_ref[...] * ff
    if final:
        ms = jnp.mean(x2 * x2, axis=-1, keepdims=True)
        x2 = x2 * lax.rsqrt(ms + RMS_EPS) * fg_ref[...]
    o_ref[...] = x2


def _moe_combine(y_sorted, pos1, pos2, route, x1, gate_f, final_g, final):
    bsz, seq, d = x1.shape
    n = bsz * seq
    tm = COMBINE_TILE
    per_batch = seq // tm
    smem = pl.BlockSpec((tm,), lambda i: (i,), memory_space=pltpu.SMEM)
    smem_next = pl.BlockSpec((tm,), lambda i: (i + 1,), memory_space=pltpu.SMEM)
    out = pl.pallas_call(
        functools.partial(_combine_kernel, final=final),
        grid=(n // tm,),
        in_specs=[smem, smem, smem_next, smem_next, pl.BlockSpec(memory_space=pl.ANY),
                  pl.BlockSpec((tm, LANES), lambda i: (i, 0)),
                  pl.BlockSpec((tm, d), lambda i: (i, 0)),
                  pl.BlockSpec((None, 1, d), lambda i: (i // per_batch, 0, 0)),
                  pl.BlockSpec((1, d), lambda i: (0, 0))],
        out_specs=pl.BlockSpec((tm, d), lambda i: (i, 0)),
        out_shape=jax.ShapeDtypeStruct((n, d), F32),
        scratch_shapes=[pltpu.VMEM((2, 2, tm, d), F32), pltpu.SemaphoreType.DMA((2,))],
        compiler_params=_params(1),
        name="moe_combine",
    )(pos1, pos2, pos1, pos2, y_sorted, route.reshape(n, LANES), x1.reshape(n, d),
      gate_f.reshape(bsz, 1, d), final_g.reshape(1, d))
    return out.reshape(bsz, seq, d)


def _final_norm_kernel(x_ref, g_ref, o_ref):
    x = x_ref[...]
    ms = jnp.mean(x * x, axis=-1, keepdims=True)
    o_ref[...] = x * lax.rsqrt(ms + RMS_EPS) * g_ref[...]


def _final_norm(x, g):
    bsz, seq, d = x.shape
    tm = ROW_TILE
    row = pl.BlockSpec((None, tm, d), lambda b, i: (b, i, 0))
    return pl.pallas_call(
        _final_norm_kernel,
        grid=(bsz, seq // tm),
        in_specs=[row, pl.BlockSpec((1, d), lambda b, i: (0, 0))],
        out_specs=row,
        out_shape=jax.ShapeDtypeStruct((bsz, seq, d), F32),
        compiler_params=_params(2),
        name="final_norm",
    )(x, g.reshape(1, d))


def _even_layer(x, mods, norm_mix_g, norm_ffn_g, cos, sin, w_in, conv_w, w_out,
                w_gate, w_up, w_down):
    sh_m, sc_m, g_m, sh_f, sc_f, g_f = mods
    qkv, b_out = _even_inproj(x, norm_mix_g, sh_m, sc_m, w_in.astype(BF16), cos, sin, conv_w)
    a_out = _dilated_attention(qkv)
    x1, h2 = _even_outproj(a_out, b_out, x, w_out.astype(BF16), g_m, norm_ffn_g, sh_f, sc_f)
    return _dense_ffn(h2, x1, w_gate.astype(BF16), w_up.astype(BF16), w_down.astype(BF16), g_f)


def _odd_layer(x, mods, norm_mix_g, norm_ffn_g, ssm, d_skip, glu_w, glu_b,
               router_w, router_b, w_gate, w_up, w_down, final_g, final):
    sh_m, sc_m, g_m, sh_f, sc_f, g_f = mods
    bsz, seq, d = x.shape
    h = _ada_norm_call(x, norm_mix_g, sh_m, sc_m)
    y = _ssm_scan(h, _ssm_operators(*ssm))
    x1, h2, route = _ssm_out(y, h, x, d_skip, glu_w.astype(BF16), glu_b, g_m,
                             norm_ffn_g, sh_f, sc_f, router_w, router_b)
    route = route.reshape(bsz * seq, LANES)
    src, tile_expert, pos1, pos2 = _routing_tables(route, MOE_TILE)
    y_sorted = _moe_experts(h2.reshape(bsz * seq, d), src, tile_expert,
                            w_gate.astype(BF16), w_up.astype(BF16), w_down.astype(BF16))
    return _moe_combine(y_sorted, pos1, pos2, route, x1, g_f, final_g, final)


def kernel(x, c, positions, mod_w, mod_b, norm_mix_g, norm_ffn_g, ev_w_in, ev_conv_w, ev_w_out, ffn_w_gate, ffn_w_up, ffn_w_down, ssm_a_re, ssm_a_im, ssm_log_step, ssm_b_re, ssm_b_im, ssm_c_re, ssm_c_im, ssm_d, glu_w, glu_b, moe_router_w, moe_router_b, moe_w_gate, moe_w_up, moe_w_down, final_norm_g):
    depth = mod_w.shape[0]
    d = x.shape[2]
    mod = _modulation(c, mod_w, mod_b)
    cos, sin = _rope_tables(positions)
    for layer in range(depth):
        mods = [mod[layer, :, j * d:(j + 1) * d] for j in range(6)]
        i = layer // 2
        if layer % 2 == 0:
            x = _even_layer(x, mods, norm_mix_g[layer], norm_ffn_g[layer], cos, sin,
                            ev_w_in[i], ev_conv_w[i], ev_w_out[i],
                            ffn_w_gate[i], ffn_w_up[i], ffn_w_down[i])
            if layer == depth - 1:
                x = _final_norm(x, final_norm_g)
        else:
            ssm = (ssm_a_re[i], ssm_a_im[i], ssm_log_step[i], ssm_b_re[i], ssm_b_im[i],
                   ssm_c_re[i], ssm_c_im[i])
            x = _odd_layer(x, mods, norm_mix_g[layer], norm_ffn_g[layer], ssm, ssm_d[i],
                           glu_w[i], glu_b[i], moe_router_w[i], moe_router_b[i],
                           moe_w_gate[i], moe_w_up[i], moe_w_down[i],
                           final_norm_g, layer == depth - 1)
    return x
```

```python
import functools
import math

import jax
import jax.numpy as jnp
from jax import lax
from jax.experimental import pallas as pl
from jax.experimental.pallas import tpu as pltpu

F32 = jnp.float32
BF16 = jnp.bfloat16

ATTN_HEADS = 8
HEAD_DIM = 64
ATTN_WIDTH = ATTN_HEADS * HEAD_DIM
ROPE_DIM = HEAD_DIM // 4
ROPE_THETA = 500000.0
DILATED_PAIRS = ((128, 1), (512, 4), (2048, 16))
ATTN_BLOCK = 128
SSM_GROUP = 16
SSM_STATE = 64
SSM_CHUNK = 16
N_EXPERTS = 8
RMS_EPS = 1e-6

LANES = 128
VMEM_LIMIT_BYTES = 56 * 1024 * 1024

ROW_TILE = 512
MOE_TILE = 512
COMBINE_TILE = 256
MOE_FF_CHUNKS = 2
ATTN_GROUP = 4
SSM_CHUNKS_PER_STEP = 64
SSM_LANE_GROUPS = LANES // SSM_GROUP


def _params(n_axes, vmem=VMEM_LIMIT_BYTES):
    return pltpu.CompilerParams(
        dimension_semantics=("arbitrary",) * n_axes, vmem_limit_bytes=vmem)


def _dot(a, b):
    return jnp.dot(a, b, preferred_element_type=F32)


def _sigmoid(x):
    return 1.0 / (1.0 + jnp.exp(-x))


def _split_bf16(x):
    hi = x.astype(BF16)
    lo = (x - hi.astype(F32)).astype(BF16)
    return hi, lo


def _ada_norm(x, g, shift, scale):
    ms = jnp.mean(x * x, axis=-1, keepdims=True)
    return x * lax.rsqrt(ms + RMS_EPS) * g * (1.0 + scale) + shift


def _mod_kernel(c_ref, w_ref, b_ref, o_ref):
    c = c_ref[...]
    cond = c * _sigmoid(c)
    ch, cl = _split_bf16(cond)
    wh, wl = _split_bf16(w_ref[...])
    o_ref[...] = _dot(ch, wh) + _dot(cl, wh) + _dot(ch, wl) + b_ref[...]


def _modulation(c, mod_w, mod_b):
    depth, d, n = mod_w.shape
    bsz = c.shape[0]
    tn = 1024
    return pl.pallas_call(
        _mod_kernel,
        grid=(depth, n // tn),
        in_specs=[pl.BlockSpec((bsz, d), lambda l, j: (0, 0)),
                  pl.BlockSpec((None, d, tn), lambda l, j: (l, 0, j)),
                  pl.BlockSpec((None, 1, tn), lambda l, j: (l, 0, j))],
        out_specs=pl.BlockSpec((None, bsz, tn), lambda l, j: (l, 0, j)),
        out_shape=jax.ShapeDtypeStruct((depth, bsz, n), F32),
        compiler_params=_params(2),
        name="modulation",
    )(c, mod_w, mod_b.reshape(depth, 1, n))


def _rope_kernel(pos_ref, inv_ref, cos_ref, sin_ref):
    ang = pos_ref[...].astype(F32) * inv_ref[...]
    cos_ref[...] = jnp.cos(ang)
    sin_ref[...] = jnp.sin(ang)


def _rope_tables(positions):
    bsz, seq = positions.shape
    inv = ROPE_THETA ** (-jnp.arange(0, ROPE_DIM, 2, dtype=F32) / ROPE_DIM)
    lane = jnp.arange(LANES) % HEAD_DIM
    inv_lane = jnp.where(lane < ROPE_DIM, inv[lane % (ROPE_DIM // 2)], 0.0).reshape(1, LANES)
    tm = ROW_TILE
    spec = pl.BlockSpec((None, tm, LANES), lambda b, i: (b, i, 0))
    return pl.pallas_call(
        _rope_kernel,
        grid=(bsz, seq // tm),
        in_specs=[pl.BlockSpec((None, tm, 1), lambda b, i: (b, i, 0)),
                  pl.BlockSpec((1, LANES), lambda b, i: (0, 0))],
        out_specs=[spec, spec],
        out_shape=[jax.ShapeDtypeStruct((bsz, seq, LANES), F32)] * 2,
        compiler_params=_params(2),
        name="rope_tables",
    )(positions.reshape(bsz, seq, 1), inv_lane)


def _inproj_kernel(x_ref, g_ref, sh_ref, sc_ref, w_ref, cos_ref, sin_ref, cw_ref, *rest):
    n_pat = len(DILATED_PAIRS)
    qkv_refs = [rest[3 * p:3 * p + 3] for p in range(n_pat)]
    bo_ref, ubuf, stage = rest[3 * n_pat:]
    tm = x_ref.shape[0]
    aw = ATTN_WIDTH
    cwid = cw_ref.shape[1]
    h = _ada_norm(x_ref[...], g_ref[...], sh_ref[...], sc_ref[...]).astype(BF16)
    proj = _dot(h, w_ref[...])

    cos = cos_ref[...]
    sin = sin_ref[...]
    lane = lax.broadcasted_iota(jnp.int32, (tm, LANES), 1) % HEAD_DIM
    first_half = lane < ROPE_DIM // 2

    def rope(t):
        rot = jnp.where(first_half,
                        -pltpu.roll(t, LANES - ROPE_DIM // 2, 1),
                        pltpu.roll(t, ROPE_DIM // 2, 1))
        return t * cos + rot * sin

    n_col = aw // LANES
    for j in range(n_col):
        sl = slice(j * LANES, (j + 1) * LANES)
        stage[0, j] = rope(proj[:, sl]) * (HEAD_DIM ** -0.5)
        stage[1, j] = rope(proj[:, aw + j * LANES: aw + (j + 1) * LANES])
        stage[2, j] = proj[:, 2 * aw + j * LANES: 2 * aw + (j + 1) * LANES]
    for (_, dil), refs in zip(DILATED_PAIRS, qkv_refs):
        for which, ref in enumerate(refs):
            for j in range(n_col):
                sl = slice(j * LANES, (j + 1) * LANES)
                if dil == 1:
                    ref[:, sl] = stage[which, j].astype(BF16)
                else:
                    for r in range(dil):
                        ref[r, :, sl] = stage[which, j, pl.ds(r, tm // dil, stride=dil), :].astype(BF16)

    b_gate = proj[:, 3 * aw:3 * aw + cwid]
    c_gate = proj[:, 3 * aw + cwid:3 * aw + 2 * cwid]
    xin = proj[:, 3 * aw + 2 * cwid:]
    u = c_gate * xin

    @pl.when(pl.program_id(1) == 0)
    def _():
        ubuf[0:8, :] = jnp.zeros((8, cwid), F32)

    ubuf[8:, :] = u
    conv = (cw_ref[0:1, :] * ubuf[6:6 + tm, :] + cw_ref[1:2, :] * ubuf[7:7 + tm, :]
            + cw_ref[2:3, :] * u)
    bo_ref[...] = (b_gate * conv).astype(BF16)
    ubuf[0:8, :] = ubuf[tm:tm + 8, :]


def _even_inproj(x, g, shift, scale, w_in, cos, sin, conv_w):
    bsz, seq, d = x.shape
    n = w_in.shape[1]
    cwid = conv_w.shape[1]
    tm = ROW_TILE
    aw = ATTN_WIDTH
    row = lambda w: pl.BlockSpec((None, tm, w), lambda b, i: (b, i, 0))
    per_b = pl.BlockSpec((None, 1, d), lambda b, i: (b, 0, 0))
    qkv_specs, qkv_shapes = [], []
    for window, dil in DILATED_PAIRS:
        assert window // dil == ATTN_BLOCK and seq % window == 0 and (dil == 1 or window % tm == 0)
        if dil == 1:
            spec, shape = row(aw), (bsz, seq, aw)
        else:
            per_span = window // tm
            spec = pl.BlockSpec((None, None, dil, tm // dil, aw),
                                lambda b, i, per_span=per_span: (b, i // per_span, 0, i % per_span, 0))
            shape = (bsz, seq // window, dil, ATTN_BLOCK, aw)
        qkv_specs += [spec] * 3
        qkv_shapes += [jax.ShapeDtypeStruct(shape, BF16)] * 3
    outs = pl.pallas_call(
        _inproj_kernel,
        grid=(bsz, seq // tm),
        in_specs=[row(d), pl.BlockSpec((1, d), lambda b, i: (0, 0)), per_b, per_b,
                  pl.BlockSpec((d, n), lambda b, i: (0, 0)),
                  row(LANES), row(LANES),
                  pl.BlockSpec(conv_w.shape, lambda b, i: (0, 0))],
        out_specs=qkv_specs + [row(cwid)],
        out_shape=qkv_shapes + [jax.ShapeDtypeStruct((bsz, seq, cwid), BF16)],
        scratch_shapes=[pltpu.VMEM((tm + 8, cwid), F32),
                        pltpu.VMEM((3, aw // LANES, tm, LANES), F32)],
        compiler_params=_params(2),
        name="even_inproj",
    )(x, g.reshape(1, d), shift.reshape(bsz, 1, d), scale.reshape(bsz, 1, d),
      w_in, cos, sin, conv_w)
    qkv = [outs[3 * p:3 * p + 3] for p in range(len(DILATED_PAIRS))]
    return qkv, outs[-1]


def _attn_item(q, kk, vv, valid, ml_old, acc_old):
    blk = ATTN_BLOCK
    lane = lax.broadcasted_iota(jnp.int32, (blk, LANES), 1)
    lane_kv = lax.broadcasted_iota(jnp.int32, (2 * blk, LANES), 1)
    neg_inf = jnp.float32(-jnp.inf)
    first = ml_old is None
    ml_new = jnp.zeros((blk, LANES), F32)
    accs, linvs = [], []
    for j in range(ATTN_WIDTH // LANES):
        sl = slice(j * LANES, (j + 1) * LANES)
        qj, kj, vj = q[:, sl], kk[:, sl], vv[:, sl]
        pv, alpha, linv = [], [], []
        for hh in range(LANES // HEAD_DIM):
            head = j * (LANES // HEAD_DIM) + hh
            in_head = (lane // HEAD_DIM) == hh
            in_head_kv = (lane_kv // HEAD_DIM) == hh
            qm = jnp.where(in_head, qj, jnp.zeros_like(qj))
            s = lax.dot_general(qm, kj, (((1,), (1,)), ((), ())), preferred_element_type=F32)
            s = jnp.where(valid, s, neg_inf)
            m_new = jnp.max(s, axis=1, keepdims=True)
            if not first:
                m_old = ml_old[:, head:head + 1]
                l_old = ml_old[:, ATTN_HEADS + head:ATTN_HEADS + head + 1]
                m_new = jnp.maximum(m_old, m_new)
                a = jnp.exp(m_old - m_new)
                alpha.append(a)
            p = jnp.exp(s - m_new)
            l_new = jnp.sum(p, axis=1, keepdims=True)
            if not first:
                l_new = a * l_old + l_new
            vm = jnp.where(in_head_kv, vj, jnp.zeros_like(vj))
            pv.append(_dot(p.astype(BF16), vm))
            linv.append(1.0 / l_new)
            ml_new = jnp.where(lane == head, m_new, ml_new)
            ml_new = jnp.where(lane == ATTN_HEADS + head, l_new, ml_new)
        acc = pv[0] + pv[1]
        if not first:
            acc = jnp.where(lane < HEAD_DIM, alpha[0], alpha[1]) * acc_old[j] + acc
        accs.append(acc)
        linvs.append(jnp.where(lane < HEAD_DIM, linv[0], linv[1]))
    return accs, ml_new, linvs


def _band_mask(has_prev):
    blk = ATTN_BLOCK
    row = lax.broadcasted_iota(jnp.int32, (blk, 2 * blk), 0)
    col = lax.broadcasted_iota(jnp.int32, (blk, 2 * blk), 1)
    band = (col >= row) & (col <= row + blk)
    return band, band & ((col >= blk) | has_prev)


def _col_blocks(ref_or_val):
    return [ref_or_val[:, j * LANES:(j + 1) * LANES] for j in range(ATTN_WIDTH // LANES)]


def _attn_wide_kernel(q_ref, kc_ref, vc_ref, kp_ref, vp_ref, acc_out, ml_out):
    _, valid = _band_mask(pl.program_id(1) > 0)
    for r in range(q_ref.shape[0]):
        kk = jnp.concatenate([kp_ref[r], kc_ref[r]], axis=0)
        vv = jnp.concatenate([vp_ref[r], vc_ref[r]], axis=0)
        accs, ml_new, _ = _attn_item(q_ref[r], kk, vv, valid, None, None)
        for j, a in enumerate(accs):
            acc_out[r, :, j * LANES:(j + 1) * LANES] = a
        ml_out[r] = ml_new


def _attn_mid_kernel(q_ref, kc_ref, vc_ref, kp_ref, vp_ref, accw_ref, mlw_ref, acc_out, ml_out,
                     tmp_acc, tmp_ml):
    dil = q_ref.shape[0]
    ratio = accw_ref.shape[0] // dil
    sub = accw_ref.shape[1]
    n_col = ATTN_WIDTH // LANES
    _, valid = _band_mask(pl.program_id(1) > 0)
    for r in range(dil):
        for qd in range(ratio):
            rows = pl.ds(qd, sub, stride=ratio)
            for j in range(n_col):
                tmp_acc[j, rows, :] = accw_ref[dil * qd + r, :, j * LANES:(j + 1) * LANES]
            tmp_ml[rows, :] = mlw_ref[dil * qd + r]
        kk = jnp.concatenate([kp_ref[r], kc_ref[r]], axis=0)
        vv = jnp.concatenate([vp_ref[r], vc_ref[r]], axis=0)
        accs, ml_new, _ = _attn_item(q_ref[r], kk, vv, valid, tmp_ml[...],
                                     [tmp_acc[j] for j in range(n_col)])
        rows = pl.ds(r, ATTN_BLOCK, stride=dil)
        for j, a in enumerate(accs):
            acc_out[j, rows, :] = a
        ml_out[rows, :] = ml_new


def _attn_last_kernel(q_ref, kc_ref, vc_ref, kp_ref, vp_ref, acc_in, ml_in, a_out):
    gb = q_ref.shape[0]
    band, valid0 = _band_mask(pl.program_id(1) > 0)
    for i in range(gb):
        if i == 0:
            k_prev, v_prev, valid = kp_ref[...], vp_ref[...], valid0
        else:
            k_prev, v_prev, valid = kc_ref[i - 1], vc_ref[i - 1], band
        kk = jnp.concatenate([k_prev, kc_ref[i]], axis=0)
        vv = jnp.concatenate([v_prev, vc_ref[i]], axis=0)
        rows = slice(i * ATTN_BLOCK, (i + 1) * ATTN_BLOCK)
        acc_old = [acc_in[j, rows, :] for j in range(ATTN_WIDTH // LANES)]
        accs, _, linvs = _attn_item(q_ref[i], kk, vv, valid, ml_in[i], acc_old)
        for j, (a, li) in enumerate(zip(accs, linvs)):
            a_out[i, :, j * LANES:(j + 1) * LANES] = (a * li).astype(BF16)


def _dilated_attention(qkv):
    (_, d1), (_, dm), (_, dw) = DILATED_PAIRS
    assert d1 == 1 and dw % dm == 0
    blk, aw = ATTN_BLOCK, ATTN_WIDTH
    (q1, k1, v1), (qm, km, vm), (qw, kw, vw) = qkv
    bsz, seq, _ = q1.shape

    nw = qw.shape[1]
    rpg = min(ATTN_GROUP, dw)
    cur = lambda w: pl.BlockSpec((None, None, rpg, blk, w), lambda b, n, g: (b, n, g, 0, 0))
    prev = pl.BlockSpec((None, None, rpg, blk, aw),
                        lambda b, n, g: (b, jnp.maximum(n - 1, 0), g, 0, 0))
    acc_w, ml_w = pl.pallas_call(
        _attn_wide_kernel,
        grid=(bsz, nw, dw // rpg),
        in_specs=[cur(aw), cur(aw), cur(aw), prev, prev],
        out_specs=[cur(aw), cur(LANES)],
        out_shape=[jax.ShapeDtypeStruct((bsz, nw, dw, blk, aw), F32),
                   jax.ShapeDtypeStruct((bsz, nw, dw, blk, LANES), F32)],
        compiler_params=_params(3),
        name=f"attn_dil{dw}",
    )(qw, kw, vw, kw, vw)

    nm = qm.shape[1]
    ratio = dw // dm
    sub = blk // ratio
    cur = pl.BlockSpec((None, None, dm, blk, aw), lambda b, n: (b, n, 0, 0, 0))
    prev = pl.BlockSpec((None, None, dm, blk, aw),
                        lambda b, n: (b, jnp.maximum(n - 1, 0), 0, 0, 0))
    wide = lambda w: pl.BlockSpec((None, None, dw, sub, w),
                                  lambda b, n: (b, n // ratio, 0, n % ratio, 0))
    n_col = aw // LANES
    acc, ml = pl.pallas_call(
        _attn_mid_kernel,
        grid=(bsz, nm),
        in_specs=[cur, cur, cur, prev, prev, wide(aw), wide(LANES)],
        out_specs=[pl.BlockSpec((None, n_col, dm * blk, LANES), lambda b, n: (b, 0, n, 0)),
                   pl.BlockSpec((None, dm * blk, LANES), lambda b, n: (b, n, 0))],
        out_shape=[jax.ShapeDtypeStruct((bsz, n_col, seq, LANES), F32),
                   jax.ShapeDtypeStruct((bsz, seq, LANES), F32)],
        scratch_shapes=[pltpu.VMEM((n_col, blk, LANES), F32), pltpu.VMEM((blk, LANES), F32)],
        compiler_params=_params(2),
        name=f"attn_dil{dm}",
    )(qm, km, vm, km, vm, acc_w, ml_w)

    nb = seq // blk
    gb = min(ATTN_GROUP, nb)
    view = lambda t: t.reshape(bsz, nb, blk, t.shape[-1])
    cur = lambda w: pl.BlockSpec((None, gb, blk, w), lambda b, n: (b, n, 0, 0))
    prev = pl.BlockSpec((None, None, blk, aw), lambda b, n: (b, jnp.maximum(n * gb - 1, 0), 0, 0))
    a_out = pl.pallas_call(
        _attn_last_kernel,
        grid=(bsz, nb // gb),
        in_specs=[cur(aw), cur(aw), cur(aw), prev, prev,
                  pl.BlockSpec((None, n_col, gb * blk, LANES), lambda b, n: (b, 0, n, 0)),
                  cur(LANES)],
        out_specs=cur(aw),
        out_shape=jax.ShapeDtypeStruct((bsz, nb, blk, aw), BF16),
        compiler_params=_params(2),
        name=f"attn_dil{d1}",
    )(view(q1), view(k1), view(v1), view(k1), view(v1), acc, view(ml))
    return a_out.reshape(bsz, seq, aw)


def _outproj_kernel(a_ref, b_ref, x_ref, wa_ref, wb_ref, gm_ref, g_ref, sh_ref, sc_ref,
                    x1_ref, h_ref):
    mix = _dot(a_ref[...], wa_ref[...]) + _dot(b_ref[...], wb_ref[...])
    x1 = x_ref[...] + gm_ref[...] * mix
    x1_ref[...] = x1
    h_ref[...] = _ada_norm(x1, g_ref[...], sh_ref[...], sc_ref[...]).astype(BF16)


def _even_outproj(a, b, x, w_out, gate_m, g, shift, scale):
    bsz, seq, d = x.shape
    aw, bw = a.shape[2], b.shape[2]
    tm = ROW_TILE
    row = lambda w: pl.BlockSpec((None, tm, w), lambda bi, i: (bi, i, 0))
    per_b = pl.BlockSpec((None, 1, d), lambda bi, i: (bi, 0, 0))
    full = lambda s: pl.BlockSpec(s, lambda bi, i: (0, 0))
    pb = lambda t: t.reshape(bsz, 1, d)
    return pl.pallas_call(
        _outproj_kernel,
        grid=(bsz, seq // tm),
        in_specs=[row(aw), row(bw), row(d), full((aw, d)), full((bw, d)),
                  per_b, full((1, d)), per_b, per_b],
        out_specs=[row(d), row(d)],
        out_shape=[jax.ShapeDtypeStruct((bsz, seq, d), F32),
                   jax.ShapeDtypeStruct((bsz, seq, d), BF16)],
        compiler_params=_params(2),
        name="even_outproj",
    )(a, b, x, w_out[:aw], w_out[aw:], pb(gate_m), g.reshape(1, d), pb(shift), pb(scale))


def _ffn_kernel(h_ref, x_ref, wg_ref, wu_ref, wd_ref, gf_ref, g_ref, sh_ref, sc_ref,
                o_ref, hn_ref):
    h = h_ref[...]
    g = _dot(h, wg_ref[...])
    u = _dot(h, wu_ref[...])
    act = (g * _sigmoid(g) * u).astype(BF16)
    x2 = x_ref[...] + gf_ref[...] * _dot(act, wd_ref[...])
    o_ref[...] = x2
    hn_ref[...] = _ada_norm(x2, g_ref[...], sh_ref[...], sc_ref[...]).astype(BF16)


def _dense_ffn(h, x, w_gate, w_up, w_down, gate_f, next_g, next_shift, next_scale):
    bsz, seq, d = x.shape
    ff = w_gate.shape[1]
    tm = ROW_TILE
    row = pl.BlockSpec((None, tm, d), lambda b, i: (b, i, 0))
    per_b = pl.BlockSpec((None, 1, d), lambda b, i: (b, 0, 0))
    full = lambda s: pl.BlockSpec(s, lambda b, i: (0, 0), pipeline_mode=pl.Buffered(1))
    pb = lambda t: t.reshape(bsz, 1, d)
    return pl.pallas_call(
        _ffn_kernel,
        grid=(bsz, seq // tm),
        in_specs=[row, row, full((d, ff)), full((d, ff)), full((ff, d)), per_b,
                  pl.BlockSpec((1, d), lambda b, i: (0, 0)), per_b, per_b],
        out_specs=[row, row],
        out_shape=[jax.ShapeDtypeStruct((bsz, seq, d), F32),
                   jax.ShapeDtypeStruct((bsz, seq, d), BF16)],
        compiler_params=_params(2),
        name="dense_ffn",
    )(h, x, w_gate, w_up, w_down, pb(gate_f), next_g.reshape(1, d), pb(next_shift),
      pb(next_scale))


def _ssm_operators(a_re, a_im, log_step, b_re, b_im, c_re, c_im):
    L = SSM_CHUNK
    a_re, a_im = a_re.astype(F32), a_im.astype(F32)
    dt = jnp.exp(log_step.astype(F32))[:, None]
    mag = jnp.exp(a_re * dt)
    abar_re = mag * jnp.cos(a_im * dt)
    abar_im = mag * jnp.sin(a_im * dt)
    den = a_re * a_re + a_im * a_im
    nr = abar_re - 1.0
    f_re = (nr * a_re + abar_im * a_im) / den
    f_im = (abar_im * a_re - nr * a_im) / den
    b_re, b_im = b_re.astype(F32), b_im.astype(F32)
    bb_re = f_re[..., None] * b_re - f_im[..., None] * b_im
    bb_im = f_re[..., None] * b_im + f_im[..., None] * b_re
    c_re, c_im = c_re.astype(F32), c_im.astype(F32)

    def step(carry, _):
        pr, pi = carry
        nxt = (pr * abar_re - pi * abar_im, pr * abar_im + pi * abar_re)
        return nxt, carry
    (pl_re, pl_im), (pw_re, pw_im) = lax.scan(
        step, (jnp.ones_like(abar_re), jnp.zeros_like(abar_re)), None, length=L)
    pw_re = jnp.concatenate([pw_re, pl_re[None]], axis=0)
    pw_im = jnp.concatenate([pw_im, pl_im[None]], axis=0)

    hi = lax.Precision.HIGHEST
    cp_re = c_re[None] * pw_re[:L, :, None, :] - c_im[None] * pw_im[:L, :, None, :]
    cp_im = c_re[None] * pw_im[:L, :, None, :] + c_im[None] * pw_re[:L, :, None, :]
    w = (jnp.einsum('tgcp,gpd->tgdc', cp_re, bb_re, precision=hi)
         - jnp.einsum('tgcp,gpd->tgdc', cp_im, bb_im, precision=hi))
    s_idx = jnp.arange(L)[:, None]
    t_idx = jnp.arange(L)[None, :]
    tau = t_idx - s_idx
    toep = jnp.where((tau >= 0)[:, :, None, None, None], w[jnp.clip(tau, 0, L - 1)], 0.0)
    n_g, n_c = a_re.shape[0], b_re.shape[2]
    toep = toep.transpose(2, 0, 3, 1, 4).reshape(n_g, L * n_c, L * n_c)

    rp_re, rp_im = pw_re[:L][::-1], pw_im[:L][::-1]
    so_re = rp_re[..., None] * bb_re[None] - rp_im[..., None] * bb_im[None]
    so_im = rp_re[..., None] * bb_im[None] + rp_im[..., None] * bb_re[None]
    to_rows = lambda t: t.transpose(1, 0, 3, 2).reshape(n_g, L * n_c, -1)
    s_out = jnp.concatenate([to_rows(so_re), to_rows(so_im)], axis=-1)
    s_out_sw = jnp.concatenate([to_rows(so_im), to_rows(so_re)], axis=-1)

    qp_re, qp_im = pw_re[1:], pw_im[1:]
    ci_re = c_re[None] * qp_re[:, :, None, :] - c_im[None] * qp_im[:, :, None, :]
    ci_im = c_re[None] * qp_im[:, :, None, :] + c_im[None] * qp_re[:, :, None, :]
    to_cols = lambda t: t.transpose(1, 3, 0, 2).reshape(n_g, -1, L * n_c)
    c_in = jnp.concatenate([to_cols(ci_re), -to_cols(ci_im)], axis=1)

    al_re, al_im = pw_re[L], pw_im[L]
    a1 = jnp.concatenate([al_re, al_re], axis=-1)[:, None, :]
    a2 = jnp.concatenate([-al_im, al_im], axis=-1)[:, None, :]
    return (toep.astype(BF16), s_out.astype(BF16), s_out_sw.astype(BF16),
            c_in.astype(BF16), a1, a2)


def _piece_transpose(arrs):
    n = len(arrs)
    piece = lax.broadcasted_iota(jnp.int32, (1, LANES), 1) // SSM_GROUP
    arrs = list(arrs)
    dist = n // 2
    while dist >= 1:
        keep = (piece & dist) == 0
        for i in range(n):
            if i & dist == 0:
                a, b = arrs[i], arrs[i + dist]
                arrs[i] = jnp.where(keep, a, pltpu.roll(b, dist * SSM_GROUP, 1))
                arrs[i + dist] = jnp.where(keep, pltpu.roll(a, LANES - dist * SSM_GROUP, 1), b)
        dist //= 2
    return arrs


def _ssm_kernel(u_ref, t_ref, s_ref, ssw_ref, cin_ref, a1_ref, a2_ref, y_ref,
                st_ref, stsw_ref, xs_ref, xc_ref, zc_ref):
    kt, L, bsz, _ = u_ref.shape
    rows = kt * bsz
    n_lg = SSM_LANE_GROUPS

    @pl.when(pl.program_id(1) == 0)
    def _():
        xc_ref[...] = jnp.zeros(xc_ref.shape, F32)
        zc_ref[...] = jnp.zeros(zc_ref.shape, F32)

    zs = [u_ref[:, s].reshape(rows, LANES).astype(F32) for s in range(L)]
    lo = _piece_transpose(zs[:n_lg])
    hi = _piece_transpose(zs[n_lg:])
    ys = []
    for gl in range(n_lg):
        v = jnp.concatenate([lo[gl], hi[gl]], axis=1).astype(BF16)
        st_ref[...] = _dot(v, s_ref[gl])
        stsw_ref[...] = _dot(v, ssw_ref[gl])
        a1 = jnp.broadcast_to(a1_ref[gl], (bsz, 2 * SSM_STATE))
        a2 = jnp.broadcast_to(a2_ref[gl], (bsz, 2 * SSM_STATE))

        def body(k, carry):
            x, z = carry
            r = pl.ds(pl.multiple_of(k * bsz, bsz), bsz)
            xs_ref[r, :] = x
            x_new = x * a1 + z * a2 + st_ref[r, :]
            z_new = z * a1 - x * a2 + stsw_ref[r, :]
            return x_new, z_new

        x, z = lax.fori_loop(0, kt, body, (xc_ref[gl], zc_ref[gl]), unroll=8)
        xc_ref[gl] = x
        zc_ref[gl] = z
        ys.append(_dot(v, t_ref[gl]) + _dot(xs_ref[...].astype(BF16), cin_ref[gl]))
    out_lo = _piece_transpose([y[:, :LANES] for y in ys])
    out_hi = _piece_transpose([y[:, LANES:] for y in ys])
    for t in range(n_lg):
        y_ref[:, t] = out_lo[t].astype(BF16).reshape(kt, bsz, LANES)
        y_ref[:, n_lg + t] = out_hi[t].astype(BF16).reshape(kt, bsz, LANES)


def _ssm_scan(h, ops):
    bsz, seq, d = h.shape
    toep, s_out, s_out_sw, c_in, a1, a2 = ops
    L, C, n_lg = SSM_CHUNK, SSM_GROUP, SSM_LANE_GROUPS
    assert L == 2 * n_lg
    n_k = seq // L
    kt = min(SSM_CHUNKS_PER_STEP, n_k)
    hv = h.transpose(1, 0, 2).reshape(n_k, L, bsz, d)
    act = pl.BlockSpec((kt, L, bsz, LANES), lambda o, k: (k, 0, 0, o))
    wblk = lambda s: pl.BlockSpec((n_lg,) + s, lambda o, k: (o, 0, 0))
    y = pl.pallas_call(
        _ssm_kernel,
        grid=(d // LANES, n_k // kt),
        in_specs=[act, wblk((L * C, L * C)), wblk((L * C, 2 * SSM_STATE)),
                  wblk((L * C, 2 * SSM_STATE)), wblk((2 * SSM_STATE, L * C)),
                  wblk((1, 2 * SSM_STATE)), wblk((1, 2 * SSM_STATE))],
        out_specs=act,
        out_shape=jax.ShapeDtypeStruct((n_k, L, bsz, d), BF16),
        scratch_shapes=[pltpu.VMEM((kt * bsz, 2 * SSM_STATE), F32)] * 3
        + [pltpu.VMEM((n_lg, bsz, 2 * SSM_STATE), F32)] * 2,
        compiler_params=_params(2),
        name="ssm_scan",
    )(hv, toep, s_out, s_out_sw, c_in, a1, a2)
    return y.reshape(seq, bsz, d).transpose(1, 0, 2)


def _gelu_tanh(x):
    return 0.5 * x * (1.0 + jnp.tanh(math.sqrt(2.0 / math.pi) * (x + 0.044715 * (x * x * x))))


def _ssm_out_kernel(y_ref, h_ref, x_ref, d_ref, w_ref, b_ref, gm_ref, g_ref, sh_ref, sc_ref,
                    rwh_ref, rwl_ref, rb_ref, x1_ref, h2_ref, route_ref):
    d = x_ref.shape[1]
    y = y_ref[...].astype(F32) + d_ref[...] * h_ref[...].astype(F32)
    z = _dot(_gelu_tanh(y).astype(BF16), w_ref[...]) + b_ref[...]
    mix = z[:, :d] * _sigmoid(z[:, d:])
    x1 = x_ref[...] + gm_ref[...] * mix
    x1_ref[...] = x1
    h2 = _ada_norm(x1, g_ref[...], sh_ref[...], sc_ref[...])
    h2_ref[...] = h2

    hh, hl = _split_bf16(h2)
    logits = _dot(hh, rwh_ref[...]) + _dot(hl, rwh_ref[...]) + _dot(hh, rwl_ref[...]) + rb_ref[...]
    lane = lax.broadcasted_iota(jnp.int32, logits.shape, 1)
    lane_f = lane.astype(F32)
    neg_inf = jnp.float32(-jnp.inf)
    logits = jnp.where(lane < N_EXPERTS, logits, neg_inf)
    m1 = jnp.max(logits, axis=1, keepdims=True)
    i1 = jnp.min(jnp.where(logits == m1, lane_f, float(LANES)), axis=1, keepdims=True)
    rest = jnp.where(lane_f == i1, neg_inf, logits)
    m2 = jnp.max(rest, axis=1, keepdims=True)
    i2 = jnp.min(jnp.where(rest == m2, lane_f, float(LANES)), axis=1, keepdims=True)
    e2 = jnp.exp(m2 - m1)
    g1 = 1.0 / (1.0 + e2)
    g2 = e2 / (1.0 + e2)
    route = jnp.where(lane == 0, i1, jnp.where(lane == 1, i2,
                      jnp.where(lane == 2, g1, jnp.where(lane == 3, g2, 0.0))))
    route_ref[...] = route


def _ssm_out(y, h, x, d_skip, glu_w, glu_b, gate_m, g, shift, scale, router_w, router_b):
    bsz, seq, d = x.shape
    tm = ROW_TILE
    row = lambda w: pl.BlockSpec((None, tm, w), lambda b, i: (b, i, 0))
    per_b = pl.BlockSpec((None, 1, d), lambda b, i: (b, 0, 0))
    full = lambda s: pl.BlockSpec(s, lambda b, i: (0, 0))
    pb = lambda t: t.reshape(bsz, 1, d)
    rw = jnp.zeros((d, LANES), F32).at[:, :N_EXPERTS].set(router_w)
    rwh = rw.astype(BF16)
    rwl = (rw - rwh.astype(F32)).astype(BF16)
    rb = jnp.zeros((1, LANES), F32).at[0, :N_EXPERTS].set(router_b)
    return pl.pallas_call(
        _ssm_out_kernel,
        grid=(bsz, seq // tm),
        in_specs=[row(d), row(d), row(d), full((1, d)), full((d, 2 * d)), full((1, 2 * d)),
                  per_b, full((1, d)), per_b, per_b,
                  full((d, LANES)), full((d, LANES)), full((1, LANES))],
        out_specs=[row(d), row(d), row(LANES)],
        out_shape=[jax.ShapeDtypeStruct((bsz, seq, d), F32),
                   jax.ShapeDtypeStruct((bsz, seq, d), F32),
                   jax.ShapeDtypeStruct((bsz, seq, LANES), F32)],
        compiler_params=_params(2),
        name="ssm_out",
    )(y, h, x, d_skip.reshape(1, d), glu_w, glu_b.reshape(1, 2 * d), pb(gate_m),
      g.reshape(1, d), pb(shift), pb(scale), rwh, rwl, rb)


def _routing_tables(route, tm):
    n = route.shape[0]
    i32 = jnp.int32
    e_flat = route[:, :2].astype(i32).T.reshape(-1)
    onehot = (e_flat[:, None] == jnp.arange(N_EXPERTS)[None, :]).astype(i32)
    incl = jnp.cumsum(onehot, axis=0)
    counts = incl[-1]
    rank = jnp.sum((incl - onehot) * onehot, axis=1)
    padded = ((counts + tm - 1) // tm) * tm
    ends = jnp.cumsum(padded)
    starts = ends - padded
    dest = starts[e_flat] + rank
    n_tiles = (2 * n) // tm + N_EXPERTS
    n_rows = n_tiles * tm
    s = jnp.arange(n_rows, dtype=i32)
    e_of_s = jnp.sum((s[:, None] >= ends[None, :]).astype(i32), axis=1)
    starts9 = jnp.concatenate([starts, ends[-1:]])
    counts9 = jnp.concatenate([counts, jnp.zeros((1,), i32)])
    ustart9 = jnp.concatenate([jnp.cumsum(counts) - counts, jnp.full((1,), 2 * n, i32)])
    real_before = ustart9[e_of_s] + jnp.minimum(s - starts9[e_of_s], counts9[e_of_s])
    pair = (2 * n + s - real_before).at[dest].set(
        jnp.arange(2 * n, dtype=i32), unique_indices=True, mode="promise_in_bounds")
    src = jnp.concatenate([jnp.where(pair < 2 * n, pair % n, 0), jnp.zeros((tm,), i32)])
    dst = jnp.concatenate([n_rows + jnp.arange(tm, dtype=i32), pair])
    tile_expert = jnp.minimum(e_of_s[::tm], N_EXPERTS - 1)
    return src, dst, tile_expert


def _moe_kernel(te_ref, src_ref, src_next_ref, dst_prev_ref, dst_ref, h_hbm, wg_ref, wu_ref, wd_ref,
                y_hbm, hbuf, ybuf, gsem, ssem):
    t = pl.program_id(0)
    n_t = pl.num_programs(0)
    tm = hbuf.shape[1]
    ff = wg_ref.shape[1]
    fc = ff // MOE_FF_CHUNKS
    slot = t % 2
    other = 1 - slot

    def gather(slot_, r, row):
        return pltpu.make_async_copy(h_hbm.at[pl.ds(row, 1)], hbuf.at[slot_, pl.ds(r, 1)],
                                     gsem.at[slot_])

    def scatter(slot_, r, row):
        return pltpu.make_async_copy(ybuf.at[slot_, pl.ds(r, 1)], y_hbm.at[pl.ds(row, 1)],
                                     ssem.at[slot_])

    def for_rows(fn):
        def body(r, c):
            fn(r)
            return c
        lax.fori_loop(0, tm, body, 0, unroll=8)

    @pl.when(t == 0)
    def _():
        ybuf[...] = jnp.zeros(ybuf.shape, F32)
        for_rows(lambda r: gather(0, r, src_ref[r]).start())

    for_rows(lambda r: gather(slot, r, 0).wait())

    @pl.when(t > 0)
    def _():
        for_rows(lambda r: scatter(slot, r, 0).wait())

    for r in range(tm):
        gather(other, r, src_next_ref[r]).start()
    for r in range(tm):
        scatter(other, r, dst_prev_ref[r]).start()

    h = hbuf[slot].astype(BF16)
    acc = jnp.zeros((tm, wd_ref.shape[1]), F32)
    for c in range(MOE_FF_CHUNKS):
        g = _dot(h, wg_ref[:, c * fc:(c + 1) * fc])
        u = _dot(h, wu_ref[:, c * fc:(c + 1) * fc])
        act = (g * _sigmoid(g) * u).astype(BF16)
        acc = acc + _dot(act, wd_ref[c * fc:(c + 1) * fc, :])
    ybuf[slot] = acc

    @pl.when(t == n_t - 1)
    def _():
        for_rows(lambda r: gather(other, r, 0).wait())
        for_rows(lambda r: scatter(other, r, 0).wait())
        for_rows(lambda r: scatter(slot, r, dst_ref[r]).start())
        for_rows(lambda r: scatter(slot, r, 0).wait())


def _moe_experts(h2, src, dst, tile_expert, w_gate, w_up, w_down):
    n, d = h2.shape
    ff = w_gate.shape[2]
    tm = MOE_TILE
    n_tiles = tile_expert.shape[0]
    wspec = lambda s: pl.BlockSpec((None,) + s, lambda t, te: (te[t], 0, 0),
                                   pipeline_mode=pl.Buffered(1))
    idx = lambda off: pl.BlockSpec((tm,), lambda t, te: (t + off,), memory_space=pltpu.SMEM)
    grid_spec = pltpu.PrefetchScalarGridSpec(
        num_scalar_prefetch=1,
        grid=(n_tiles,),
        in_specs=[idx(0), idx(1), idx(0), idx(1),
                  pl.BlockSpec(memory_space=pl.ANY),
                  wspec((d, ff)), wspec((d, ff)), wspec((ff, d))],
        out_specs=pl.BlockSpec(memory_space=pl.ANY),
        scratch_shapes=[pltpu.VMEM((2, tm, d), F32), pltpu.VMEM((2, tm, d), F32),
                        pltpu.SemaphoreType.DMA((2,)), pltpu.SemaphoreType.DMA((2,))],
    )
    return pl.pallas_call(
        _moe_kernel,
        grid_spec=grid_spec,
        out_shape=jax.ShapeDtypeStruct((n_tiles * tm + tm, d), F32),
        compiler_params=_params(1),
        name="moe_experts",
    )(tile_expert, src, src, dst, dst, h2, w_gate, w_up, w_down)


def _combine_kernel(y1_ref, y2_ref, route_ref, x_ref, gf_ref, fg_ref, o_ref, *, final):
    route = route_ref[...]
    ff = route[:, 2:3] * y1_ref[...] + route[:, 3:4] * y2_ref[...]
    x2 = x_ref[...] + gf_ref[...] * ff
    if final:
        ms = jnp.mean(x2 * x2, axis=-1, keepdims=True)
        x2 = x2 * lax.rsqrt(ms + RMS_EPS) * fg_ref[...]
    o_ref[...] = x2


def _moe_combine(y, route, x1, gate_f, final_g, final):
    bsz, seq, d = x1.shape
    n = bsz * seq
    tm = ROW_TILE
    per_batch = seq // tm
    out = pl.pallas_call(
        functools.partial(_combine_kernel, final=final),
        grid=(n // tm,),
        in_specs=[pl.BlockSpec((tm, d), lambda i: (i, 0)),
                  pl.BlockSpec((tm, d), lambda i: (n // tm + i, 0)),
                  pl.BlockSpec((tm, LANES), lambda i: (i, 0)),
                  pl.BlockSpec((tm, d), lambda i: (i, 0)),
                  pl.BlockSpec((None, 1, d), lambda i: (i // per_batch, 0, 0)),
                  pl.BlockSpec((1, d), lambda i: (0, 0))],
        out_specs=pl.BlockSpec((tm, d), lambda i: (i, 0)),
        out_shape=jax.ShapeDtypeStruct((n, d), F32),
        compiler_params=_params(1),
        name="moe_combine",
    )(y, y, route.reshape(n, LANES), x1.reshape(n, d),
      gate_f.reshape(bsz, 1, d), final_g.reshape(1, d))
    return out.reshape(bsz, seq, d)


def _final_norm_kernel(x_ref, g_ref, o_ref):
    x = x_ref[...]
    ms = jnp.mean(x * x, axis=-1, keepdims=True)
    o_ref[...] = x * lax.rsqrt(ms + RMS_EPS) * g_ref[...]


def _final_norm(x, g):
    bsz, seq, d = x.shape
    tm = ROW_TILE
    row = pl.BlockSpec((None, tm, d), lambda b, i: (b, i, 0))
    return pl.pallas_call(
        _final_norm_kernel,
        grid=(bsz, seq // tm),
        in_specs=[row, pl.BlockSpec((1, d), lambda b, i: (0, 0))],
        out_specs=row,
        out_shape=jax.ShapeDtypeStruct((bsz, seq, d), F32),
        compiler_params=_params(2),
        name="final_norm",
    )(x, g.reshape(1, d))


def _even_layer(x, mods, norm_mix_g, norm_ffn_g, cos, sin, w_in, conv_w, w_out,
                w_gate, w_up, w_down, next_norm):
    sh_m, sc_m, g_m, sh_f, sc_f, g_f = mods
    qkv, b_out = _even_inproj(x, norm_mix_g, sh_m, sc_m, w_in.astype(BF16), cos, sin, conv_w)
    a_out = _dilated_attention(qkv)
    x1, h2 = _even_outproj(a_out, b_out, x, w_out.astype(BF16), g_m, norm_ffn_g, sh_f, sc_f)
    return _dense_ffn(h2, x1, w_gate.astype(BF16), w_up.astype(BF16), w_down.astype(BF16), g_f,
                      *next_norm)


def _odd_layer(x, h, mods, norm_ffn_g, ssm, d_skip, glu_w, glu_b,
               router_w, router_b, w_gate, w_up, w_down, final_g, final):
    _, _, g_m, sh_f, sc_f, g_f = mods
    bsz, seq, d = x.shape
    y = _ssm_scan(h, _ssm_operators(*ssm))
    x1, h2, route = _ssm_out(y, h, x, d_skip, glu_w.astype(BF16), glu_b, g_m,
                             norm_ffn_g, sh_f, sc_f, router_w, router_b)
    route = route.reshape(bsz * seq, LANES)
    src, dst, tile_expert = _routing_tables(route, MOE_TILE)
    y_pairs = _moe_experts(h2.reshape(bsz * seq, d), src, dst, tile_expert,
                           w_gate.astype(BF16), w_up.astype(BF16), w_down.astype(BF16))
    return _moe_combine(y_pairs, route, x1, g_f, final_g, final)


def kernel(x, c, positions, mod_w, mod_b, norm_mix_g, norm_ffn_g, ev_w_in, ev_conv_w, ev_w_out, ffn_w_gate, ffn_w_up, ffn_w_down, ssm_a_re, ssm_a_im, ssm_log_step, ssm_b_re, ssm_b_im, ssm_c_re, ssm_c_im, ssm_d, glu_w, glu_b, moe_router_w, moe_router_b, moe_w_gate, moe_w_up, moe_w_down, final_norm_g):
    depth = mod_w.shape[0]
    d = x.shape[2]
    assert depth % 2 == 0
    mod = _modulation(c, mod_w, mod_b)
    cos, sin = _rope_tables(positions)
    layer_mods = [[mod[layer, :, j * d:(j + 1) * d] for j in range(6)] for layer in range(depth)]
    h = None
    for layer in range(depth):
        mods = layer_mods[layer]
        i = layer // 2
        if layer % 2 == 0:
            nxt = layer_mods[layer + 1]
            x, h = _even_layer(x, mods, norm_mix_g[layer], norm_ffn_g[layer], cos, sin,
                               ev_w_in[i], ev_conv_w[i], ev_w_out[i],
                               ffn_w_gate[i], ffn_w_up[i], ffn_w_down[i],
                               (norm_mix_g[layer + 1], nxt[0], nxt[1]))
        else:
            ssm = (ssm_a_re[i], ssm_a_im[i], ssm_log_step[i], ssm_b_re[i], ssm_b_im[i],
                   ssm_c_re[i], ssm_c_im[i])
            x = _odd_layer(x, h, mods, norm_ffn_g[layer], ssm, ssm_d[i],
                           glu_w[i], glu_b[i], moe_router_w[i], moe_router_b[i],
                           moe_w_gate[i], moe_w_up[i], moe_w_down[i],
                           final_norm_g, layer == depth - 1)
    return x
```

```python
import functools
import math

import jax
import jax.numpy as jnp
from jax import lax
from jax.experimental import pallas as pl
from jax.experimental.pallas import tpu as pltpu

F32 = jnp.float32
BF16 = jnp.bfloat16

ATTN_HEADS = 8
HEAD_DIM = 64
ATTN_WIDTH = ATTN_HEADS * HEAD_DIM
ROPE_DIM = HEAD_DIM // 4
ROPE_THETA = 500000.0
DILATED_PAIRS = ((128, 1), (512, 4), (2048, 16))
ATTN_BLOCK = 128
SSM_GROUP = 16
SSM_STATE = 64
SSM_CHUNK = 16
N_EXPERTS = 8
RMS_EPS = 1e-6

LANES = 128
VMEM_LIMIT_BYTES = 56 * 1024 * 1024

ROW_TILE = 512
MOE_TILE = 512
MOE_FF_CHUNKS = 2
ATTN_GROUP = 4
SSM_CHUNKS_PER_STEP = 64
SSM_LANE_GROUPS = LANES // SSM_GROUP


def _params(n_axes, vmem=VMEM_LIMIT_BYTES):
    return pltpu.CompilerParams(
        dimension_semantics=("arbitrary",) * n_axes, vmem_limit_bytes=vmem)


def _dot(a, b):
    return jnp.dot(a, b, preferred_element_type=F32)


def _sigmoid(x):
    return 1.0 / (1.0 + jnp.exp(-x))


def _split_bf16(x):
    hi = x.astype(BF16)
    lo = (x - hi.astype(F32)).astype(BF16)
    return hi, lo


def _ada_norm(x, g, shift, scale):
    ms = jnp.mean(x * x, axis=-1, keepdims=True)
    return x * lax.rsqrt(ms + RMS_EPS) * g * (1.0 + scale) + shift


def _mod_kernel(c_ref, w_ref, b_ref, o_ref):
    c = c_ref[...]
    cond = c * _sigmoid(c)
    ch, cl = _split_bf16(cond)
    wh, wl = _split_bf16(w_ref[...])
    o_ref[...] = _dot(ch, wh) + _dot(cl, wh) + _dot(ch, wl) + b_ref[...]


def _modulation(c, mod_w, mod_b):
    depth, d, n = mod_w.shape
    bsz = c.shape[0]
    tn = 1024
    return pl.pallas_call(
        _mod_kernel,
        grid=(depth, n // tn),
        in_specs=[pl.BlockSpec((bsz, d), lambda l, j: (0, 0)),
                  pl.BlockSpec((None, d, tn), lambda l, j: (l, 0, j)),
                  pl.BlockSpec((None, 1, tn), lambda l, j: (l, 0, j))],
        out_specs=pl.BlockSpec((None, bsz, tn), lambda l, j: (l, 0, j)),
        out_shape=jax.ShapeDtypeStruct((depth, bsz, n), F32),
        compiler_params=_params(2),
        name="modulation",
    )(c, mod_w, mod_b.reshape(depth, 1, n))


def _rope_kernel(pos_ref, inv_ref, cos_ref, sin_ref):
    ang = pos_ref[...].astype(F32) * inv_ref[...]
    cos_ref[...] = jnp.cos(ang)
    sin_ref[...] = jnp.sin(ang)


def _rope_tables(positions):
    bsz, seq = positions.shape
    inv = ROPE_THETA ** (-jnp.arange(0, ROPE_DIM, 2, dtype=F32) / ROPE_DIM)
    lane = jnp.arange(LANES) % HEAD_DIM
    inv_lane = jnp.where(lane < ROPE_DIM, inv[lane % (ROPE_DIM // 2)], 0.0).reshape(1, LANES)
    tm = ROW_TILE
    spec = pl.BlockSpec((None, tm, LANES), lambda b, i: (b, i, 0))
    return pl.pallas_call(
        _rope_kernel,
        grid=(bsz, seq // tm),
        in_specs=[pl.BlockSpec((None, tm, 1), lambda b, i: (b, i, 0)),
                  pl.BlockSpec((1, LANES), lambda b, i: (0, 0))],
        out_specs=[spec, spec],
        out_shape=[jax.ShapeDtypeStruct((bsz, seq, LANES), F32)] * 2,
        compiler_params=_params(2),
        name="rope_tables",
    )(positions.reshape(bsz, seq, 1), inv_lane)


def _inproj_kernel(x_ref, g_ref, sh_ref, sc_ref, w_ref, cos_ref, sin_ref, cw_ref, *rest):
    n_pat = len(DILATED_PAIRS)
    qkv_refs = [rest[3 * p:3 * p + 3] for p in range(n_pat)]
    bo_ref, ubuf, stage = rest[3 * n_pat:]
    tm = x_ref.shape[0]
    aw = ATTN_WIDTH
    cwid = cw_ref.shape[1]
    h = _ada_norm(x_ref[...], g_ref[...], sh_ref[...], sc_ref[...]).astype(BF16)
    proj = _dot(h, w_ref[...])

    cos = cos_ref[...]
    sin = sin_ref[...]
    lane = lax.broadcasted_iota(jnp.int32, (tm, LANES), 1) % HEAD_DIM
    first_half = lane < ROPE_DIM // 2

    def rope(t):
        rot = jnp.where(first_half,
                        -pltpu.roll(t, LANES - ROPE_DIM // 2, 1),
                        pltpu.roll(t, ROPE_DIM // 2, 1))
        return t * cos + rot * sin

    n_col = aw // LANES
    for j in range(n_col):
        sl = slice(j * LANES, (j + 1) * LANES)
        stage[0, j] = rope(proj[:, sl]) * (HEAD_DIM ** -0.5)
        stage[1, j] = rope(proj[:, aw + j * LANES: aw + (j + 1) * LANES])
        stage[2, j] = proj[:, 2 * aw + j * LANES: 2 * aw + (j + 1) * LANES]
    for (_, dil), refs in zip(DILATED_PAIRS, qkv_refs):
        for which, ref in enumerate(refs):
            for j in range(n_col):
                sl = slice(j * LANES, (j + 1) * LANES)
                if dil == 1:
                    ref[:, sl] = stage[which, j].astype(BF16)
                else:
                    for r in range(dil):
                        ref[r, :, sl] = stage[which, j, pl.ds(r, tm // dil, stride=dil), :].astype(BF16)

    b_gate = proj[:, 3 * aw:3 * aw + cwid]
    c_gate = proj[:, 3 * aw + cwid:3 * aw + 2 * cwid]
    xin = proj[:, 3 * aw + 2 * cwid:]
    u = c_gate * xin

    @pl.when(pl.program_id(1) == 0)
    def _():
        ubuf[0:8, :] = jnp.zeros((8, cwid), F32)

    ubuf[8:, :] = u
    conv = (cw_ref[0:1, :] * ubuf[6:6 + tm, :] + cw_ref[1:2, :] * ubuf[7:7 + tm, :]
            + cw_ref[2:3, :] * u)
    bo_ref[...] = (b_gate * conv).astype(BF16)
    ubuf[0:8, :] = ubuf[tm:tm + 8, :]


def _even_inproj(x, g, shift, scale, w_in, cos, sin, conv_w):
    bsz, seq, d = x.shape
    n = w_in.shape[1]
    cwid = conv_w.shape[1]
    tm = ROW_TILE
    aw = ATTN_WIDTH
    row = lambda w: pl.BlockSpec((None, tm, w), lambda b, i: (b, i, 0))
    per_b = pl.BlockSpec((None, 1, d), lambda b, i: (b, 0, 0))
    qkv_specs, qkv_shapes = [], []
    for window, dil in DILATED_PAIRS:
        assert window // dil == ATTN_BLOCK and seq % window == 0 and (dil == 1 or window % tm == 0)
        if dil == 1:
            spec, shape = row(aw), (bsz, seq, aw)
        else:
            per_span = window // tm
            spec = pl.BlockSpec((None, None, dil, tm // dil, aw),
                                lambda b, i, per_span=per_span: (b, i // per_span, 0, i % per_span, 0))
            shape = (bsz, seq // window, dil, ATTN_BLOCK, aw)
        qkv_specs += [spec] * 3
        qkv_shapes += [jax.ShapeDtypeStruct(shape, BF16)] * 3
    outs = pl.pallas_call(
        _inproj_kernel,
        grid=(bsz, seq // tm),
        in_specs=[row(d), pl.BlockSpec((1, d), lambda b, i: (0, 0)), per_b, per_b,
                  pl.BlockSpec((d, n), lambda b, i: (0, 0)),
                  row(LANES), row(LANES),
                  pl.BlockSpec(conv_w.shape, lambda b, i: (0, 0))],
        out_specs=qkv_specs + [row(cwid)],
        out_shape=qkv_shapes + [jax.ShapeDtypeStruct((bsz, seq, cwid), BF16)],
        scratch_shapes=[pltpu.VMEM((tm + 8, cwid), F32),
                        pltpu.VMEM((3, aw // LANES, tm, LANES), F32)],
        compiler_params=_params(2),
        name="even_inproj",
    )(x, g.reshape(1, d), shift.reshape(bsz, 1, d), scale.reshape(bsz, 1, d),
      w_in, cos, sin, conv_w)
    qkv = [outs[3 * p:3 * p + 3] for p in range(len(DILATED_PAIRS))]
    return qkv, outs[-1]


def _attn_item(q, kk, vv, valid):
    blk = ATTN_BLOCK
    lane = lax.broadcasted_iota(jnp.int32, (blk, LANES), 1)
    lane_kv = lax.broadcasted_iota(jnp.int32, (2 * blk, LANES), 1)
    neg_inf = jnp.float32(-jnp.inf)
    ml_new = jnp.zeros((blk, LANES), F32)
    n_col, per_col = ATTN_WIDTH // LANES, LANES // HEAD_DIM
    scores = []
    for j in range(n_col):
        sl = slice(j * LANES, (j + 1) * LANES)
        for hh in range(per_col):
            in_head = (lane // HEAD_DIM) == hh
            qm = jnp.where(in_head, q[:, sl], jnp.zeros_like(q[:, sl]))
            s = lax.dot_general(qm, kk[:, sl], (((1,), (1,)), ((), ())),
                                preferred_element_type=F32)
            scores.append(jnp.where(valid, s, neg_inf))
    accs = []
    for j in range(n_col):
        sl = slice(j * LANES, (j + 1) * LANES)
        vj = vv[:, sl]
        pv = []
        for hh in range(per_col):
            head = j * per_col + hh
            in_head_kv = (lane_kv // HEAD_DIM) == hh
            s = scores[head]
            m_new = jnp.max(s, axis=1, keepdims=True)
            p = jnp.exp(s - m_new)
            l_new = jnp.sum(p, axis=1, keepdims=True)
            vm = jnp.where(in_head_kv, vj, jnp.zeros_like(vj))
            pv.append(_dot(p.astype(BF16), vm))
            ml_new = jnp.where(lane == head, m_new, ml_new)
            ml_new = jnp.where(lane == ATTN_HEADS + head, l_new, ml_new)
        accs.append(pv[0] + pv[1])
    return accs, ml_new


def _head_columns(tile, first_lane, j):
    lane = lax.broadcasted_iota(jnp.int32, tile.shape, 1)
    h0 = first_lane + j * (LANES // HEAD_DIM)
    return jnp.where(lane < HEAD_DIM, tile[:, h0:h0 + 1], tile[:, h0 + 1:h0 + 2])


def _attn_merge(acc_old, ml_old, acc_loc, ml_loc):
    lane = lax.broadcasted_iota(jnp.int32, ml_old.shape, 1)
    m_new = jnp.maximum(ml_old, ml_loc)
    a_old = jnp.exp(ml_old - m_new)
    a_loc = jnp.exp(ml_loc - m_new)
    l_new = (pltpu.roll(a_old, ATTN_HEADS, 1) * ml_old
             + pltpu.roll(a_loc, ATTN_HEADS, 1) * ml_loc)
    ml_new = jnp.where(lane < ATTN_HEADS, m_new, jnp.where(lane < 2 * ATTN_HEADS, l_new, 0.0))
    accs = [_head_columns(a_old, 0, j) * acc_old[j] + _head_columns(a_loc, 0, j) * acc_loc[j]
            for j in range(len(acc_loc))]
    return accs, ml_new


def _band_mask(has_prev):
    blk = ATTN_BLOCK
    row = lax.broadcasted_iota(jnp.int32, (blk, 2 * blk), 0)
    col = lax.broadcasted_iota(jnp.int32, (blk, 2 * blk), 1)
    band = (col >= row) & (col <= row + blk)
    return band, band & ((col >= blk) | has_prev)


def _attn_wide_kernel(q_ref, kc_ref, vc_ref, kp_ref, vp_ref, acc_out, ml_out):
    _, valid = _band_mask(pl.program_id(1) > 0)
    for r in range(q_ref.shape[0]):
        kk = jnp.concatenate([kp_ref[r], kc_ref[r]], axis=0)
        vv = jnp.concatenate([vp_ref[r], vc_ref[r]], axis=0)
        accs, ml_new = _attn_item(q_ref[r], kk, vv, valid)
        for j, a in enumerate(accs):
            acc_out[r, :, j * LANES:(j + 1) * LANES] = a
        ml_out[r] = ml_new


def _attn_mid_kernel(q_ref, kc_ref, vc_ref, kp_ref, vp_ref, accw_ref, mlw_ref, acc_out, ml_out,
                     tmp_acc, tmp_ml):
    dil = q_ref.shape[0]
    ratio = accw_ref.shape[0] // dil
    sub = accw_ref.shape[1]
    n_col = ATTN_WIDTH // LANES
    _, valid = _band_mask(pl.program_id(1) > 0)
    for r in range(dil):
        for qd in range(ratio):
            rows = pl.ds(qd, sub, stride=ratio)
            for j in range(n_col):
                tmp_acc[j, rows, :] = accw_ref[dil * qd + r, :, j * LANES:(j + 1) * LANES]
            tmp_ml[rows, :] = mlw_ref[dil * qd + r]
        kk = jnp.concatenate([kp_ref[r], kc_ref[r]], axis=0)
        vv = jnp.concatenate([vp_ref[r], vc_ref[r]], axis=0)
        acc_loc, ml_loc = _attn_item(q_ref[r], kk, vv, valid)
        accs, ml_new = _attn_merge([tmp_acc[j] for j in range(n_col)], tmp_ml[...],
                                   acc_loc, ml_loc)
        rows = pl.ds(r, ATTN_BLOCK, stride=dil)
        for j, a in enumerate(accs):
            acc_out[j, rows, :] = a
        ml_out[rows, :] = ml_new


def _attn_last_kernel(q_ref, kc_ref, vc_ref, kp_ref, vp_ref, acc_in, ml_in, a_out):
    gb = q_ref.shape[0]
    band, valid0 = _band_mask(pl.program_id(1) > 0)
    for i in range(gb):
        if i == 0:
            k_prev, v_prev, valid = kp_ref[...], vp_ref[...], valid0
        else:
            k_prev, v_prev, valid = kc_ref[i - 1], vc_ref[i - 1], band
        kk = jnp.concatenate([k_prev, kc_ref[i]], axis=0)
        vv = jnp.concatenate([v_prev, vc_ref[i]], axis=0)
        rows = slice(i * ATTN_BLOCK, (i + 1) * ATTN_BLOCK)
        acc_old = [acc_in[j, rows, :] for j in range(ATTN_WIDTH // LANES)]
        acc_loc, ml_loc = _attn_item(q_ref[i], kk, vv, valid)
        accs, ml_new = _attn_merge(acc_old, ml_in[i], acc_loc, ml_loc)
        lane = lax.broadcasted_iota(jnp.int32, ml_new.shape, 1)
        is_l = (lane >= ATTN_HEADS) & (lane < 2 * ATTN_HEADS)
        linv = 1.0 / jnp.where(is_l, ml_new, 1.0)
        for j, a in enumerate(accs):
            a_out[i, :, j * LANES:(j + 1) * LANES] = (
                a * _head_columns(linv, ATTN_HEADS, j)).astype(BF16)


def _dilated_attention(qkv):
    (_, d1), (_, dm), (_, dw) = DILATED_PAIRS
    assert d1 == 1 and dw % dm == 0
    blk, aw = ATTN_BLOCK, ATTN_WIDTH
    (q1, k1, v1), (qm, km, vm), (qw, kw, vw) = qkv
    bsz, seq, _ = q1.shape

    nw = qw.shape[1]
    rpg = min(ATTN_GROUP, dw)
    cur = lambda w: pl.BlockSpec((None, None, rpg, blk, w), lambda b, n, g: (b, n, g, 0, 0))
    prev = pl.BlockSpec((None, None, rpg, blk, aw),
                        lambda b, n, g: (b, jnp.maximum(n - 1, 0), g, 0, 0))
    acc_w, ml_w = pl.pallas_call(
        _attn_wide_kernel,
        grid=(bsz, nw, dw // rpg),
        in_specs=[cur(aw), cur(aw), cur(aw), prev, prev],
        out_specs=[cur(aw), cur(LANES)],
        out_shape=[jax.ShapeDtypeStruct((bsz, nw, dw, blk, aw), F32),
                   jax.ShapeDtypeStruct((bsz, nw, dw, blk, LANES), F32)],
        compiler_params=_params(3),
        name=f"attn_dil{dw}",
    )(qw, kw, vw, kw, vw)

    nm = qm.shape[1]
    ratio = dw // dm
    sub = blk // ratio
    cur = pl.BlockSpec((None, None, dm, blk, aw), lambda b, n: (b, n, 0, 0, 0))
    prev = pl.BlockSpec((None, None, dm, blk, aw),
                        lambda b, n: (b, jnp.maximum(n - 1, 0), 0, 0, 0))
    wide = lambda w: pl.BlockSpec((None, None, dw, sub, w),
                                  lambda b, n: (b, n // ratio, 0, n % ratio, 0))
    n_col = aw // LANES
    acc, ml = pl.pallas_call(
        _attn_mid_kernel,
        grid=(bsz, nm),
        in_specs=[cur, cur, cur, prev, prev, wide(aw), wide(LANES)],
        out_specs=[pl.BlockSpec((None, n_col, dm * blk, LANES), lambda b, n: (b, 0, n, 0)),
                   pl.BlockSpec((None, dm * blk, LANES), lambda b, n: (b, n, 0))],
        out_shape=[jax.ShapeDtypeStruct((bsz, n_col, seq, LANES), F32),
                   jax.ShapeDtypeStruct((bsz, seq, LANES), F32)],
        scratch_shapes=[pltpu.VMEM((n_col, blk, LANES), F32), pltpu.VMEM((blk, LANES), F32)],
        compiler_params=_params(2),
        name=f"attn_dil{dm}",
    )(qm, km, vm, km, vm, acc_w, ml_w)

    nb = seq // blk
    gb = min(ATTN_GROUP, nb)
    view = lambda t: t.reshape(bsz, nb, blk, t.shape[-1])
    cur = lambda w: pl.BlockSpec((None, gb, blk, w), lambda b, n: (b, n, 0, 0))
    prev = pl.BlockSpec((None, None, blk, aw), lambda b, n: (b, jnp.maximum(n * gb - 1, 0), 0, 0))
    a_out = pl.pallas_call(
        _attn_last_kernel,
        grid=(bsz, nb // gb),
        in_specs=[cur(aw), cur(aw), cur(aw), prev, prev,
                  pl.BlockSpec((None, n_col, gb * blk, LANES), lambda b, n: (b, 0, n, 0)),
                  cur(LANES)],
        out_specs=cur(aw),
        out_shape=jax.ShapeDtypeStruct((bsz, nb, blk, aw), BF16),
        compiler_params=_params(2),
        name=f"attn_dil{d1}",
    )(view(q1), view(k1), view(v1), view(k1), view(v1), acc, view(ml))
    return a_out.reshape(bsz, seq, aw)


def _outproj_kernel(a_ref, b_ref, x_ref, wa_ref, wb_ref, gm_ref, g_ref, sh_ref, sc_ref,
                    x1_ref, h_ref):
    mix = _dot(a_ref[...], wa_ref[...]) + _dot(b_ref[...], wb_ref[...])
    x1 = x_ref[...] + gm_ref[...] * mix
    x1_ref[...] = x1
    h_ref[...] = _ada_norm(x1, g_ref[...], sh_ref[...], sc_ref[...]).astype(BF16)


def _even_outproj(a, b, x, w_out, gate_m, g, shift, scale):
    bsz, seq, d = x.shape
    aw, bw = a.shape[2], b.shape[2]
    tm = ROW_TILE
    row = lambda w: pl.BlockSpec((None, tm, w), lambda bi, i: (bi, i, 0))
    per_b = pl.BlockSpec((None, 1, d), lambda bi, i: (bi, 0, 0))
    full = lambda s: pl.BlockSpec(s, lambda bi, i: (0, 0))
    pb = lambda t: t.reshape(bsz, 1, d)
    return pl.pallas_call(
        _outproj_kernel,
        grid=(bsz, seq // tm),
        in_specs=[row(aw), row(bw), row(d), full((aw, d)), full((bw, d)),
                  per_b, full((1, d)), per_b, per_b],
        out_specs=[row(d), row(d)],
        out_shape=[jax.ShapeDtypeStruct((bsz, seq, d), F32),
                   jax.ShapeDtypeStruct((bsz, seq, d), BF16)],
        compiler_params=_params(2),
        name="even_outproj",
    )(a, b, x, w_out[:aw], w_out[aw:], pb(gate_m), g.reshape(1, d), pb(shift), pb(scale))


def _ffn_kernel(h_ref, x_ref, wg_ref, wu_ref, wd_ref, gf_ref, g_ref, sh_ref, sc_ref,
                o_ref, hn_ref):
    h = h_ref[...]
    g = _dot(h, wg_ref[...])
    u = _dot(h, wu_ref[...])
    act = (g * _sigmoid(g) * u).astype(BF16)
    x2 = x_ref[...] + gf_ref[...] * _dot(act, wd_ref[...])
    o_ref[...] = x2
    hn_ref[...] = _ada_norm(x2, g_ref[...], sh_ref[...], sc_ref[...]).astype(BF16)


def _dense_ffn(h, x, w_gate, w_up, w_down, gate_f, next_g, next_shift, next_scale):
    bsz, seq, d = x.shape
    ff = w_gate.shape[1]
    tm = ROW_TILE
    row = pl.BlockSpec((None, tm, d), lambda b, i: (b, i, 0))
    per_b = pl.BlockSpec((None, 1, d), lambda b, i: (b, 0, 0))
    full = lambda s: pl.BlockSpec(s, lambda b, i: (0, 0), pipeline_mode=pl.Buffered(1))
    pb = lambda t: t.reshape(bsz, 1, d)
    return pl.pallas_call(
        _ffn_kernel,
        grid=(bsz, seq // tm),
        in_specs=[row, row, full((d, ff)), full((d, ff)), full((ff, d)), per_b,
                  pl.BlockSpec((1, d), lambda b, i: (0, 0)), per_b, per_b],
        out_specs=[row, row],
        out_shape=[jax.ShapeDtypeStruct((bsz, seq, d), F32),
                   jax.ShapeDtypeStruct((bsz, seq, d), BF16)],
        compiler_params=_params(2),
        name="dense_ffn",
    )(h, x, w_gate, w_up, w_down, pb(gate_f), next_g.reshape(1, d), pb(next_shift),
      pb(next_scale))


def _ssm_operators(a_re, a_im, log_step, b_re, b_im, c_re, c_im):
    L = SSM_CHUNK
    a_re, a_im = a_re.astype(F32), a_im.astype(F32)
    dt = jnp.exp(log_step.astype(F32))[:, None]
    mag = jnp.exp(a_re * dt)
    abar_re = mag * jnp.cos(a_im * dt)
    abar_im = mag * jnp.sin(a_im * dt)
    den = a_re * a_re + a_im * a_im
    nr = abar_re - 1.0
    f_re = (nr * a_re + abar_im * a_im) / den
    f_im = (abar_im * a_re - nr * a_im) / den
    b_re, b_im = b_re.astype(F32), b_im.astype(F32)
    bb_re = f_re[..., None] * b_re - f_im[..., None] * b_im
    bb_im = f_re[..., None] * b_im + f_im[..., None] * b_re
    c_re, c_im = c_re.astype(F32), c_im.astype(F32)

    def step(carry, _):
        pr, pi = carry
        nxt = (pr * abar_re - pi * abar_im, pr * abar_im + pi * abar_re)
        return nxt, carry
    (pl_re, pl_im), (pw_re, pw_im) = lax.scan(
        step, (jnp.ones_like(abar_re), jnp.zeros_like(abar_re)), None, length=L)
    pw_re = jnp.concatenate([pw_re, pl_re[None]], axis=0)
    pw_im = jnp.concatenate([pw_im, pl_im[None]], axis=0)

    hi = lax.Precision.HIGHEST
    cp_re = c_re[None] * pw_re[:L, :, None, :] - c_im[None] * pw_im[:L, :, None, :]
    cp_im = c_re[None] * pw_im[:L, :, None, :] + c_im[None] * pw_re[:L, :, None, :]
    w = (jnp.einsum('tgcp,gpd->tgdc', cp_re, bb_re, precision=hi)
         - jnp.einsum('tgcp,gpd->tgdc', cp_im, bb_im, precision=hi))
    s_idx = jnp.arange(L)[:, None]
    t_idx = jnp.arange(L)[None, :]
    tau = t_idx - s_idx
    toep = jnp.where((tau >= 0)[:, :, None, None, None], w[jnp.clip(tau, 0, L - 1)], 0.0)
    n_g, n_c = a_re.shape[0], b_re.shape[2]
    toep = toep.transpose(2, 0, 3, 1, 4).reshape(n_g, L * n_c, L * n_c)

    rp_re, rp_im = pw_re[:L][::-1], pw_im[:L][::-1]
    so_re = rp_re[..., None] * bb_re[None] - rp_im[..., None] * bb_im[None]
    so_im = rp_re[..., None] * bb_im[None] + rp_im[..., None] * bb_re[None]
    to_rows = lambda t: t.transpose(1, 0, 3, 2).reshape(n_g, L * n_c, -1)
    s_out = jnp.concatenate([to_rows(so_re), to_rows(so_im)], axis=-1)
    s_out_sw = jnp.concatenate([to_rows(so_im), to_rows(so_re)], axis=-1)

    qp_re, qp_im = pw_re[1:], pw_im[1:]
    ci_re = c_re[None] * qp_re[:, :, None, :] - c_im[None] * qp_im[:, :, None, :]
    ci_im = c_re[None] * qp_im[:, :, None, :] + c_im[None] * qp_re[:, :, None, :]
    to_cols = lambda t: t.transpose(1, 3, 0, 2).reshape(n_g, -1, L * n_c)
    c_in = jnp.concatenate([to_cols(ci_re), -to_cols(ci_im)], axis=1)

    al_re, al_im = pw_re[L], pw_im[L]
    a1 = jnp.concatenate([al_re, al_re], axis=-1)[:, None, :]
    a2 = jnp.concatenate([-al_im, al_im], axis=-1)[:, None, :]
    return (toep.astype(BF16), s_out.astype(BF16), s_out_sw.astype(BF16),
            c_in.astype(BF16), a1, a2)


def _piece_transpose(arrs):
    n = len(arrs)
    piece = lax.broadcasted_iota(jnp.int32, (1, LANES), 1) // SSM_GROUP
    arrs = list(arrs)
    dist = n // 2
    while dist >= 1:
        keep = (piece & dist) == 0
        for i in range(n):
            if i & dist == 0:
                a, b = arrs[i], arrs[i + dist]
                arrs[i] = jnp.where(keep, a, pltpu.roll(b, dist * SSM_GROUP, 1))
                arrs[i + dist] = jnp.where(keep, pltpu.roll(a, LANES - dist * SSM_GROUP, 1), b)
        dist //= 2
    return arrs


def _ssm_kernel(u_ref, t_ref, s_ref, ssw_ref, cin_ref, a1_ref, a2_ref, y_ref,
                st_ref, stsw_ref, xs_ref, xc_ref, zc_ref):
    kt, L, bsz, _ = u_ref.shape
    rows = kt * bsz
    n_lg = SSM_LANE_GROUPS

    @pl.when(pl.program_id(1) == 0)
    def _():
        xc_ref[...] = jnp.zeros(xc_ref.shape, F32)
        zc_ref[...] = jnp.zeros(zc_ref.shape, F32)

    zs = [u_ref[:, s].reshape(rows, LANES) for s in range(L)]
    lo = _piece_transpose(zs[:n_lg])
    hi = _piece_transpose(zs[n_lg:])
    ys = []
    for gl in range(n_lg):
        v = jnp.concatenate([lo[gl], hi[gl]], axis=1)
        st_ref[...] = _dot(v, s_ref[gl])
        stsw_ref[...] = _dot(v, ssw_ref[gl])
        a1 = jnp.broadcast_to(a1_ref[gl], (bsz, 2 * SSM_STATE))
        a2 = jnp.broadcast_to(a2_ref[gl], (bsz, 2 * SSM_STATE))

        def body(k, carry):
            x, z = carry
            r = pl.ds(pl.multiple_of(k * bsz, bsz), bsz)
            xs_ref[r, :] = x
            x_new = x * a1 + z * a2 + st_ref[r, :]
            z_new = z * a1 - x * a2 + stsw_ref[r, :]
            return x_new, z_new

        x, z = lax.fori_loop(0, kt, body, (xc_ref[gl], zc_ref[gl]), unroll=8)
        xc_ref[gl] = x
        zc_ref[gl] = z
        ys.append((_dot(v, t_ref[gl]) + _dot(xs_ref[...].astype(BF16), cin_ref[gl])).astype(BF16))
    out_lo = _piece_transpose([y[:, :LANES] for y in ys])
    out_hi = _piece_transpose([y[:, LANES:] for y in ys])
    for t in range(n_lg):
        y_ref[:, t] = out_lo[t].reshape(kt, bsz, LANES)
        y_ref[:, n_lg + t] = out_hi[t].reshape(kt, bsz, LANES)


def _ssm_scan(h, ops):
    bsz, seq, d = h.shape
    toep, s_out, s_out_sw, c_in, a1, a2 = ops
    L, C, n_lg = SSM_CHUNK, SSM_GROUP, SSM_LANE_GROUPS
    assert L == 2 * n_lg
    n_k = seq // L
    kt = min(SSM_CHUNKS_PER_STEP, n_k)
    hv = h.transpose(1, 0, 2).reshape(n_k, L, bsz, d)
    act = pl.BlockSpec((kt, L, bsz, LANES), lambda o, k: (k, 0, 0, o))
    wblk = lambda s: pl.BlockSpec((n_lg,) + s, lambda o, k: (o, 0, 0))
    y = pl.pallas_call(
        _ssm_kernel,
        grid=(d // LANES, n_k // kt),
        in_specs=[act, wblk((L * C, L * C)), wblk((L * C, 2 * SSM_STATE)),
                  wblk((L * C, 2 * SSM_STATE)), wblk((2 * SSM_STATE, L * C)),
                  wblk((1, 2 * SSM_STATE)), wblk((1, 2 * SSM_STATE))],
        out_specs=act,
        out_shape=jax.ShapeDtypeStruct((n_k, L, bsz, d), BF16),
        scratch_shapes=[pltpu.VMEM((kt * bsz, 2 * SSM_STATE), F32)] * 3
        + [pltpu.VMEM((n_lg, bsz, 2 * SSM_STATE), F32)] * 2,
        compiler_params=_params(2),
        name="ssm_scan",
    )(hv, toep, s_out, s_out_sw, c_in, a1, a2)
    return y.reshape(seq, bsz, d).transpose(1, 0, 2)


def _gelu_tanh(x):
    return 0.5 * x * (1.0 + jnp.tanh(math.sqrt(2.0 / math.pi) * (x + 0.044715 * (x * x * x))))


def _ssm_out_kernel(y_ref, h_ref, x_ref, d_ref, w_ref, b_ref, gm_ref, g_ref, sh_ref, sc_ref,
                    rw_ref, rb_ref, x1_ref, h2_ref, route_ref):
    d = x_ref.shape[1]
    y = y_ref[...].astype(F32) + d_ref[...] * h_ref[...].astype(F32)
    z = _dot(_gelu_tanh(y).astype(BF16), w_ref[...]) + b_ref[...]
    mix = z[:, :d] * _sigmoid(z[:, d:])
    x1 = x_ref[...] + gm_ref[...] * mix
    x1_ref[...] = x1
    h2 = _ada_norm(x1, g_ref[...], sh_ref[...], sc_ref[...])
    h2_ref[...] = h2

    hh, hl = _split_bf16(h2)
    both = _dot(hh, rw_ref[...])
    logits = (both[:, :LANES] + both[:, LANES:] + _dot(hl, rw_ref[:, :LANES]) + rb_ref[...])
    lane = lax.broadcasted_iota(jnp.int32, logits.shape, 1)
    lane_f = lane.astype(F32)
    neg_inf = jnp.float32(-jnp.inf)
    logits = jnp.where(lane < N_EXPERTS, logits, neg_inf)
    m1 = jnp.max(logits, axis=1, keepdims=True)
    i1 = jnp.min(jnp.where(logits == m1, lane_f, float(LANES)), axis=1, keepdims=True)
    rest = jnp.where(lane_f == i1, neg_inf, logits)
    m2 = jnp.max(rest, axis=1, keepdims=True)
    i2 = jnp.min(jnp.where(rest == m2, lane_f, float(LANES)), axis=1, keepdims=True)
    e2 = jnp.exp(m2 - m1)
    g1 = 1.0 / (1.0 + e2)
    g2 = e2 / (1.0 + e2)
    route = jnp.where(lane == 0, i1, jnp.where(lane == 1, i2,
                      jnp.where(lane == 2, g1, jnp.where(lane == 3, g2, 0.0))))
    route_ref[...] = route


def _ssm_out(y, h, x, d_skip, glu_w, glu_b, gate_m, g, shift, scale, router_w, router_b):
    bsz, seq, d = x.shape
    tm = ROW_TILE
    row = lambda w: pl.BlockSpec((None, tm, w), lambda b, i: (b, i, 0))
    per_b = pl.BlockSpec((None, 1, d), lambda b, i: (b, 0, 0))
    full = lambda s: pl.BlockSpec(s, lambda b, i: (0, 0))
    pb = lambda t: t.reshape(bsz, 1, d)
    rw = jnp.zeros((d, LANES), F32).at[:, :N_EXPERTS].set(router_w)
    rwh = rw.astype(BF16)
    rw_cat = jnp.concatenate([rwh, (rw - rwh.astype(F32)).astype(BF16)], axis=1)
    rb = jnp.zeros((1, LANES), F32).at[0, :N_EXPERTS].set(router_b)
    return pl.pallas_call(
        _ssm_out_kernel,
        grid=(bsz, seq // tm),
        in_specs=[row(d), row(d), row(d), full((1, d)), full((d, 2 * d)), full((1, 2 * d)),
                  per_b, full((1, d)), per_b, per_b,
                  full((d, 2 * LANES)), full((1, LANES))],
        out_specs=[row(d), row(d), row(LANES)],
        out_shape=[jax.ShapeDtypeStruct((bsz, seq, d), F32),
                   jax.ShapeDtypeStruct((bsz, seq, d), F32),
                   jax.ShapeDtypeStruct((bsz, seq, LANES), F32)],
        compiler_params=_params(2),
        name="ssm_out",
    )(y, h, x, d_skip.reshape(1, d), glu_w, glu_b.reshape(1, 2 * d), pb(gate_m),
      g.reshape(1, d), pb(shift), pb(scale), rw_cat, rb)


def _routing_tables(route, tm):
    n = route.shape[0]
    i32 = jnp.int32
    e_flat = route[:, :2].astype(i32).T.reshape(-1)
    onehot = (e_flat[:, None] == jnp.arange(N_EXPERTS)[None, :]).astype(i32)
    incl = jnp.cumsum(onehot, axis=0)
    counts = incl[-1]
    rank = jnp.sum((incl - onehot) * onehot, axis=1)
    padded = ((counts + tm - 1) // tm) * tm
    ends = jnp.cumsum(padded)
    starts = ends - padded
    dest = starts[e_flat] + rank
    n_tiles = (2 * n) // tm + N_EXPERTS
    n_rows = n_tiles * tm
    s = jnp.arange(n_rows, dtype=i32)
    e_of_s = jnp.sum((s[:, None] >= ends[None, :]).astype(i32), axis=1)
    starts9 = jnp.concatenate([starts, ends[-1:]])
    counts9 = jnp.concatenate([counts, jnp.zeros((1,), i32)])
    ustart9 = jnp.concatenate([jnp.cumsum(counts) - counts, jnp.full((1,), 2 * n, i32)])
    real_before = ustart9[e_of_s] + jnp.minimum(s - starts9[e_of_s], counts9[e_of_s])
    pair = (2 * n + s - real_before).at[dest].set(
        jnp.arange(2 * n, dtype=i32), unique_indices=True, mode="promise_in_bounds")
    src = jnp.concatenate([jnp.where(pair < 2 * n, pair % n, 0), jnp.zeros((tm,), i32)])
    dst = jnp.concatenate([n_rows + jnp.arange(tm, dtype=i32), pair])
    tile_expert = jnp.minimum(e_of_s[::tm], N_EXPERTS - 1)
    return src, dst, tile_expert


def _moe_kernel(te_ref, src_ref, src_next_ref, dst_prev_ref, dst_ref, h_hbm, wg_ref, wu_ref, wd_ref,
                y_hbm, hbuf, ybuf, gsem, ssem):
    t = pl.program_id(0)
    n_t = pl.num_programs(0)
    tm = hbuf.shape[1]
    ff = wg_ref.shape[1]
    fc = ff // MOE_FF_CHUNKS
    slot = t % 2
    other = 1 - slot

    def gather(slot_, r, row):
        return pltpu.make_async_copy(h_hbm.at[pl.ds(row, 1)], hbuf.at[slot_, pl.ds(r, 1)],
                                     gsem.at[slot_])

    def scatter(slot_, r, row):
        return pltpu.make_async_copy(ybuf.at[slot_, pl.ds(r, 1)], y_hbm.at[pl.ds(row, 1)],
                                     ssem.at[slot_])

    def for_rows(fn):
        def body(r, c):
            fn(r)
            return c
        lax.fori_loop(0, tm, body, 0, unroll=8)

    @pl.when(t == 0)
    def _():
        ybuf[...] = jnp.zeros(ybuf.shape, F32)
        for_rows(lambda r: gather(0, r, src_ref[r]).start())

    for_rows(lambda r: gather(slot, r, 0).wait())

    @pl.when(t > 0)
    def _():
        for_rows(lambda r: scatter(slot, r, 0).wait())

    h = hbuf[slot].astype(BF16)
    for r in range(tm):
        gather(other, r, src_next_ref[r]).start()
    for r in range(tm):
        scatter(other, r, dst_prev_ref[r]).start()

    acc = jnp.zeros((tm, wd_ref.shape[1]), F32)
    for c in range(MOE_FF_CHUNKS):
        g = _dot(h, wg_ref[:, c * fc:(c + 1) * fc])
        u = _dot(h, wu_ref[:, c * fc:(c + 1) * fc])
        act = (g * _sigmoid(g) * u).astype(BF16)
        acc = acc + _dot(act, wd_ref[c * fc:(c + 1) * fc, :])
    ybuf[slot] = acc

    @pl.when(t == n_t - 1)
    def _():
        for_rows(lambda r: gather(other, r, 0).wait())
        for_rows(lambda r: scatter(other, r, 0).wait())
        for_rows(lambda r: scatter(slot, r, dst_ref[r]).start())
        for_rows(lambda r: scatter(slot, r, 0).wait())


def _moe_experts(h2, src, dst, tile_expert, w_gate, w_up, w_down):
    n, d = h2.shape
    ff = w_gate.shape[2]
    tm = MOE_TILE
    n_tiles = tile_expert.shape[0]
    wspec = lambda s: pl.BlockSpec((None,) + s, lambda t, te: (te[t], 0, 0),
                                   pipeline_mode=pl.Buffered(1))
    idx = lambda off: pl.BlockSpec((tm,), lambda t, te: (t + off,), memory_space=pltpu.SMEM)
    grid_spec = pltpu.PrefetchScalarGridSpec(
        num_scalar_prefetch=1,
        grid=(n_tiles,),
        in_specs=[idx(0), idx(1), idx(0), idx(1),
                  pl.BlockSpec(memory_space=pl.ANY),
                  wspec((d, ff)), wspec((d, ff)), wspec((ff, d))],
        out_specs=pl.BlockSpec(memory_space=pl.ANY),
        scratch_shapes=[pltpu.VMEM((2, tm, d), F32), pltpu.VMEM((2, tm, d), F32),
                        pltpu.SemaphoreType.DMA((2,)), pltpu.SemaphoreType.DMA((2,))],
    )
    return pl.pallas_call(
        _moe_kernel,
        grid_spec=grid_spec,
        out_shape=jax.ShapeDtypeStruct((n_tiles * tm + tm, d), F32),
        compiler_params=_params(1),
        name="moe_experts",
    )(tile_expert, src, src, dst, dst, h2, w_gate, w_up, w_down)


def _combine_kernel(y1_ref, y2_ref, route_ref, x_ref, gf_ref, fg_ref, o_ref, *, final):
    route = route_ref[...]
    ff = route[:, 2:3] * y1_ref[...] + route[:, 3:4] * y2_ref[...]
    x2 = x_ref[...] + gf_ref[...] * ff
    if final:
        ms = jnp.mean(x2 * x2, axis=-1, keepdims=True)
        x2 = x2 * lax.rsqrt(ms + RMS_EPS) * fg_ref[...]
    o_ref[...] = x2


def _moe_combine(y, route, x1, gate_f, final_g, final):
    bsz, seq, d = x1.shape
    n = bsz * seq
    tm = ROW_TILE
    per_batch = seq // tm
    out = pl.pallas_call(
        functools.partial(_combine_kernel, final=final),
        grid=(n // tm,),
        in_specs=[pl.BlockSpec((tm, d), lambda i: (i, 0)),
                  pl.BlockSpec((tm, d), lambda i: (n // tm + i, 0)),
                  pl.BlockSpec((tm, LANES), lambda i: (i, 0)),
                  pl.BlockSpec((tm, d), lambda i: (i, 0)),
                  pl.BlockSpec((None, 1, d), lambda i: (i // per_batch, 0, 0)),
                  pl.BlockSpec((1, d), lambda i: (0, 0))],
        out_specs=pl.BlockSpec((tm, d), lambda i: (i, 0)),
        out_shape=jax.ShapeDtypeStruct((n, d), F32),
        compiler_params=_params(1),
        name="moe_combine",
    )(y, y, route.reshape(n, LANES), x1.reshape(n, d),
      gate_f.reshape(bsz, 1, d), final_g.reshape(1, d))
    return out.reshape(bsz, seq, d)


def _even_layer(x, mods, norm_mix_g, norm_ffn_g, cos, sin, w_in, conv_w, w_out,
                w_gate, w_up, w_down, next_norm):
    sh_m, sc_m, g_m, sh_f, sc_f, g_f = mods
    qkv, b_out = _even_inproj(x, norm_mix_g, sh_m, sc_m, w_in.astype(BF16), cos, sin, conv_w)
    a_out = _dilated_attention(qkv)
    x1, h2 = _even_outproj(a_out, b_out, x, w_out.astype(BF16), g_m, norm_ffn_g, sh_f, sc_f)
    return _dense_ffn(h2, x1, w_gate.astype(BF16), w_up.astype(BF16), w_down.astype(BF16), g_f,
                      *next_norm)


def _odd_layer(x, h, mods, norm_ffn_g, ssm, d_skip, glu_w, glu_b,
               router_w, router_b, w_gate, w_up, w_down, final_g, final):
    _, _, g_m, sh_f, sc_f, g_f = mods
    bsz, seq, d = x.shape
    y = _ssm_scan(h, _ssm_operators(*ssm))
    x1, h2, route = _ssm_out(y, h, x, d_skip, glu_w.astype(BF16), glu_b, g_m,
                             norm_ffn_g, sh_f, sc_f, router_w, router_b)
    route = route.reshape(bsz * seq, LANES)
    src, dst, tile_expert = _routing_tables(route, MOE_TILE)
    y_pairs = _moe_experts(h2.reshape(bsz * seq, d), src, dst, tile_expert,
                           w_gate.astype(BF16), w_up.astype(BF16), w_down.astype(BF16))
    return _moe_combine(y_pairs, route, x1, g_f, final_g, final)


def kernel(x, c, positions, mod_w, mod_b, norm_mix_g, norm_ffn_g, ev_w_in, ev_conv_w, ev_w_out, ffn_w_gate, ffn_w_up, ffn_w_down, ssm_a_re, ssm_a_im, ssm_log_step, ssm_b_re, ssm_b_im, ssm_c_re, ssm_c_im, ssm_d, glu_w, glu_b, moe_router_w, moe_router_b, moe_w_gate, moe_w_up, moe_w_down, final_norm_g):
    depth = mod_w.shape[0]
    d = x.shape[2]
    assert depth % 2 == 0
    mod = _modulation(c, mod_w, mod_b)
    cos, sin = _rope_tables(positions)
    layer_mods = [[mod[layer, :, j * d:(j + 1) * d] for j in range(6)] for layer in range(depth)]
    h = None
    for layer in range(depth):
        mods = layer_mods[layer]
        i = layer // 2
        if layer % 2 == 0:
            nxt = layer_mods[layer + 1]
            x, h = _even_layer(x, mods, norm_mix_g[layer], norm_ffn_g[layer], cos, sin,
                               ev_w_in[i], ev_conv_w[i], ev_w_out[i],
                               ffn_w_gate[i], ffn_w_up[i], ffn_w_down[i],
                               (norm_mix_g[layer + 1], nxt[0], nxt[1]))
        else:
            ssm = (ssm_a_re[i], ssm_a_im[i], ssm_log_step[i], ssm_b_re[i], ssm_b_im[i],
                   ssm_c_re[i], ssm_c_im[i])
            x = _odd_layer(x, h, mods, norm_ffn_g[layer], ssm, ssm_d[i],
                           glu_w[i], glu_b[i], moe_router_w[i], moe_router_b[i],
                           moe_w_gate[i], moe_w_up[i], moe_w_down[i],
                           final_norm_g, layer == depth - 1)
    return x
```

```python
import functools
import math

import jax
import jax.numpy as jnp
from jax import lax
from jax.experimental import pallas as pl
from jax.experimental.pallas import tpu as pltpu

F32 = jnp.float32
BF16 = jnp.bfloat16

ATTN_HEADS = 8
HEAD_DIM = 64
ATTN_WIDTH = ATTN_HEADS * HEAD_DIM
ROPE_DIM = HEAD_DIM // 4
ROPE_THETA = 500000.0
DILATED_PAIRS = ((128, 1), (512, 4), (2048, 16))
ATTN_BLOCK = 128
SSM_GROUP = 16
SSM_STATE = 64
SSM_CHUNK = 16
N_EXPERTS = 8
RMS_EPS = 1e-6

LANES = 128
VMEM_LIMIT_BYTES = 56 * 1024 * 1024

ROW_TILE = 512
MOE_TILE = 512
MOE_FF_CHUNKS = 2
ATTN_GROUP = 4
SSM_CHUNKS_PER_STEP = 64
SSM_LANE_GROUPS = LANES // SSM_GROUP


def _params(n_axes, vmem=VMEM_LIMIT_BYTES):
    return pltpu.CompilerParams(
        dimension_semantics=("arbitrary",) * n_axes, vmem_limit_bytes=vmem)


def _dot(a, b):
    return jnp.dot(a, b, preferred_element_type=F32)


def _sigmoid(x):
    return 1.0 / (1.0 + jnp.exp(-x))


def _split_bf16(x):
    hi = x.astype(BF16)
    lo = (x - hi.astype(F32)).astype(BF16)
    return hi, lo


def _ada_norm(x, g, shift, scale):
    ms = jnp.mean(x * x, axis=-1, keepdims=True)
    return x * lax.rsqrt(ms + RMS_EPS) * g * (1.0 + scale) + shift


def _mod_kernel(c_ref, w_ref, b_ref, o_ref):
    c = c_ref[...]
    cond = c * _sigmoid(c)
    ch, cl = _split_bf16(cond)
    wh, wl = _split_bf16(w_ref[...])
    o_ref[...] = _dot(ch, wh) + _dot(cl, wh) + _dot(ch, wl) + b_ref[...]


def _modulation(c, mod_w, mod_b):
    depth, d, n = mod_w.shape
    bsz = c.shape[0]
    tn = 1024
    return pl.pallas_call(
        _mod_kernel,
        grid=(depth, n // tn),
        in_specs=[pl.BlockSpec((bsz, d), lambda l, j: (0, 0)),
                  pl.BlockSpec((None, d, tn), lambda l, j: (l, 0, j)),
                  pl.BlockSpec((None, 1, tn), lambda l, j: (l, 0, j))],
        out_specs=pl.BlockSpec((None, bsz, tn), lambda l, j: (l, 0, j)),
        out_shape=jax.ShapeDtypeStruct((depth, bsz, n), F32),
        compiler_params=_params(2),
        name="modulation",
    )(c, mod_w, mod_b.reshape(depth, 1, n))


def _rope_kernel(pos_ref, inv_ref, cos_ref, sin_ref):
    ang = pos_ref[...].astype(F32) * inv_ref[...]
    cos_ref[...] = jnp.cos(ang)
    sin_ref[...] = jnp.sin(ang)


def _rope_tables(positions):
    bsz, seq = positions.shape
    inv = ROPE_THETA ** (-jnp.arange(0, ROPE_DIM, 2, dtype=F32) / ROPE_DIM)
    lane = jnp.arange(LANES) % HEAD_DIM
    inv_lane = jnp.where(lane < ROPE_DIM, inv[lane % (ROPE_DIM // 2)], 0.0).reshape(1, LANES)
    tm = ROW_TILE
    spec = pl.BlockSpec((None, tm, LANES), lambda b, i: (b, i, 0))
    return pl.pallas_call(
        _rope_kernel,
        grid=(bsz, seq // tm),
        in_specs=[pl.BlockSpec((None, tm, 1), lambda b, i: (b, i, 0)),
                  pl.BlockSpec((1, LANES), lambda b, i: (0, 0))],
        out_specs=[spec, spec],
        out_shape=[jax.ShapeDtypeStruct((bsz, seq, LANES), F32)] * 2,
        compiler_params=_params(2),
        name="rope_tables",
    )(positions.reshape(bsz, seq, 1), inv_lane)


def _inproj_kernel(x_ref, g_ref, sh_ref, sc_ref, w_ref, cos_ref, sin_ref, cw_ref, *rest):
    n_pat = len(DILATED_PAIRS)
    qkv_refs = [rest[3 * p:3 * p + 3] for p in range(n_pat)]
    bo_ref, ubuf, stage = rest[3 * n_pat:]
    tm = x_ref.shape[0]
    aw = ATTN_WIDTH
    cwid = cw_ref.shape[1]
    h = _ada_norm(x_ref[...], g_ref[...], sh_ref[...], sc_ref[...]).astype(BF16)
    proj = _dot(h, w_ref[...])

    cos = cos_ref[...]
    sin = sin_ref[...]
    lane = lax.broadcasted_iota(jnp.int32, (tm, LANES), 1) % HEAD_DIM
    first_half = lane < ROPE_DIM // 2

    def rope(t):
        rot = jnp.where(first_half,
                        -pltpu.roll(t, LANES - ROPE_DIM // 2, 1),
                        pltpu.roll(t, ROPE_DIM // 2, 1))
        return t * cos + rot * sin

    n_col = aw // LANES
    for j in range(n_col):
        sl = slice(j * LANES, (j + 1) * LANES)
        stage[0, j] = rope(proj[:, sl]) * (HEAD_DIM ** -0.5)
        stage[1, j] = rope(proj[:, aw + j * LANES: aw + (j + 1) * LANES])
        stage[2, j] = proj[:, 2 * aw + j * LANES: 2 * aw + (j + 1) * LANES]
    for (_, dil), refs in zip(DILATED_PAIRS, qkv_refs):
        for which, ref in enumerate(refs):
            for j in range(n_col):
                sl = slice(j * LANES, (j + 1) * LANES)
                if dil == 1:
                    ref[:, sl] = stage[which, j].astype(BF16)
                else:
                    for r in range(dil):
                        ref[r, :, sl] = stage[which, j, pl.ds(r, tm // dil, stride=dil), :].astype(BF16)

    b_gate = proj[:, 3 * aw:3 * aw + cwid]
    c_gate = proj[:, 3 * aw + cwid:3 * aw + 2 * cwid]
    xin = proj[:, 3 * aw + 2 * cwid:]
    u = c_gate * xin

    @pl.when(pl.program_id(1) == 0)
    def _():
        ubuf[0:8, :] = jnp.zeros((8, cwid), F32)

    ubuf[8:, :] = u
    conv = (cw_ref[0:1, :] * ubuf[6:6 + tm, :] + cw_ref[1:2, :] * ubuf[7:7 + tm, :]
            + cw_ref[2:3, :] * u)
    bo_ref[...] = (b_gate * conv).astype(BF16)
    ubuf[0:8, :] = ubuf[tm:tm + 8, :]


def _even_inproj(x, g, shift, scale, w_in, cos, sin, conv_w):
    bsz, seq, d = x.shape
    n = w_in.shape[1]
    cwid = conv_w.shape[1]
    tm = ROW_TILE
    aw = ATTN_WIDTH
    row = lambda w: pl.BlockSpec((None, tm, w), lambda b, i: (b, i, 0))
    per_b = pl.BlockSpec((None, 1, d), lambda b, i: (b, 0, 0))
    qkv_specs, qkv_shapes = [], []
    for window, dil in DILATED_PAIRS:
        assert window // dil == ATTN_BLOCK and seq % window == 0 and (dil == 1 or window % tm == 0)
        if dil == 1:
            spec, shape = row(aw), (bsz, seq, aw)
        else:
            per_span = window // tm
            spec = pl.BlockSpec((None, None, dil, tm // dil, aw),
                                lambda b, i, per_span=per_span: (b, i // per_span, 0, i % per_span, 0))
            shape = (bsz, seq // window, dil, ATTN_BLOCK, aw)
        qkv_specs += [spec] * 3
        qkv_shapes += [jax.ShapeDtypeStruct(shape, BF16)] * 3
    outs = pl.pallas_call(
        _inproj_kernel,
        grid=(bsz, seq // tm),
        in_specs=[row(d), pl.BlockSpec((1, d), lambda b, i: (0, 0)), per_b, per_b,
                  pl.BlockSpec((d, n), lambda b, i: (0, 0)),
                  row(LANES), row(LANES),
                  pl.BlockSpec(conv_w.shape, lambda b, i: (0, 0))],
        out_specs=qkv_specs + [row(cwid)],
        out_shape=qkv_shapes + [jax.ShapeDtypeStruct((bsz, seq, cwid), BF16)],
        scratch_shapes=[pltpu.VMEM((tm + 8, cwid), F32),
                        pltpu.VMEM((3, aw // LANES, tm, LANES), F32)],
        compiler_params=_params(2),
        name="even_inproj",
    )(x, g.reshape(1, d), shift.reshape(bsz, 1, d), scale.reshape(bsz, 1, d),
      w_in, cos, sin, conv_w)
    qkv = [outs[3 * p:3 * p + 3] for p in range(len(DILATED_PAIRS))]
    return qkv, outs[-1]


def _attn_item(q, kk, vv, valid):
    blk = ATTN_BLOCK
    lane = lax.broadcasted_iota(jnp.int32, (blk, LANES), 1)
    lane_kv = lax.broadcasted_iota(jnp.int32, (2 * blk, LANES), 1)
    neg_inf = jnp.float32(-jnp.inf)
    ml_new = jnp.zeros((blk, LANES), F32)
    n_col, per_col = ATTN_WIDTH // LANES, LANES // HEAD_DIM
    scores = []
    for j in range(n_col):
        sl = slice(j * LANES, (j + 1) * LANES)
        for hh in range(per_col):
            in_head = (lane // HEAD_DIM) == hh
            qm = jnp.where(in_head, q[:, sl], jnp.zeros_like(q[:, sl]))
            s = lax.dot_general(qm, kk[:, sl], (((1,), (1,)), ((), ())),
                                preferred_element_type=F32)
            scores.append(jnp.where(valid, s, neg_inf))
    accs = []
    for j in range(n_col):
        sl = slice(j * LANES, (j + 1) * LANES)
        vj = vv[:, sl]
        pv = []
        for hh in range(per_col):
            head = j * per_col + hh
            in_head_kv = (lane_kv // HEAD_DIM) == hh
            s = scores[head]
            m_new = jnp.max(s, axis=1, keepdims=True)
            p = jnp.exp(s - m_new)
            l_new = jnp.sum(p, axis=1, keepdims=True)
            vm = jnp.where(in_head_kv, vj, jnp.zeros_like(vj))
            pv.append(_dot(p.astype(BF16), vm))
            ml_new = jnp.where(lane == head, m_new, ml_new)
            ml_new = jnp.where(lane == ATTN_HEADS + head, l_new, ml_new)
        accs.append(pv[0] + pv[1])
    return accs, ml_new


def _head_columns(tile, first_lane, j):
    lane = lax.broadcasted_iota(jnp.int32, tile.shape, 1)
    h0 = first_lane + j * (LANES // HEAD_DIM)
    return jnp.where(lane < HEAD_DIM, tile[:, h0:h0 + 1], tile[:, h0 + 1:h0 + 2])


def _attn_merge(acc_old, ml_old, acc_loc, ml_loc):
    lane = lax.broadcasted_iota(jnp.int32, ml_old.shape, 1)
    m_new = jnp.maximum(ml_old, ml_loc)
    a_old = jnp.exp(ml_old - m_new)
    a_loc = jnp.exp(ml_loc - m_new)
    l_new = (pltpu.roll(a_old, ATTN_HEADS, 1) * ml_old
             + pltpu.roll(a_loc, ATTN_HEADS, 1) * ml_loc)
    ml_new = jnp.where(lane < ATTN_HEADS, m_new, jnp.where(lane < 2 * ATTN_HEADS, l_new, 0.0))
    accs = [_head_columns(a_old, 0, j) * acc_old[j] + _head_columns(a_loc, 0, j) * acc_loc[j]
            for j in range(len(acc_loc))]
    return accs, ml_new


def _band_mask(has_prev):
    blk = ATTN_BLOCK
    row = lax.broadcasted_iota(jnp.int32, (blk, 2 * blk), 0)
    col = lax.broadcasted_iota(jnp.int32, (blk, 2 * blk), 1)
    band = (col >= row) & (col <= row + blk)
    return band, band & ((col >= blk) | has_prev)


def _attn_wide_kernel(q_ref, kc_ref, vc_ref, kp_ref, vp_ref, acc_out, ml_out):
    _, valid = _band_mask(pl.program_id(1) > 0)
    for r in range(q_ref.shape[0]):
        kk = jnp.concatenate([kp_ref[r], kc_ref[r]], axis=0)
        vv = jnp.concatenate([vp_ref[r], vc_ref[r]], axis=0)
        accs, ml_new = _attn_item(q_ref[r], kk, vv, valid)
        for j, a in enumerate(accs):
            acc_out[r, :, j * LANES:(j + 1) * LANES] = a
        ml_out[r] = ml_new


def _attn_mid_kernel(q_ref, kc_ref, vc_ref, kp_ref, vp_ref, accw_ref, mlw_ref, acc_out, ml_out,
                     old_acc, old_ml, loc_acc, loc_ml):
    dil = q_ref.shape[0]
    ratio = accw_ref.shape[0] // dil
    sub = accw_ref.shape[1]
    n_col = ATTN_WIDTH // LANES
    blk = ATTN_BLOCK
    _, valid = _band_mask(pl.program_id(1) > 0)
    for r in range(dil):
        for qd in range(ratio):
            rows = pl.ds(r * blk + qd, sub, stride=ratio)
            for j in range(n_col):
                old_acc[j, rows, :] = accw_ref[dil * qd + r, :, j * LANES:(j + 1) * LANES]
            old_ml[rows, :] = mlw_ref[dil * qd + r]
        kk = jnp.concatenate([kp_ref[r], kc_ref[r]], axis=0)
        vv = jnp.concatenate([vp_ref[r], vc_ref[r]], axis=0)
        accs, ml = _attn_item(q_ref[r], kk, vv, valid)
        for j, a in enumerate(accs):
            loc_acc[j, r * blk:(r + 1) * blk, :] = a
        loc_ml[r * blk:(r + 1) * blk, :] = ml
    accs, ml_new = _attn_merge([old_acc[j] for j in range(n_col)], old_ml[...],
                               [loc_acc[j] for j in range(n_col)], loc_ml[...])
    for r in range(dil):
        rows = pl.ds(r, blk, stride=dil)
        for j, a in enumerate(accs):
            acc_out[j, rows, :] = a[r * blk:(r + 1) * blk]
        ml_out[rows, :] = ml_new[r * blk:(r + 1) * blk]


def _attn_last_kernel(q_ref, kc_ref, vc_ref, kp_ref, vp_ref, acc_in, ml_in, a_out,
                      loc_acc, loc_ml):
    gb = q_ref.shape[0]
    n_col = ATTN_WIDTH // LANES
    blk = ATTN_BLOCK
    band, valid0 = _band_mask(pl.program_id(1) > 0)
    for i in range(gb):
        if i == 0:
            k_prev, v_prev, valid = kp_ref[...], vp_ref[...], valid0
        else:
            k_prev, v_prev, valid = kc_ref[i - 1], vc_ref[i - 1], band
        kk = jnp.concatenate([k_prev, kc_ref[i]], axis=0)
        vv = jnp.concatenate([v_prev, vc_ref[i]], axis=0)
        accs, ml = _attn_item(q_ref[i], kk, vv, valid)
        for j, a in enumerate(accs):
            loc_acc[j, i * blk:(i + 1) * blk, :] = a
        loc_ml[i * blk:(i + 1) * blk, :] = ml
    ml_old = ml_in[...].reshape(gb * blk, LANES)
    accs, ml_new = _attn_merge([acc_in[j] for j in range(n_col)], ml_old,
                               [loc_acc[j] for j in range(n_col)], loc_ml[...])
    lane = lax.broadcasted_iota(jnp.int32, ml_new.shape, 1)
    is_l = (lane >= ATTN_HEADS) & (lane < 2 * ATTN_HEADS)
    linv = 1.0 / jnp.where(is_l, ml_new, 1.0)
    for j, a in enumerate(accs):
        out = (a * _head_columns(linv, ATTN_HEADS, j)).astype(BF16)
        a_out[:, :, j * LANES:(j + 1) * LANES] = out.reshape(gb, blk, LANES)


def _dilated_attention(qkv):
    (_, d1), (_, dm), (_, dw) = DILATED_PAIRS
    assert d1 == 1 and dw % dm == 0
    blk, aw = ATTN_BLOCK, ATTN_WIDTH
    (q1, k1, v1), (qm, km, vm), (qw, kw, vw) = qkv
    bsz, seq, _ = q1.shape

    nw = qw.shape[1]
    rpg = min(ATTN_GROUP, dw)
    cur = lambda w: pl.BlockSpec((None, None, rpg, blk, w), lambda b, n, g: (b, n, g, 0, 0))
    prev = pl.BlockSpec((None, None, rpg, blk, aw),
                        lambda b, n, g: (b, jnp.maximum(n - 1, 0), g, 0, 0))
    acc_w, ml_w = pl.pallas_call(
        _attn_wide_kernel,
        grid=(bsz, nw, dw // rpg),
        in_specs=[cur(aw), cur(aw), cur(aw), prev, prev],
        out_specs=[cur(aw), cur(LANES)],
        out_shape=[jax.ShapeDtypeStruct((bsz, nw, dw, blk, aw), F32),
                   jax.ShapeDtypeStruct((bsz, nw, dw, blk, LANES), F32)],
        compiler_params=_params(3),
        name=f"attn_dil{dw}",
    )(qw, kw, vw, kw, vw)

    nm = qm.shape[1]
    ratio = dw // dm
    sub = blk // ratio
    cur = pl.BlockSpec((None, None, dm, blk, aw), lambda b, n: (b, n, 0, 0, 0))
    prev = pl.BlockSpec((None, None, dm, blk, aw),
                        lambda b, n: (b, jnp.maximum(n - 1, 0), 0, 0, 0))
    wide = lambda w: pl.BlockSpec((None, None, dw, sub, w),
                                  lambda b, n: (b, n // ratio, 0, n % ratio, 0))
    n_col = aw // LANES
    acc, ml = pl.pallas_call(
        _attn_mid_kernel,
        grid=(bsz, nm),
        in_specs=[cur, cur, cur, prev, prev, wide(aw), wide(LANES)],
        out_specs=[pl.BlockSpec((None, n_col, dm * blk, LANES), lambda b, n: (b, 0, n, 0)),
                   pl.BlockSpec((None, dm * blk, LANES), lambda b, n: (b, n, 0))],
        out_shape=[jax.ShapeDtypeStruct((bsz, n_col, seq, LANES), F32),
                   jax.ShapeDtypeStruct((bsz, seq, LANES), F32)],
        scratch_shapes=[pltpu.VMEM((n_col, dm * blk, LANES), F32), pltpu.VMEM((dm * blk, LANES), F32),
                        pltpu.VMEM((n_col, dm * blk, LANES), F32), pltpu.VMEM((dm * blk, LANES), F32)],
        compiler_params=_params(2),
        name=f"attn_dil{dm}",
    )(qm, km, vm, km, vm, acc_w, ml_w)

    nb = seq // blk
    gb = min(ATTN_GROUP, nb)
    view = lambda t: t.reshape(bsz, nb, blk, t.shape[-1])
    cur = lambda w: pl.BlockSpec((None, gb, blk, w), lambda b, n: (b, n, 0, 0))
    prev = pl.BlockSpec((None, None, blk, aw), lambda b, n: (b, jnp.maximum(n * gb - 1, 0), 0, 0))
    a_out = pl.pallas_call(
        _attn_last_kernel,
        grid=(bsz, nb // gb),
        in_specs=[cur(aw), cur(aw), cur(aw), prev, prev,
                  pl.BlockSpec((None, n_col, gb * blk, LANES), lambda b, n: (b, 0, n, 0)),
                  cur(LANES)],
        out_specs=cur(aw),
        out_shape=jax.ShapeDtypeStruct((bsz, nb, blk, aw), BF16),
        scratch_shapes=[pltpu.VMEM((n_col, gb * blk, LANES), F32),
                        pltpu.VMEM((gb * blk, LANES), F32)],
        compiler_params=_params(2),
        name=f"attn_dil{d1}",
    )(view(q1), view(k1), view(v1), view(k1), view(v1), acc, view(ml))
    return a_out.reshape(bsz, seq, aw)


def _outproj_kernel(a_ref, b_ref, x_ref, wa_ref, wb_ref, gm_ref, g_ref, sh_ref, sc_ref,
                    x1_ref, h_ref):
    mix = _dot(a_ref[...], wa_ref[...]) + _dot(b_ref[...], wb_ref[...])
    x1 = x_ref[...] + gm_ref[...] * mix
    x1_ref[...] = x1
    h_ref[...] = _ada_norm(x1, g_ref[...], sh_ref[...], sc_ref[...]).astype(BF16)


def _even_outproj(a, b, x, w_out, gate_m, g, shift, scale):
    bsz, seq, d = x.shape
    aw, bw = a.shape[2], b.shape[2]
    tm = ROW_TILE
    row = lambda w: pl.BlockSpec((None, tm, w), lambda bi, i: (bi, i, 0))
    per_b = pl.BlockSpec((None, 1, d), lambda bi, i: (bi, 0, 0))
    full = lambda s: pl.BlockSpec(s, lambda bi, i: (0, 0))
    pb = lambda t: t.reshape(bsz, 1, d)
    return pl.pallas_call(
        _outproj_kernel,
        grid=(bsz, seq // tm),
        in_specs=[row(aw), row(bw), row(d), full((aw, d)), full((bw, d)),
                  per_b, full((1, d)), per_b, per_b],
        out_specs=[row(d), row(d)],
        out_shape=[jax.ShapeDtypeStruct((bsz, seq, d), F32),
                   jax.ShapeDtypeStruct((bsz, seq, d), BF16)],
        compiler_params=_params(2),
        name="even_outproj",
    )(a, b, x, w_out[:aw], w_out[aw:], pb(gate_m), g.reshape(1, d), pb(shift), pb(scale))


def _ffn_kernel(h_ref, x_ref, wg_ref, wu_ref, wd_ref, gf_ref, g_ref, sh_ref, sc_ref,
                o_ref, hn_ref):
    h = h_ref[...]
    g = _dot(h, wg_ref[...])
    u = _dot(h, wu_ref[...])
    act = (g * _sigmoid(g) * u).astype(BF16)
    x2 = x_ref[...] + gf_ref[...] * _dot(act, wd_ref[...])
    o_ref[...] = x2
    hn_ref[...] = _ada_norm(x2, g_ref[...], sh_ref[...], sc_ref[...]).astype(BF16)


def _dense_ffn(h, x, w_gate, w_up, w_down, gate_f, next_g, next_shift, next_scale):
    bsz, seq, d = x.shape
    ff = w_gate.shape[1]
    tm = ROW_TILE
    row = pl.BlockSpec((None, tm, d), lambda b, i: (b, i, 0))
    per_b = pl.BlockSpec((None, 1, d), lambda b, i: (b, 0, 0))
    full = lambda s: pl.BlockSpec(s, lambda b, i: (0, 0), pipeline_mode=pl.Buffered(1))
    pb = lambda t: t.reshape(bsz, 1, d)
    return pl.pallas_call(
        _ffn_kernel,
        grid=(bsz, seq // tm),
        in_specs=[row, row, full((d, ff)), full((d, ff)), full((ff, d)), per_b,
                  pl.BlockSpec((1, d), lambda b, i: (0, 0)), per_b, per_b],
        out_specs=[row, row],
        out_shape=[jax.ShapeDtypeStruct((bsz, seq, d), F32),
                   jax.ShapeDtypeStruct((bsz, seq, d), BF16)],
        compiler_params=_params(2),
        name="dense_ffn",
    )(h, x, w_gate, w_up, w_down, pb(gate_f), next_g.reshape(1, d), pb(next_shift),
      pb(next_scale))


def _ssm_operators(a_re, a_im, log_step, b_re, b_im, c_re, c_im):
    L = SSM_CHUNK
    a_re, a_im = a_re.astype(F32), a_im.astype(F32)
    dt = jnp.exp(log_step.astype(F32))[:, None]
    mag = jnp.exp(a_re * dt)
    abar_re = mag * jnp.cos(a_im * dt)
    abar_im = mag * jnp.sin(a_im * dt)
    den = a_re * a_re + a_im * a_im
    nr = abar_re - 1.0
    f_re = (nr * a_re + abar_im * a_im) / den
    f_im = (abar_im * a_re - nr * a_im) / den
    b_re, b_im = b_re.astype(F32), b_im.astype(F32)
    bb_re = f_re[..., None] * b_re - f_im[..., None] * b_im
    bb_im = f_re[..., None] * b_im + f_im[..., None] * b_re
    c_re, c_im = c_re.astype(F32), c_im.astype(F32)

    def step(carry, _):
        pr, pi = carry
        nxt = (pr * abar_re - pi * abar_im, pr * abar_im + pi * abar_re)
        return nxt, carry
    (pl_re, pl_im), (pw_re, pw_im) = lax.scan(
        step, (jnp.ones_like(abar_re), jnp.zeros_like(abar_re)), None, length=L)
    pw_re = jnp.concatenate([pw_re, pl_re[None]], axis=0)
    pw_im = jnp.concatenate([pw_im, pl_im[None]], axis=0)

    hi = lax.Precision.HIGHEST
    cp_re = c_re[None] * pw_re[:L, :, None, :] - c_im[None] * pw_im[:L, :, None, :]
    cp_im = c_re[None] * pw_im[:L, :, None, :] + c_im[None] * pw_re[:L, :, None, :]
    w = (jnp.einsum('tgcp,gpd->tgdc', cp_re, bb_re, precision=hi)
         - jnp.einsum('tgcp,gpd->tgdc', cp_im, bb_im, precision=hi))
    s_idx = jnp.arange(L)[:, None]
    t_idx = jnp.arange(L)[None, :]
    tau = t_idx - s_idx
    toep = jnp.where((tau >= 0)[:, :, None, None, None], w[jnp.clip(tau, 0, L - 1)], 0.0)
    n_g, n_c = a_re.shape[0], b_re.shape[2]
    toep = toep.transpose(2, 0, 3, 1, 4).reshape(n_g, L * n_c, L * n_c)

    rp_re, rp_im = pw_re[:L][::-1], pw_im[:L][::-1]
    so_re = rp_re[..., None] * bb_re[None] - rp_im[..., None] * bb_im[None]
    so_im = rp_re[..., None] * bb_im[None] + rp_im[..., None] * bb_re[None]
    to_rows = lambda t: t.transpose(1, 0, 3, 2).reshape(n_g, L * n_c, -1)
    s_out = jnp.concatenate([to_rows(so_re), to_rows(so_im)], axis=-1)
    s_out_sw = jnp.concatenate([to_rows(so_im), to_rows(so_re)], axis=-1)

    qp_re, qp_im = pw_re[1:], pw_im[1:]
    ci_re = c_re[None] * qp_re[:, :, None, :] - c_im[None] * qp_im[:, :, None, :]
    ci_im = c_re[None] * qp_im[:, :, None, :] + c_im[None] * qp_re[:, :, None, :]
    to_cols = lambda t: t.transpose(1, 3, 0, 2).reshape(n_g, -1, L * n_c)
    c_in = jnp.concatenate([to_cols(ci_re), -to_cols(ci_im)], axis=1)

    al_re, al_im = pw_re[L], pw_im[L]
    a1 = jnp.concatenate([al_re, al_re], axis=-1)[:, None, :]
    a2 = jnp.concatenate([-al_im, al_im], axis=-1)[:, None, :]
    return (toep.astype(BF16), s_out.astype(BF16), s_out_sw.astype(BF16),
            c_in.astype(BF16), a1, a2)


def _piece_transpose(arrs):
    n = len(arrs)
    piece = lax.broadcasted_iota(jnp.int32, (1, LANES), 1) // SSM_GROUP
    arrs = list(arrs)
    dist = n // 2
    while dist >= 1:
        keep = (piece & dist) == 0
        for i in range(n):
            if i & dist == 0:
                a, b = arrs[i], arrs[i + dist]
                arrs[i] = jnp.where(keep, a, pltpu.roll(b, dist * SSM_GROUP, 1))
                arrs[i + dist] = jnp.where(keep, pltpu.roll(a, LANES - dist * SSM_GROUP, 1), b)
        dist //= 2
    return arrs


def _ssm_kernel(u_ref, t_ref, s_ref, ssw_ref, cin_ref, a1_ref, a2_ref, y_ref,
                st_ref, stsw_ref, xs_ref, xc_ref, zc_ref):
    kt, L, bsz, _ = u_ref.shape
    rows = kt * bsz
    n_lg = SSM_LANE_GROUPS

    @pl.when(pl.program_id(1) == 0)
    def _():
        xc_ref[...] = jnp.zeros(xc_ref.shape, F32)
        zc_ref[...] = jnp.zeros(zc_ref.shape, F32)

    zs = [u_ref[:, s].reshape(rows, LANES) for s in range(L)]
    lo = _piece_transpose(zs[:n_lg])
    hi = _piece_transpose(zs[n_lg:])
    ys = []
    for gl in range(n_lg):
        v = jnp.concatenate([lo[gl], hi[gl]], axis=1)
        st_ref[...] = _dot(v, s_ref[gl])
        stsw_ref[...] = _dot(v, ssw_ref[gl])
        a1 = jnp.broadcast_to(a1_ref[gl], (bsz, 2 * SSM_STATE))
        a2 = jnp.broadcast_to(a2_ref[gl], (bsz, 2 * SSM_STATE))

        def body(k, carry):
            x, z = carry
            r = pl.ds(pl.multiple_of(k * bsz, bsz), bsz)
            xs_ref[r, :] = x
            x_new = x * a1 + z * a2 + st_ref[r, :]
            z_new = z * a1 - x * a2 + stsw_ref[r, :]
            return x_new, z_new

        x, z = lax.fori_loop(0, kt, body, (xc_ref[gl], zc_ref[gl]), unroll=8)
        xc_ref[gl] = x
        zc_ref[gl] = z
        ys.append((_dot(v, t_ref[gl]) + _dot(xs_ref[...].astype(BF16), cin_ref[gl])).astype(BF16))
    out_lo = _piece_transpose([y[:, :LANES] for y in ys])
    out_hi = _piece_transpose([y[:, LANES:] for y in ys])
    for t in range(n_lg):
        y_ref[:, t] = out_lo[t].reshape(kt, bsz, LANES)
        y_ref[:, n_lg + t] = out_hi[t].reshape(kt, bsz, LANES)


def _ssm_scan(h, ops):
    bsz, seq, d = h.shape
    toep, s_out, s_out_sw, c_in, a1, a2 = ops
    L, C, n_lg = SSM_CHUNK, SSM_GROUP, SSM_LANE_GROUPS
    assert L == 2 * n_lg
    n_k = seq // L
    kt = min(SSM_CHUNKS_PER_STEP, n_k)
    hv = h.transpose(1, 0, 2).reshape(n_k, L, bsz, d)
    act = pl.BlockSpec((kt, L, bsz, LANES), lambda o, k: (k, 0, 0, o))
    wblk = lambda s: pl.BlockSpec((n_lg,) + s, lambda o, k: (o, 0, 0))
    y = pl.pallas_call(
        _ssm_kernel,
        grid=(d // LANES, n_k // kt),
        in_specs=[act, wblk((L * C, L * C)), wblk((L * C, 2 * SSM_STATE)),
                  wblk((L * C, 2 * SSM_STATE)), wblk((2 * SSM_STATE, L * C)),
                  wblk((1, 2 * SSM_STATE)), wblk((1, 2 * SSM_STATE))],
        out_specs=act,
        out_shape=jax.ShapeDtypeStruct((n_k, L, bsz, d), BF16),
        scratch_shapes=[pltpu.VMEM((kt * bsz, 2 * SSM_STATE), F32)] * 3
        + [pltpu.VMEM((n_lg, bsz, 2 * SSM_STATE), F32)] * 2,
        compiler_params=_params(2),
        name="ssm_scan",
    )(hv, toep, s_out, s_out_sw, c_in, a1, a2)
    return y.reshape(seq, bsz, d).transpose(1, 0, 2)


def _gelu_tanh(x):
    return 0.5 * x * (1.0 + jnp.tanh(math.sqrt(2.0 / math.pi) * (x + 0.044715 * (x * x * x))))


def _ssm_out_kernel(y_ref, h_ref, x_ref, d_ref, w_ref, b_ref, gm_ref, g_ref, sh_ref, sc_ref,
                    rw_ref, rb_ref, x1_ref, h2_ref, route_ref):
    d = x_ref.shape[1]
    y = y_ref[...].astype(F32) + d_ref[...] * h_ref[...].astype(F32)
    z = _dot(_gelu_tanh(y).astype(BF16), w_ref[...]) + b_ref[...]
    mix = z[:, :d] * _sigmoid(z[:, d:])
    x1 = x_ref[...] + gm_ref[...] * mix
    x1_ref[...] = x1
    h2 = _ada_norm(x1, g_ref[...], sh_ref[...], sc_ref[...])
    h2_ref[...] = h2

    hh, hl = _split_bf16(h2)
    both = _dot(hh, rw_ref[...])
    logits = (both[:, :LANES] + both[:, LANES:] + _dot(hl, rw_ref[:, :LANES]) + rb_ref[...])
    lane = lax.broadcasted_iota(jnp.int32, logits.shape, 1)
    lane_f = lane.astype(F32)
    neg_inf = jnp.float32(-jnp.inf)
    logits = jnp.where(lane < N_EXPERTS, logits, neg_inf)
    m1 = jnp.max(logits, axis=1, keepdims=True)
    i1 = jnp.min(jnp.where(logits == m1, lane_f, float(LANES)), axis=1, keepdims=True)
    rest = jnp.where(lane_f == i1, neg_inf, logits)
    m2 = jnp.max(rest, axis=1, keepdims=True)
    i2 = jnp.min(jnp.where(rest == m2, lane_f, float(LANES)), axis=1, keepdims=True)
    e2 = jnp.exp(m2 - m1)
    g1 = 1.0 / (1.0 + e2)
    g2 = e2 / (1.0 + e2)
    route = jnp.where(lane == 0, i1, jnp.where(lane == 1, i2,
                      jnp.where(lane == 2, g1, jnp.where(lane == 3, g2, 0.0))))
    route_ref[...] = route


def _ssm_out(y, h, x, d_skip, glu_w, glu_b, gate_m, g, shift, scale, router_w, router_b):
    bsz, seq, d = x.shape
    tm = ROW_TILE
    row = lambda w: pl.BlockSpec((None, tm, w), lambda b, i: (b, i, 0))
    per_b = pl.BlockSpec((None, 1, d), lambda b, i: (b, 0, 0))
    full = lambda s: pl.BlockSpec(s, lambda b, i: (0, 0))
    pb = lambda t: t.reshape(bsz, 1, d)
    rw = jnp.zeros((d, LANES), F32).at[:, :N_EXPERTS].set(router_w)
    rwh = rw.astype(BF16)
    rw_cat = jnp.concatenate([rwh, (rw - rwh.astype(F32)).astype(BF16)], axis=1)
    rb = jnp.zeros((1, LANES), F32).at[0, :N_EXPERTS].set(router_b)
    return pl.pallas_call(
        _ssm_out_kernel,
        grid=(bsz, seq // tm),
        in_specs=[row(d), row(d), row(d), full((1, d)), full((d, 2 * d)), full((1, 2 * d)),
                  per_b, full((1, d)), per_b, per_b,
                  full((d, 2 * LANES)), full((1, LANES))],
        out_specs=[row(d), row(d), row(LANES)],
        out_shape=[jax.ShapeDtypeStruct((bsz, seq, d), F32),
                   jax.ShapeDtypeStruct((bsz, seq, d), F32),
                   jax.ShapeDtypeStruct((bsz, seq, LANES), F32)],
        compiler_params=_params(2),
        name="ssm_out",
    )(y, h, x, d_skip.reshape(1, d), glu_w, glu_b.reshape(1, 2 * d), pb(gate_m),
      g.reshape(1, d), pb(shift), pb(scale), rw_cat, rb)


def _routing_tables(route, tm):
    n = route.shape[0]
    i32 = jnp.int32
    e_flat = route[:, :2].astype(i32).T.reshape(-1)
    onehot = (e_flat[:, None] == jnp.arange(N_EXPERTS)[None, :]).astype(i32)
    incl = jnp.cumsum(onehot, axis=0)
    counts = incl[-1]
    rank = jnp.sum((incl - onehot) * onehot, axis=1)
    padded = ((counts + tm - 1) // tm) * tm
    ends = jnp.cumsum(padded)
    starts = ends - padded
    dest = starts[e_flat] + rank
    n_tiles = (2 * n) // tm + N_EXPERTS
    n_rows = n_tiles * tm
    s = jnp.arange(n_rows, dtype=i32)
    e_of_s = jnp.sum((s[:, None] >= ends[None, :]).astype(i32), axis=1)
    starts9 = jnp.concatenate([starts, ends[-1:]])
    counts9 = jnp.concatenate([counts, jnp.zeros((1,), i32)])
    ustart9 = jnp.concatenate([jnp.cumsum(counts) - counts, jnp.full((1,), 2 * n, i32)])
    real_before = ustart9[e_of_s] + jnp.minimum(s - starts9[e_of_s], counts9[e_of_s])
    pair = (2 * n + s - real_before).at[dest].set(
        jnp.arange(2 * n, dtype=i32), unique_indices=True, mode="promise_in_bounds")
    src = jnp.concatenate([jnp.where(pair < 2 * n, pair % n, 0), jnp.zeros((tm,), i32)])
    dst = jnp.concatenate([n_rows + jnp.arange(tm, dtype=i32), pair])
    tile_expert = jnp.minimum(e_of_s[::tm], N_EXPERTS - 1)
    return src, dst, tile_expert


def _moe_kernel(te_ref, src_ref, src_next_ref, dst_prev_ref, dst_ref, h_hbm, wg_ref, wu_ref, wd_ref,
                y_hbm, hbuf, ybuf, gsem, ssem):
    t = pl.program_id(0)
    n_t = pl.num_programs(0)
    tm = hbuf.shape[1]
    ff = wg_ref.shape[1]
    fc = ff // MOE_FF_CHUNKS
    slot = t % 2
    other = 1 - slot

    def gather(slot_, r, row):
        return pltpu.make_async_copy(h_hbm.at[pl.ds(row, 1)], hbuf.at[slot_, pl.ds(r, 1)],
                                     gsem.at[slot_])

    def scatter(slot_, r, row):
        return pltpu.make_async_copy(ybuf.at[slot_, pl.ds(r, 1)], y_hbm.at[pl.ds(row, 1)],
                                     ssem.at[slot_])

    def for_rows(fn):
        def body(r, c):
            fn(r)
            return c
        lax.fori_loop(0, tm, body, 0, unroll=8)

    @pl.when(t == 0)
    def _():
        ybuf[...] = jnp.zeros(ybuf.shape, F32)
        for_rows(lambda r: gather(0, r, src_ref[r]).start())

    for_rows(lambda r: gather(slot, r, 0).wait())

    @pl.when(t > 0)
    def _():
        for_rows(lambda r: scatter(slot, r, 0).wait())

    h = hbuf[slot].astype(BF16)
    for r in range(tm):
        gather(other, r, src_next_ref[r]).start()
    for r in range(tm):
        scatter(other, r, dst_prev_ref[r]).start()

    acc = jnp.zeros((tm, wd_ref.shape[1]), F32)
    for c in range(MOE_FF_CHUNKS):
        g = _dot(h, wg_ref[:, c * fc:(c + 1) * fc])
        u = _dot(h, wu_ref[:, c * fc:(c + 1) * fc])
        act = (g * _sigmoid(g) * u).astype(BF16)
        acc = acc + _dot(act, wd_ref[c * fc:(c + 1) * fc, :])
    ybuf[slot] = acc

    @pl.when(t == n_t - 1)
    def _():
        for_rows(lambda r: gather(other, r, 0).wait())
        for_rows(lambda r: scatter(other, r, 0).wait())
        for_rows(lambda r: scatter(slot, r, dst_ref[r]).start())
        for_rows(lambda r: scatter(slot, r, 0).wait())


def _moe_experts(h2, src, dst, tile_expert, w_gate, w_up, w_down):
    n, d = h2.shape
    ff = w_gate.shape[2]
    tm = MOE_TILE
    n_tiles = tile_expert.shape[0]
    wspec = lambda s: pl.BlockSpec((None,) + s, lambda t, te: (te[t], 0, 0),
                                   pipeline_mode=pl.Buffered(1))
    idx = lambda off: pl.BlockSpec((tm,), lambda t, te: (t + off,), memory_space=pltpu.SMEM)
    grid_spec = pltpu.PrefetchScalarGridSpec(
        num_scalar_prefetch=1,
        grid=(n_tiles,),
        in_specs=[idx(0), idx(1), idx(0), idx(1),
                  pl.BlockSpec(memory_space=pl.ANY),
                  wspec((d, ff)), wspec((d, ff)), wspec((ff, d))],
        out_specs=pl.BlockSpec(memory_space=pl.ANY),
        scratch_shapes=[pltpu.VMEM((2, tm, d), F32), pltpu.VMEM((2, tm, d), F32),
                        pltpu.SemaphoreType.DMA((2,)), pltpu.SemaphoreType.DMA((2,))],
    )
    return pl.pallas_call(
        _moe_kernel,
        grid_spec=grid_spec,
        out_shape=jax.ShapeDtypeStruct((n_tiles * tm + tm, d), F32),
        compiler_params=_params(1),
        name="moe_experts",
    )(tile_expert, src, src, dst, dst, h2, w_gate, w_up, w_down)


def _combine_kernel(y1_ref, y2_ref, route_ref, x_ref, gf_ref, fg_ref, o_ref, *, final):
    route = route_ref[...]
    ff = route[:, 2:3] * y1_ref[...] + route[:, 3:4] * y2_ref[...]
    x2 = x_ref[...] + gf_ref[...] * ff
    if final:
        ms = jnp.mean(x2 * x2, axis=-1, keepdims=True)
        x2 = x2 * lax.rsqrt(ms + RMS_EPS) * fg_ref[...]
    o_ref[...] = x2


def _moe_combine(y, route, x1, gate_f, final_g, final):
    bsz, seq, d = x1.shape
    n = bsz * seq
    tm = ROW_TILE
    per_batch = seq // tm
    out = pl.pallas_call(
        functools.partial(_combine_kernel, final=final),
        grid=(n // tm,),
        in_specs=[pl.BlockSpec((tm, d), lambda i: (i, 0)),
                  pl.BlockSpec((tm, d), lambda i: (n // tm + i, 0)),
                  pl.BlockSpec((tm, LANES), lambda i: (i, 0)),
                  pl.BlockSpec((tm, d), lambda i: (i, 0)),
                  pl.BlockSpec((None, 1, d), lambda i: (i // per_batch, 0, 0)),
                  pl.BlockSpec((1, d), lambda i: (0, 0))],
        out_specs=pl.BlockSpec((tm, d), lambda i: (i, 0)),
        out_shape=jax.ShapeDtypeStruct((n, d), F32),
        compiler_params=_params(1),
        name="moe_combine",
    )(y, y, route.reshape(n, LANES), x1.reshape(n, d),
      gate_f.reshape(bsz, 1, d), final_g.reshape(1, d))
    return out.reshape(bsz, seq, d)


def _even_layer(x, mods, norm_mix_g, norm_ffn_g, cos, sin, w_in, conv_w, w_out,
                w_gate, w_up, w_down, next_norm):
    sh_m, sc_m, g_m, sh_f, sc_f, g_f = mods
    qkv, b_out = _even_inproj(x, norm_mix_g, sh_m, sc_m, w_in.astype(BF16), cos, sin, conv_w)
    a_out = _dilated_attention(qkv)
    x1, h2 = _even_outproj(a_out, b_out, x, w_out.astype(BF16), g_m, norm_ffn_g, sh_f, sc_f)
    return _dense_ffn(h2, x1, w_gate.astype(BF16), w_up.astype(BF16), w_down.astype(BF16), g_f,
                      *next_norm)


def _odd_layer(x, h, mods, norm_ffn_g, ssm, d_skip, glu_w, glu_b,
               router_w, router_b, w_gate, w_up, w_down, final_g, final):
    _, _, g_m, sh_f, sc_f, g_f = mods
    bsz, seq, d = x.shape
    y = _ssm_scan(h, _ssm_operators(*ssm))
    x1, h2, route = _ssm_out(y, h, x, d_skip, glu_w.astype(BF16), glu_b, g_m,
                             norm_ffn_g, sh_f, sc_f, router_w, router_b)
    route = route.reshape(bsz * seq, LANES)
    src, dst, tile_expert = _routing_tables(route, MOE_TILE)
    y_pairs = _moe_experts(h2.reshape(bsz * seq, d), src, dst, tile_expert,
                           w_gate.astype(BF16), w_up.astype(BF16), w_down.astype(BF16))
    return _moe_combine(y_pairs, route, x1, g_f, final_g, final)


def kernel(x, c, positions, mod_w, mod_b, norm_mix_g, norm_ffn_g, ev_w_in, ev_conv_w, ev_w_out, ffn_w_gate, ffn_w_up, ffn_w_down, ssm_a_re, ssm_a_im, ssm_log_step, ssm_b_re, ssm_b_im, ssm_c_re, ssm_c_im, ssm_d, glu_w, glu_b, moe_router_w, moe_router_b, moe_w_gate, moe_w_up, moe_w_down, final_norm_g):
    depth = mod_w.shape[0]
    d = x.shape[2]
    assert depth % 2 == 0
    mod = _modulation(c, mod_w, mod_b)
    cos, sin = _rope_tables(positions)
    layer_mods = [[mod[layer, :, j * d:(j + 1) * d] for j in range(6)] for layer in range(depth)]
    h = None
    for layer in range(depth):
        mods = layer_mods[layer]
        i = layer // 2
        if layer % 2 == 0:
            nxt = layer_mods[layer + 1]
            x, h = _even_layer(x, mods, norm_mix_g[layer], norm_ffn_g[layer], cos, sin,
                               ev_w_in[i], ev_conv_w[i], ev_w_out[i],
                               ffn_w_gate[i], ffn_w_up[i], ffn_w_down[i],
                               (norm_mix_g[layer + 1], nxt[0], nxt[1]))
        else:
            ssm = (ssm_a_re[i], ssm_a_im[i], ssm_log_step[i], ssm_b_re[i], ssm_b_im[i],
                   ssm_c_re[i], ssm_c_im[i])
            x = _odd_layer(x, h, mods, norm_ffn_g[layer], ssm, ssm_d[i],
                           glu_w[i], glu_b[i], moe_router_w[i], moe_router_b[i],
                           moe_w_gate[i], moe_w_up[i], moe_w_down[i],
                           final_norm_g, layer == depth - 1)
    return x
```

```python
import functools
import math

import jax
import jax.numpy as jnp
from jax import lax
from jax.experimental import pallas as pl
from jax.experimental.pallas import tpu as pltpu

F32 = jnp.float32
BF16 = jnp.bfloat16

ATTN_HEADS = 8
HEAD_DIM = 64
ATTN_WIDTH = ATTN_HEADS * HEAD_DIM
ROPE_DIM = HEAD_DIM // 4
ROPE_THETA = 500000.0
DILATED_PAIRS = ((128, 1), (512, 4), (2048, 16))
ATTN_BLOCK = 128
SSM_GROUP = 16
SSM_STATE = 64
SSM_CHUNK = 16
N_EXPERTS = 8
RMS_EPS = 1e-6

LANES = 128
VMEM_LIMIT_BYTES = 56 * 1024 * 1024

ROW_TILE = 512
MOE_TILE = 512
MOE_FF_CHUNKS = 2
ATTN_GROUP = 4
SSM_CHUNKS_PER_STEP = 64
SSM_LANE_GROUPS = LANES // SSM_GROUP


def _params(n_axes, vmem=VMEM_LIMIT_BYTES):
    return pltpu.CompilerParams(
        dimension_semantics=("arbitrary",) * n_axes, vmem_limit_bytes=vmem)


def _dot(a, b):
    return jnp.dot(a, b, preferred_element_type=F32)


def _sigmoid(x):
    return 1.0 / (1.0 + jnp.exp(-x))


def _split_bf16(x):
    hi = x.astype(BF16)
    lo = (x - hi.astype(F32)).astype(BF16)
    return hi, lo


def _ada_norm(x, g, shift, scale):
    ms = jnp.mean(x * x, axis=-1, keepdims=True)
    return x * lax.rsqrt(ms + RMS_EPS) * g * (1.0 + scale) + shift


def _mod_kernel(c_ref, w_ref, b_ref, o_ref):
    c = c_ref[...]
    cond = c * _sigmoid(c)
    ch, cl = _split_bf16(cond)
    wh, wl = _split_bf16(w_ref[...])
    o_ref[...] = _dot(ch, wh) + _dot(cl, wh) + _dot(ch, wl) + b_ref[...]


def _modulation(c, mod_w, mod_b):
    depth, d, n = mod_w.shape
    bsz = c.shape[0]
    tn = 1024
    return pl.pallas_call(
        _mod_kernel,
        grid=(depth, n // tn),
        in_specs=[pl.BlockSpec((bsz, d), lambda l, j: (0, 0)),
                  pl.BlockSpec((None, d, tn), lambda l, j: (l, 0, j)),
                  pl.BlockSpec((None, 1, tn), lambda l, j: (l, 0, j))],
        out_specs=pl.BlockSpec((None, bsz, tn), lambda l, j: (l, 0, j)),
        out_shape=jax.ShapeDtypeStruct((depth, bsz, n), F32),
        compiler_params=_params(2),
        name="modulation",
    )(c, mod_w, mod_b.reshape(depth, 1, n))


def _rope_kernel(pos_ref, inv_ref, cos_ref, sin_ref):
    ang = pos_ref[...].astype(F32) * inv_ref[...]
    cos_ref[...] = jnp.cos(ang)
    sin_ref[...] = jnp.sin(ang)


def _rope_tables(positions):
    bsz, seq = positions.shape
    inv = ROPE_THETA ** (-jnp.arange(0, ROPE_DIM, 2, dtype=F32) / ROPE_DIM)
    lane = jnp.arange(LANES) % HEAD_DIM
    inv_lane = jnp.where(lane < ROPE_DIM, inv[lane % (ROPE_DIM // 2)], 0.0).reshape(1, LANES)
    tm = ROW_TILE
    spec = pl.BlockSpec((None, tm, LANES), lambda b, i: (b, i, 0))
    return pl.pallas_call(
        _rope_kernel,
        grid=(bsz, seq // tm),
        in_specs=[pl.BlockSpec((None, tm, 1), lambda b, i: (b, i, 0)),
                  pl.BlockSpec((1, LANES), lambda b, i: (0, 0))],
        out_specs=[spec, spec],
        out_shape=[jax.ShapeDtypeStruct((bsz, seq, LANES), F32)] * 2,
        compiler_params=_params(2),
        name="rope_tables",
    )(positions.reshape(bsz, seq, 1), inv_lane)


def _inproj_kernel(x_ref, g_ref, sh_ref, sc_ref, w_ref, cos_ref, sin_ref, cw_ref, *rest):
    n_pat = len(DILATED_PAIRS)
    qkv_refs = [rest[3 * p:3 * p + 3] for p in range(n_pat)]
    bo_ref, ubuf, stage = rest[3 * n_pat:]
    tm = x_ref.shape[0]
    aw = ATTN_WIDTH
    cwid = cw_ref.shape[1]
    h = _ada_norm(x_ref[...], g_ref[...], sh_ref[...], sc_ref[...]).astype(BF16)
    proj = _dot(h, w_ref[...])

    cos = cos_ref[...]
    sin = sin_ref[...]
    lane = lax.broadcasted_iota(jnp.int32, (tm, LANES), 1) % HEAD_DIM
    first_half = lane < ROPE_DIM // 2

    def rope(t):
        rot = jnp.where(first_half,
                        -pltpu.roll(t, LANES - ROPE_DIM // 2, 1),
                        pltpu.roll(t, ROPE_DIM // 2, 1))
        return t * cos + rot * sin

    n_col = aw // LANES
    for j in range(n_col):
        sl = slice(j * LANES, (j + 1) * LANES)
        stage[0, j] = rope(proj[:, sl]) * (HEAD_DIM ** -0.5)
        stage[1, j] = rope(proj[:, aw + j * LANES: aw + (j + 1) * LANES])
        stage[2, j] = proj[:, 2 * aw + j * LANES: 2 * aw + (j + 1) * LANES]
    for (_, dil), refs in zip(DILATED_PAIRS, qkv_refs):
        for which, ref in enumerate(refs):
            for j in range(n_col):
                sl = slice(j * LANES, (j + 1) * LANES)
                if dil == 1:
                    ref[:, sl] = stage[which, j].astype(BF16)
                else:
                    for r in range(dil):
                        ref[r, :, sl] = stage[which, j, pl.ds(r, tm // dil, stride=dil), :].astype(BF16)

    b_gate = proj[:, 3 * aw:3 * aw + cwid]
    c_gate = proj[:, 3 * aw + cwid:3 * aw + 2 * cwid]
    xin = proj[:, 3 * aw + 2 * cwid:]
    u = c_gate * xin

    @pl.when(pl.program_id(1) == 0)
    def _():
        ubuf[0:8, :] = jnp.zeros((8, cwid), F32)

    ubuf[8:, :] = u
    conv = (cw_ref[0:1, :] * ubuf[6:6 + tm, :] + cw_ref[1:2, :] * ubuf[7:7 + tm, :]
            + cw_ref[2:3, :] * u)
    bo_ref[...] = (b_gate * conv).astype(BF16)
    ubuf[0:8, :] = ubuf[tm:tm + 8, :]


def _even_inproj(x, g, shift, scale, w_in, li, cos, sin, conv_w):
    bsz, seq, d = x.shape
    n = w_in.shape[2]
    cwid = conv_w.shape[1]
    tm = ROW_TILE
    aw = ATTN_WIDTH
    row = lambda w: pl.BlockSpec((None, tm, w), lambda b, i: (b, i, 0))
    per_b = pl.BlockSpec((None, 1, d), lambda b, i: (b, 0, 0))
    qkv_specs, qkv_shapes = [], []
    for window, dil in DILATED_PAIRS:
        assert window // dil == ATTN_BLOCK and seq % window == 0 and (dil == 1 or window % tm == 0)
        if dil == 1:
            spec, shape = row(aw), (bsz, seq, aw)
        else:
            per_span = window // tm
            spec = pl.BlockSpec((None, None, dil, tm // dil, aw),
                                lambda b, i, per_span=per_span: (b, i // per_span, 0, i % per_span, 0))
            shape = (bsz, seq // window, dil, ATTN_BLOCK, aw)
        qkv_specs += [spec] * 3
        qkv_shapes += [jax.ShapeDtypeStruct(shape, BF16)] * 3
    outs = pl.pallas_call(
        _inproj_kernel,
        grid=(bsz, seq // tm),
        in_specs=[row(d), pl.BlockSpec((1, d), lambda b, i: (0, 0)), per_b, per_b,
                  pl.BlockSpec((None, d, n), lambda b, i: (li, 0, 0)),
                  row(LANES), row(LANES),
                  pl.BlockSpec(conv_w.shape, lambda b, i: (0, 0))],
        out_specs=qkv_specs + [row(cwid)],
        out_shape=qkv_shapes + [jax.ShapeDtypeStruct((bsz, seq, cwid), BF16)],
        scratch_shapes=[pltpu.VMEM((tm + 8, cwid), F32),
                        pltpu.VMEM((3, aw // LANES, tm, LANES), F32)],
        compiler_params=_params(2),
        name="even_inproj",
    )(x, g.reshape(1, d), shift.reshape(bsz, 1, d), scale.reshape(bsz, 1, d),
      w_in, cos, sin, conv_w)
    qkv = [outs[3 * p:3 * p + 3] for p in range(len(DILATED_PAIRS))]
    return qkv, outs[-1]


def _attn_item(q, kk, vv, valid):
    blk = ATTN_BLOCK
    lane = lax.broadcasted_iota(jnp.int32, (blk, LANES), 1)
    lane_kv = lax.broadcasted_iota(jnp.int32, (2 * blk, LANES), 1)
    neg_inf = jnp.float32(-jnp.inf)
    ml_new = jnp.zeros((blk, LANES), F32)
    n_col, per_col = ATTN_WIDTH // LANES, LANES // HEAD_DIM
    scores = []
    for j in range(n_col):
        sl = slice(j * LANES, (j + 1) * LANES)
        for hh in range(per_col):
            in_head = (lane // HEAD_DIM) == hh
            qm = jnp.where(in_head, q[:, sl], jnp.zeros_like(q[:, sl]))
            s = lax.dot_general(qm, kk[:, sl], (((1,), (1,)), ((), ())),
                                preferred_element_type=F32)
            scores.append(jnp.where(valid, s, neg_inf))
    accs = []
    for j in range(n_col):
        sl = slice(j * LANES, (j + 1) * LANES)
        vj = vv[:, sl]
        pv = []
        for hh in range(per_col):
            head = j * per_col + hh
            in_head_kv = (lane_kv // HEAD_DIM) == hh
            s = scores[head]
            m_new = jnp.max(s, axis=1, keepdims=True)
            p = jnp.exp(s - m_new)
            l_new = jnp.sum(p, axis=1, keepdims=True)
            vm = jnp.where(in_head_kv, vj, jnp.zeros_like(vj))
            pv.append(_dot(p.astype(BF16), vm))
            ml_new = jnp.where(lane == head, m_new, ml_new)
            ml_new = jnp.where(lane == ATTN_HEADS + head, l_new, ml_new)
        accs.append(pv[0] + pv[1])
    return accs, ml_new


def _head_columns(tile, first_lane, j):
    lane = lax.broadcasted_iota(jnp.int32, tile.shape, 1)
    h0 = first_lane + j * (LANES // HEAD_DIM)
    return jnp.where(lane < HEAD_DIM, tile[:, h0:h0 + 1], tile[:, h0 + 1:h0 + 2])


def _attn_merge(acc_old, ml_old, acc_loc, ml_loc):
    lane = lax.broadcasted_iota(jnp.int32, ml_old.shape, 1)
    m_new = jnp.maximum(ml_old, ml_loc)
    a_old = jnp.exp(ml_old - m_new)
    a_loc = jnp.exp(ml_loc - m_new)
    l_new = (pltpu.roll(a_old, ATTN_HEADS, 1) * ml_old
             + pltpu.roll(a_loc, ATTN_HEADS, 1) * ml_loc)
    ml_new = jnp.where(lane < ATTN_HEADS, m_new, jnp.where(lane < 2 * ATTN_HEADS, l_new, 0.0))
    accs = [_head_columns(a_old, 0, j) * acc_old[j] + _head_columns(a_loc, 0, j) * acc_loc[j]
            for j in range(len(acc_loc))]
    return accs, ml_new


def _band_mask(has_prev):
    blk = ATTN_BLOCK
    row = lax.broadcasted_iota(jnp.int32, (blk, 2 * blk), 0)
    col = lax.broadcasted_iota(jnp.int32, (blk, 2 * blk), 1)
    band = (col >= row) & (col <= row + blk)
    return band, band & ((col >= blk) | has_prev)


def _attn_wide_kernel(q_ref, kc_ref, vc_ref, kp_ref, vp_ref, acc_out, ml_out):
    _, valid = _band_mask(pl.program_id(1) > 0)
    for r in range(q_ref.shape[0]):
        kk = jnp.concatenate([kp_ref[r], kc_ref[r]], axis=0)
        vv = jnp.concatenate([vp_ref[r], vc_ref[r]], axis=0)
        accs, ml_new = _attn_item(q_ref[r], kk, vv, valid)
        for j, a in enumerate(accs):
            acc_out[r, :, j * LANES:(j + 1) * LANES] = a
        ml_out[r] = ml_new


def _attn_mid_kernel(q_ref, kc_ref, vc_ref, kp_ref, vp_ref, accw_ref, mlw_ref, acc_out, ml_out,
                     old_acc, old_ml, loc_acc, loc_ml):
    dil = q_ref.shape[0]
    ratio = accw_ref.shape[0] // dil
    sub = accw_ref.shape[1]
    n_col = ATTN_WIDTH // LANES
    blk = ATTN_BLOCK
    _, valid = _band_mask(pl.program_id(1) > 0)
    for r in range(dil):
        for qd in range(ratio):
            rows = pl.ds(r * blk + qd, sub, stride=ratio)
            for j in range(n_col):
                old_acc[j, rows, :] = accw_ref[dil * qd + r, :, j * LANES:(j + 1) * LANES]
            old_ml[rows, :] = mlw_ref[dil * qd + r]
        kk = jnp.concatenate([kp_ref[r], kc_ref[r]], axis=0)
        vv = jnp.concatenate([vp_ref[r], vc_ref[r]], axis=0)
        accs, ml = _attn_item(q_ref[r], kk, vv, valid)
        for j, a in enumerate(accs):
            loc_acc[j, r * blk:(r + 1) * blk, :] = a
        loc_ml[r * blk:(r + 1) * blk, :] = ml
    accs, ml_new = _attn_merge([old_acc[j] for j in range(n_col)], old_ml[...],
                               [loc_acc[j] for j in range(n_col)], loc_ml[...])
    for r in range(dil):
        rows = pl.ds(r, blk, stride=dil)
        for j, a in enumerate(accs):
            acc_out[j, rows, :] = a[r * blk:(r + 1) * blk]
        ml_out[rows, :] = ml_new[r * blk:(r + 1) * blk]


def _attn_last_kernel(q_ref, kc_ref, vc_ref, kp_ref, vp_ref, acc_in, ml_in,
                      b_ref, x_ref, wa_ref, wb_ref, gm_ref, g_ref, sh_ref, sc_ref,
                      x1_ref, h_ref, loc_acc, loc_ml):
    gb = q_ref.shape[0]
    n_col = ATTN_WIDTH // LANES
    blk = ATTN_BLOCK
    band, valid0 = _band_mask(pl.program_id(1) > 0)
    for i in range(gb):
        if i == 0:
            k_prev, v_prev, valid = kp_ref[...], vp_ref[...], valid0
        else:
            k_prev, v_prev, valid = kc_ref[i - 1], vc_ref[i - 1], band
        kk = jnp.concatenate([k_prev, kc_ref[i]], axis=0)
        vv = jnp.concatenate([v_prev, vc_ref[i]], axis=0)
        accs, ml = _attn_item(q_ref[i], kk, vv, valid)
        for j, a in enumerate(accs):
            loc_acc[j, i * blk:(i + 1) * blk, :] = a
        loc_ml[i * blk:(i + 1) * blk, :] = ml
    ml_old = ml_in[...].reshape(gb * blk, LANES)
    accs, ml_new = _attn_merge([acc_in[j] for j in range(n_col)], ml_old,
                               [loc_acc[j] for j in range(n_col)], loc_ml[...])
    lane = lax.broadcasted_iota(jnp.int32, ml_new.shape, 1)
    is_l = (lane >= ATTN_HEADS) & (lane < 2 * ATTN_HEADS)
    linv = 1.0 / jnp.where(is_l, ml_new, 1.0)
    attn = jnp.concatenate([(a * _head_columns(linv, ATTN_HEADS, j)).astype(BF16)
                            for j, a in enumerate(accs)], axis=1)
    conv = b_ref[...].reshape(gb * blk, b_ref.shape[2])
    mix = _dot(attn, wa_ref[...]) + _dot(conv, wb_ref[...])
    x1 = x_ref[...] + gm_ref[...] * mix
    x1_ref[...] = x1
    h_ref[...] = _ada_norm(x1, g_ref[...], sh_ref[...], sc_ref[...]).astype(BF16)


def _attention_outproj(qkv, b_out, x, w_out, li, gate_m, g, shift, scale):
    (_, d1), (_, dm), (_, dw) = DILATED_PAIRS
    assert d1 == 1 and dw % dm == 0
    blk, aw = ATTN_BLOCK, ATTN_WIDTH
    (q1, k1, v1), (qm, km, vm), (qw, kw, vw) = qkv
    bsz, seq, _ = q1.shape

    nw = qw.shape[1]
    rpg = min(ATTN_GROUP, dw)
    cur = lambda w: pl.BlockSpec((None, None, rpg, blk, w), lambda b, n, g: (b, n, g, 0, 0))
    prev = pl.BlockSpec((None, None, rpg, blk, aw),
                        lambda b, n, g: (b, jnp.maximum(n - 1, 0), g, 0, 0))
    acc_w, ml_w = pl.pallas_call(
        _attn_wide_kernel,
        grid=(bsz, nw, dw // rpg),
        in_specs=[cur(aw), cur(aw), cur(aw), prev, prev],
        out_specs=[cur(aw), cur(LANES)],
        out_shape=[jax.ShapeDtypeStruct((bsz, nw, dw, blk, aw), F32),
                   jax.ShapeDtypeStruct((bsz, nw, dw, blk, LANES), F32)],
        compiler_params=_params(3),
        name=f"attn_dil{dw}",
    )(qw, kw, vw, kw, vw)

    nm = qm.shape[1]
    ratio = dw // dm
    sub = blk // ratio
    cur = pl.BlockSpec((None, None, dm, blk, aw), lambda b, n: (b, n, 0, 0, 0))
    prev = pl.BlockSpec((None, None, dm, blk, aw),
                        lambda b, n: (b, jnp.maximum(n - 1, 0), 0, 0, 0))
    wide = lambda w: pl.BlockSpec((None, None, dw, sub, w),
                                  lambda b, n: (b, n // ratio, 0, n % ratio, 0))
    n_col = aw // LANES
    acc, ml = pl.pallas_call(
        _attn_mid_kernel,
        grid=(bsz, nm),
        in_specs=[cur, cur, cur, prev, prev, wide(aw), wide(LANES)],
        out_specs=[pl.BlockSpec((None, n_col, dm * blk, LANES), lambda b, n: (b, 0, n, 0)),
                   pl.BlockSpec((None, dm * blk, LANES), lambda b, n: (b, n, 0))],
        out_shape=[jax.ShapeDtypeStruct((bsz, n_col, seq, LANES), F32),
                   jax.ShapeDtypeStruct((bsz, seq, LANES), F32)],
        scratch_shapes=[pltpu.VMEM((n_col, dm * blk, LANES), F32), pltpu.VMEM((dm * blk, LANES), F32),
                        pltpu.VMEM((n_col, dm * blk, LANES), F32), pltpu.VMEM((dm * blk, LANES), F32)],
        compiler_params=_params(2),
        name=f"attn_dil{dm}",
    )(qm, km, vm, km, vm, acc_w, ml_w)

    nb = seq // blk
    gb = min(ATTN_GROUP, nb)
    view = lambda t: t.reshape(bsz, nb, blk, t.shape[-1])
    cur = lambda w: pl.BlockSpec((None, gb, blk, w), lambda b, n: (b, n, 0, 0))
    prev = pl.BlockSpec((None, None, blk, aw), lambda b, n: (b, jnp.maximum(n * gb - 1, 0), 0, 0))
    d = x.shape[2]
    bw = b_out.shape[2]
    assert w_out.shape[1] == aw + bw and bw == aw
    rows = pl.BlockSpec((None, gb * blk, d), lambda b, n: (b, n, 0))
    per_b = pl.BlockSpec((None, 1, d), lambda b, n: (b, 0, 0))
    pb = lambda t: t.reshape(bsz, 1, d)
    return pl.pallas_call(
        _attn_last_kernel,
        grid=(bsz, nb // gb),
        in_specs=[cur(aw), cur(aw), cur(aw), prev, prev,
                  pl.BlockSpec((None, n_col, gb * blk, LANES), lambda b, n: (b, 0, n, 0)),
                  cur(LANES), cur(bw), rows,
                  pl.BlockSpec((None, aw, d), lambda b, n: (li, 0, 0)),
                  pl.BlockSpec((None, bw, d), lambda b, n: (li, 1, 0)),
                  per_b, pl.BlockSpec((1, d), lambda b, n: (0, 0)), per_b, per_b],
        out_specs=[rows, rows],
        out_shape=[jax.ShapeDtypeStruct((bsz, seq, d), F32),
                   jax.ShapeDtypeStruct((bsz, seq, d), BF16)],
        scratch_shapes=[pltpu.VMEM((n_col, gb * blk, LANES), F32),
                        pltpu.VMEM((gb * blk, LANES), F32)],
        compiler_params=_params(2),
        name=f"attn_dil{d1}_outproj",
    )(view(q1), view(k1), view(v1), view(k1), view(v1), acc, view(ml), view(b_out), x,
      w_out, w_out, pb(gate_m), g.reshape(1, d), pb(shift), pb(scale))


def _ffn_kernel(h_ref, x_ref, wg_ref, wu_ref, wd_ref, gf_ref, g_ref, sh_ref, sc_ref,
                o_ref, hn_ref):
    h = h_ref[...]
    g = _dot(h, wg_ref[...])
    u = _dot(h, wu_ref[...])
    act = (g * _sigmoid(g) * u).astype(BF16)
    x2 = x_ref[...] + gf_ref[...] * _dot(act, wd_ref[...])
    o_ref[...] = x2
    hn_ref[...] = _ada_norm(x2, g_ref[...], sh_ref[...], sc_ref[...]).astype(BF16)


def _dense_ffn(h, x, w_gate, w_up, w_down, li, gate_f, next_g, next_shift, next_scale):
    bsz, seq, d = x.shape
    ff = w_gate.shape[2]
    tm = ROW_TILE
    row = pl.BlockSpec((None, tm, d), lambda b, i: (b, i, 0))
    per_b = pl.BlockSpec((None, 1, d), lambda b, i: (b, 0, 0))
    full = lambda s: pl.BlockSpec((None,) + s, lambda b, i: (li, 0, 0),
                                  pipeline_mode=pl.Buffered(1))
    pb = lambda t: t.reshape(bsz, 1, d)
    return pl.pallas_call(
        _ffn_kernel,
        grid=(bsz, seq // tm),
        in_specs=[row, row, full((d, ff)), full((d, ff)), full((ff, d)), per_b,
                  pl.BlockSpec((1, d), lambda b, i: (0, 0)), per_b, per_b],
        out_specs=[row, row],
        out_shape=[jax.ShapeDtypeStruct((bsz, seq, d), F32),
                   jax.ShapeDtypeStruct((bsz, seq, d), BF16)],
        compiler_params=_params(2),
        name="dense_ffn",
    )(h, x, w_gate, w_up, w_down, pb(gate_f), next_g.reshape(1, d), pb(next_shift),
      pb(next_scale))


def _ssm_operators(a_re, a_im, log_step, b_re, b_im, c_re, c_im):
    L = SSM_CHUNK
    a_re, a_im = a_re.astype(F32), a_im.astype(F32)
    dt = jnp.exp(log_step.astype(F32))[:, None]
    mag = jnp.exp(a_re * dt)
    abar_re = mag * jnp.cos(a_im * dt)
    abar_im = mag * jnp.sin(a_im * dt)
    den = a_re * a_re + a_im * a_im
    nr = abar_re - 1.0
    f_re = (nr * a_re + abar_im * a_im) / den
    f_im = (abar_im * a_re - nr * a_im) / den
    b_re, b_im = b_re.astype(F32), b_im.astype(F32)
    bb_re = f_re[..., None] * b_re - f_im[..., None] * b_im
    bb_im = f_re[..., None] * b_im + f_im[..., None] * b_re
    c_re, c_im = c_re.astype(F32), c_im.astype(F32)

    def step(carry, _):
        pr, pi = carry
        nxt = (pr * abar_re - pi * abar_im, pr * abar_im + pi * abar_re)
        return nxt, carry
    (pl_re, pl_im), (pw_re, pw_im) = lax.scan(
        step, (jnp.ones_like(abar_re), jnp.zeros_like(abar_re)), None, length=L)
    pw_re = jnp.concatenate([pw_re, pl_re[None]], axis=0)
    pw_im = jnp.concatenate([pw_im, pl_im[None]], axis=0)

    hi = lax.Precision.HIGHEST
    cp_re = c_re[None] * pw_re[:L, :, None, :] - c_im[None] * pw_im[:L, :, None, :]
    cp_im = c_re[None] * pw_im[:L, :, None, :] + c_im[None] * pw_re[:L, :, None, :]
    w = (jnp.einsum('tgcp,gpd->tgdc', cp_re, bb_re, precision=hi)
         - jnp.einsum('tgcp,gpd->tgdc', cp_im, bb_im, precision=hi))
    s_idx = jnp.arange(L)[:, None]
    t_idx = jnp.arange(L)[None, :]
    tau = t_idx - s_idx
    toep = jnp.where((tau >= 0)[:, :, None, None, None], w[jnp.clip(tau, 0, L - 1)], 0.0)
    n_g, n_c = a_re.shape[0], b_re.shape[2]
    toep = toep.transpose(2, 0, 3, 1, 4).reshape(n_g, L * n_c, L * n_c)

    rp_re, rp_im = pw_re[:L][::-1], pw_im[:L][::-1]
    so_re = rp_re[..., None] * bb_re[None] - rp_im[..., None] * bb_im[None]
    so_im = rp_re[..., None] * bb_im[None] + rp_im[..., None] * bb_re[None]
    to_rows = lambda t: t.transpose(1, 0, 3, 2).reshape(n_g, L * n_c, -1)
    s_out = jnp.concatenate([to_rows(so_re), to_rows(so_im)], axis=-1)
    s_out_sw = jnp.concatenate([to_rows(so_im), to_rows(so_re)], axis=-1)

    qp_re, qp_im = pw_re[1:], pw_im[1:]
    ci_re = c_re[None] * qp_re[:, :, None, :] - c_im[None] * qp_im[:, :, None, :]
    ci_im = c_re[None] * qp_im[:, :, None, :] + c_im[None] * qp_re[:, :, None, :]
    to_cols = lambda t: t.transpose(1, 3, 0, 2).reshape(n_g, -1, L * n_c)
    c_in = jnp.concatenate([to_cols(ci_re), -to_cols(ci_im)], axis=1)

    al_re, al_im = pw_re[L], pw_im[L]
    a1 = jnp.concatenate([al_re, al_re], axis=-1)[:, None, :]
    a2 = jnp.concatenate([-al_im, al_im], axis=-1)[:, None, :]
    return (toep.astype(BF16), s_out.astype(BF16), s_out_sw.astype(BF16),
            c_in.astype(BF16), a1, a2)


def _piece_transpose(arrs):
    n = len(arrs)
    piece = lax.broadcasted_iota(jnp.int32, (1, LANES), 1) // SSM_GROUP
    arrs = list(arrs)
    dist = n // 2
    while dist >= 1:
        keep = (piece & dist) == 0
        for i in range(n):
            if i & dist == 0:
                a, b = arrs[i], arrs[i + dist]
                arrs[i] = jnp.where(keep, a, pltpu.roll(b, dist * SSM_GROUP, 1))
                arrs[i + dist] = jnp.where(keep, pltpu.roll(a, LANES - dist * SSM_GROUP, 1), b)
        dist //= 2
    return arrs


def _ssm_kernel(u_ref, t_ref, s_ref, ssw_ref, cin_ref, a1_ref, a2_ref, y_ref,
                st_ref, stsw_ref, xs_ref, xc_ref, zc_ref):
    kt, L, bsz, _ = u_ref.shape
    rows = kt * bsz
    n_lg = SSM_LANE_GROUPS

    @pl.when(pl.program_id(1) == 0)
    def _():
        xc_ref[...] = jnp.zeros(xc_ref.shape, F32)
        zc_ref[...] = jnp.zeros(zc_ref.shape, F32)

    zs = [u_ref[:, s].reshape(rows, LANES) for s in range(L)]
    lo = _piece_transpose(zs[:n_lg])
    hi = _piece_transpose(zs[n_lg:])
    ys = []
    for gl in range(n_lg):
        v = jnp.concatenate([lo[gl], hi[gl]], axis=1)
        st_ref[...] = _dot(v, s_ref[gl])
        stsw_ref[...] = _dot(v, ssw_ref[gl])
        a1 = jnp.broadcast_to(a1_ref[gl], (bsz, 2 * SSM_STATE))
        a2 = jnp.broadcast_to(a2_ref[gl], (bsz, 2 * SSM_STATE))

        def body(k, carry):
            x, z = carry
            r = pl.ds(pl.multiple_of(k * bsz, bsz), bsz)
            xs_ref[r, :] = x
            x_new = x * a1 + z * a2 + st_ref[r, :]
            z_new = z * a1 - x * a2 + stsw_ref[r, :]
            return x_new, z_new

        x, z = lax.fori_loop(0, kt, body, (xc_ref[gl], zc_ref[gl]), unroll=8)
        xc_ref[gl] = x
        zc_ref[gl] = z
        ys.append((_dot(v, t_ref[gl]) + _dot(xs_ref[...].astype(BF16), cin_ref[gl])).astype(BF16))
    out_lo = _piece_transpose([y[:, :LANES] for y in ys])
    out_hi = _piece_transpose([y[:, LANES:] for y in ys])
    for t in range(n_lg):
        y_ref[:, t] = out_lo[t].reshape(kt, bsz, LANES)
        y_ref[:, n_lg + t] = out_hi[t].reshape(kt, bsz, LANES)


def _ssm_scan(h, ops):
    bsz, seq, d = h.shape
    toep, s_out, s_out_sw, c_in, a1, a2 = ops
    L, C, n_lg = SSM_CHUNK, SSM_GROUP, SSM_LANE_GROUPS
    assert L == 2 * n_lg
    n_k = seq // L
    kt = min(SSM_CHUNKS_PER_STEP, n_k)
    hv = h.transpose(1, 0, 2).reshape(n_k, L, bsz, d)
    act = pl.BlockSpec((kt, L, bsz, LANES), lambda o, k: (k, 0, 0, o))
    wblk = lambda s: pl.BlockSpec((n_lg,) + s, lambda o, k: (o, 0, 0))
    y = pl.pallas_call(
        _ssm_kernel,
        grid=(d // LANES, n_k // kt),
        in_specs=[act, wblk((L * C, L * C)), wblk((L * C, 2 * SSM_STATE)),
                  wblk((L * C, 2 * SSM_STATE)), wblk((2 * SSM_STATE, L * C)),
                  wblk((1, 2 * SSM_STATE)), wblk((1, 2 * SSM_STATE))],
        out_specs=act,
        out_shape=jax.ShapeDtypeStruct((n_k, L, bsz, d), BF16),
        scratch_shapes=[pltpu.VMEM((kt * bsz, 2 * SSM_STATE), F32)] * 3
        + [pltpu.VMEM((n_lg, bsz, 2 * SSM_STATE), F32)] * 2,
        compiler_params=_params(2),
        name="ssm_scan",
    )(hv, toep, s_out, s_out_sw, c_in, a1, a2)
    return y.reshape(seq, bsz, d).transpose(1, 0, 2)


def _gelu_tanh(x):
    return 0.5 * x * (1.0 + jnp.tanh(math.sqrt(2.0 / math.pi) * (x + 0.044715 * (x * x * x))))


def _ssm_out_kernel(y_ref, h_ref, x_ref, d_ref, w_ref, b_ref, gm_ref, g_ref, sh_ref, sc_ref,
                    rw_ref, rb_ref, x1_ref, h2_ref, route_ref):
    d = x_ref.shape[1]
    y = y_ref[...].astype(F32) + d_ref[...] * h_ref[...].astype(F32)
    z = _dot(_gelu_tanh(y).astype(BF16), w_ref[...]) + b_ref[...]
    mix = z[:, :d] * _sigmoid(z[:, d:])
    x1 = x_ref[...] + gm_ref[...] * mix
    x1_ref[...] = x1
    h2 = _ada_norm(x1, g_ref[...], sh_ref[...], sc_ref[...])
    h2_ref[...] = h2

    hh, hl = _split_bf16(h2)
    both = _dot(hh, rw_ref[...])
    logits = (both[:, :LANES] + both[:, LANES:] + _dot(hl, rw_ref[:, :LANES]) + rb_ref[...])
    lane = lax.broadcasted_iota(jnp.int32, logits.shape, 1)
    lane_f = lane.astype(F32)
    neg_inf = jnp.float32(-jnp.inf)
    logits = jnp.where(lane < N_EXPERTS, logits, neg_inf)
    m1 = jnp.max(logits, axis=1, keepdims=True)
    i1 = jnp.min(jnp.where(logits == m1, lane_f, float(LANES)), axis=1, keepdims=True)
    rest = jnp.where(lane_f == i1, neg_inf, logits)
    m2 = jnp.max(rest, axis=1, keepdims=True)
    i2 = jnp.min(jnp.where(rest == m2, lane_f, float(LANES)), axis=1, keepdims=True)
    e2 = jnp.exp(m2 - m1)
    g1 = 1.0 / (1.0 + e2)
    g2 = e2 / (1.0 + e2)
    route = jnp.where(lane == 0, i1, jnp.where(lane == 1, i2,
                      jnp.where(lane == 2, g1, jnp.where(lane == 3, g2, 0.0))))
    route_ref[...] = route


def _ssm_out(y, h, x, d_skip, glu_w, li, glu_b, gate_m, g, shift, scale, router_w, router_b):
    bsz, seq, d = x.shape
    tm = ROW_TILE
    row = lambda w: pl.BlockSpec((None, tm, w), lambda b, i: (b, i, 0))
    per_b = pl.BlockSpec((None, 1, d), lambda b, i: (b, 0, 0))
    full = lambda s: pl.BlockSpec(s, lambda b, i: (0, 0))
    pb = lambda t: t.reshape(bsz, 1, d)
    rw = jnp.zeros((d, LANES), F32).at[:, :N_EXPERTS].set(router_w)
    rwh = rw.astype(BF16)
    rw_cat = jnp.concatenate([rwh, (rw - rwh.astype(F32)).astype(BF16)], axis=1)
    rb = jnp.zeros((1, LANES), F32).at[0, :N_EXPERTS].set(router_b)
    return pl.pallas_call(
        _ssm_out_kernel,
        grid=(bsz, seq // tm),
        in_specs=[row(d), row(d), row(d), full((1, d)),
                  pl.BlockSpec((None, d, 2 * d), lambda b, i: (li, 0, 0)), full((1, 2 * d)),
                  per_b, full((1, d)), per_b, per_b,
                  full((d, 2 * LANES)), full((1, LANES))],
        out_specs=[row(d), row(d), row(LANES)],
        out_shape=[jax.ShapeDtypeStruct((bsz, seq, d), F32),
                   jax.ShapeDtypeStruct((bsz, seq, d), F32),
                   jax.ShapeDtypeStruct((bsz, seq, LANES), F32)],
        compiler_params=_params(2),
        name="ssm_out",
    )(y, h, x, d_skip.reshape(1, d), glu_w, glu_b.reshape(1, 2 * d), pb(gate_m),
      g.reshape(1, d), pb(shift), pb(scale), rw_cat, rb)


def _routing_tables(route, tm):
    n = route.shape[0]
    i32 = jnp.int32
    e_flat = route[:, :2].astype(i32).T.reshape(-1)
    counts = jnp.sum((e_flat[:, None] == jnp.arange(N_EXPERTS)[None, :]).astype(i32), axis=0)
    order = jnp.sort(e_flat * (2 * n) + jnp.arange(2 * n, dtype=i32)) % (2 * n)
    padded = ((counts + tm - 1) // tm) * tm
    ends = jnp.cumsum(padded)
    starts = ends - padded
    n_tiles = (2 * n) // tm + N_EXPERTS
    n_rows = n_tiles * tm
    s = jnp.arange(n_rows, dtype=i32)
    e_of_s = jnp.sum((s[:, None] >= ends[None, :]).astype(i32), axis=1)
    starts9 = jnp.concatenate([starts, ends[-1:]])
    counts9 = jnp.concatenate([counts, jnp.zeros((1,), i32)])
    ustart9 = jnp.concatenate([jnp.cumsum(counts) - counts, jnp.full((1,), 2 * n, i32)])
    within = s - starts9[e_of_s]
    real_before = ustart9[e_of_s] + jnp.minimum(within, counts9[e_of_s])
    pair = jnp.where(within < counts9[e_of_s],
                     order[jnp.minimum(ustart9[e_of_s] + within, 2 * n - 1)],
                     2 * n + s - real_before)
    src = jnp.concatenate([jnp.where(pair < 2 * n, pair % n, 0), jnp.zeros((tm,), i32)])
    dst = jnp.concatenate([n_rows + jnp.arange(tm, dtype=i32), pair])
    tile_expert = jnp.minimum(e_of_s[::tm], N_EXPERTS - 1)
    return src, dst, tile_expert


def _moe_kernel(te_ref, src_ref, src_next_ref, dst_prev_ref, dst_ref, h_hbm, wg_ref, wu_ref, wd_ref,
                y_hbm, hbuf, ybuf, gsem, ssem):
    t = pl.program_id(0)
    n_t = pl.num_programs(0)
    tm = hbuf.shape[1]
    ff = wg_ref.shape[1]
    fc = ff // MOE_FF_CHUNKS
    slot = t % 2
    other = 1 - slot

    def gather(slot_, r, row):
        return pltpu.make_async_copy(h_hbm.at[pl.ds(row, 1)], hbuf.at[slot_, pl.ds(r, 1)],
                                     gsem.at[slot_])

    def scatter(slot_, r, row):
        return pltpu.make_async_copy(ybuf.at[slot_, pl.ds(r, 1)], y_hbm.at[pl.ds(row, 1)],
                                     ssem.at[slot_])

    def for_rows(fn):
        def body(r, c):
            fn(r)
            return c
        lax.fori_loop(0, tm, body, 0, unroll=8)

    @pl.when(t == 0)
    def _():
        ybuf[...] = jnp.zeros(ybuf.shape, F32)
        for_rows(lambda r: gather(0, r, src_ref[r]).start())

    for_rows(lambda r: gather(slot, r, 0).wait())

    @pl.when(t > 0)
    def _():
        for_rows(lambda r: scatter(slot, r, 0).wait())

    h = hbuf[slot].astype(BF16)
    for r in range(tm):
        gather(other, r, src_next_ref[r]).start()
    for r in range(tm):
        scatter(other, r, dst_prev_ref[r]).start()

    acc = jnp.zeros((tm, wd_ref.shape[1]), F32)
    for c in range(MOE_FF_CHUNKS):
        g = _dot(h, wg_ref[:, c * fc:(c + 1) * fc])
        u = _dot(h, wu_ref[:, c * fc:(c + 1) * fc])
        act = (g * _sigmoid(g) * u).astype(BF16)
        acc = acc + _dot(act, wd_ref[c * fc:(c + 1) * fc, :])
    ybuf[slot] = acc

    @pl.when(t == n_t - 1)
    def _():
        for_rows(lambda r: gather(other, r, 0).wait())
        for_rows(lambda r: scatter(other, r, 0).wait())
        for_rows(lambda r: scatter(slot, r, dst_ref[r]).start())
        for_rows(lambda r: scatter(slot, r, 0).wait())


def _moe_experts(h2, src, dst, tile_expert, w_gate, w_up, w_down, li):
    n, d = h2.shape
    ff = w_gate.shape[3]
    tm = MOE_TILE
    n_tiles = tile_expert.shape[0]
    wspec = lambda s: pl.BlockSpec((None, None) + s, lambda t, te: (li, te[t], 0, 0),
                                   pipeline_mode=pl.Buffered(1))
    idx = lambda off: pl.BlockSpec((tm,), lambda t, te: (t + off,), memory_space=pltpu.SMEM)
    grid_spec = pltpu.PrefetchScalarGridSpec(
        num_scalar_prefetch=1,
        grid=(n_tiles,),
        in_specs=[idx(0), idx(1), idx(0), idx(1),
                  pl.BlockSpec(memory_space=pl.ANY),
                  wspec((d, ff)), wspec((d, ff)), wspec((ff, d))],
        out_specs=pl.BlockSpec(memory_space=pl.ANY),
        scratch_shapes=[pltpu.VMEM((2, tm, d), F32), pltpu.VMEM((2, tm, d), F32),
                        pltpu.SemaphoreType.DMA((2,)), pltpu.SemaphoreType.DMA((2,))],
    )
    return pl.pallas_call(
        _moe_kernel,
        grid_spec=grid_spec,
        out_shape=jax.ShapeDtypeStruct((n_tiles * tm + tm, d), F32),
        compiler_params=_params(1),
        name="moe_experts",
    )(tile_expert, src, src, dst, dst, h2, w_gate, w_up, w_down)


def _combine_kernel(y1_ref, y2_ref, route_ref, x_ref, gf_ref, fg_ref, o_ref, *, final):
    route = route_ref[...]
    ff = route[:, 2:3] * y1_ref[...] + route[:, 3:4] * y2_ref[...]
    x2 = x_ref[...] + gf_ref[...] * ff
    if final:
        ms = jnp.mean(x2 * x2, axis=-1, keepdims=True)
        x2 = x2 * lax.rsqrt(ms + RMS_EPS) * fg_ref[...]
    o_ref[...] = x2


def _moe_combine(y, route, x1, gate_f, final_g, final):
    bsz, seq, d = x1.shape
    n = bsz * seq
    tm = ROW_TILE
    per_batch = seq // tm
    out = pl.pallas_call(
        functools.partial(_combine_kernel, final=final),
        grid=(n // tm,),
        in_specs=[pl.BlockSpec((tm, d), lambda i: (i, 0)),
                  pl.BlockSpec((tm, d), lambda i: (n // tm + i, 0)),
                  pl.BlockSpec((tm, LANES), lambda i: (i, 0)),
                  pl.BlockSpec((tm, d), lambda i: (i, 0)),
                  pl.BlockSpec((None, 1, d), lambda i: (i // per_batch, 0, 0)),
                  pl.BlockSpec((1, d), lambda i: (0, 0))],
        out_specs=pl.BlockSpec((tm, d), lambda i: (i, 0)),
        out_shape=jax.ShapeDtypeStruct((n, d), F32),
        compiler_params=_params(1),
        name="moe_combine",
    )(y, y, route.reshape(n, LANES), x1.reshape(n, d),
      gate_f.reshape(bsz, 1, d), final_g.reshape(1, d))
    return out.reshape(bsz, seq, d)


def _even_layer(x, mods, norm_mix_g, norm_ffn_g, cos, sin, li, w_in, conv_w, w_out,
                w_gate, w_up, w_down, next_norm):
    sh_m, sc_m, g_m, sh_f, sc_f, g_f = mods
    qkv, b_out = _even_inproj(x, norm_mix_g, sh_m, sc_m, w_in, li, cos, sin, conv_w)
    x1, h2 = _attention_outproj(qkv, b_out, x, w_out, li, g_m, norm_ffn_g, sh_f, sc_f)
    return _dense_ffn(h2, x1, w_gate, w_up, w_down, li, g_f, *next_norm)


def _odd_layer(x, h, mods, norm_ffn_g, ssm, d_skip, li, glu_w, glu_b,
               router_w, router_b, w_gate, w_up, w_down, final_g, final):
    _, _, g_m, sh_f, sc_f, g_f = mods
    bsz, seq, d = x.shape
    y = _ssm_scan(h, _ssm_operators(*ssm))
    x1, h2, route = _ssm_out(y, h, x, d_skip, glu_w, li, glu_b, g_m,
                             norm_ffn_g, sh_f, sc_f, router_w, router_b)
    route = route.reshape(bsz * seq, LANES)
    src, dst, tile_expert = _routing_tables(route, MOE_TILE)
    y_pairs = _moe_experts(h2.reshape(bsz * seq, d), src, dst, tile_expert,
                           w_gate, w_up, w_down, li)
    return _moe_combine(y_pairs, route, x1, g_f, final_g, final)


def kernel(x, c, positions, mod_w, mod_b, norm_mix_g, norm_ffn_g, ev_w_in, ev_conv_w, ev_w_out, ffn_w_gate, ffn_w_up, ffn_w_down, ssm_a_re, ssm_a_im, ssm_log_step, ssm_b_re, ssm_b_im, ssm_c_re, ssm_c_im, ssm_d, glu_w, glu_b, moe_router_w, moe_router_b, moe_w_gate, moe_w_up, moe_w_down, final_norm_g):
    depth = mod_w.shape[0]
    d = x.shape[2]
    assert depth % 2 == 0
    mod = _modulation(c, mod_w, mod_b)
    cos, sin = _rope_tables(positions)
    layer_mods = [[mod[layer, :, j * d:(j + 1) * d] for j in range(6)] for layer in range(depth)]
    bf = lambda w: w.astype(BF16)
    ev_w_in, ev_w_out, glu_w = bf(ev_w_in), bf(ev_w_out), bf(glu_w)
    ffn_w_gate, ffn_w_up, ffn_w_down = bf(ffn_w_gate), bf(ffn_w_up), bf(ffn_w_down)
    moe_w_gate, moe_w_up, moe_w_down = bf(moe_w_gate), bf(moe_w_up), bf(moe_w_down)
    h = None
    for layer in range(depth):
        mods = layer_mods[layer]
        i = layer // 2
        if layer % 2 == 0:
            nxt = layer_mods[layer + 1]
            x, h = _even_layer(x, mods, norm_mix_g[layer], norm_ffn_g[layer], cos, sin, i,
                               ev_w_in, ev_conv_w[i], ev_w_out,
                               ffn_w_gate, ffn_w_up, ffn_w_down,
                               (norm_mix_g[layer + 1], nxt[0], nxt[1]))
        else:
            ssm = (ssm_a_re[i], ssm_a_im[i], ssm_log_step[i], ssm_b_re[i], ssm_b_im[i],
                   ssm_c_re[i], ssm_c_im[i])
            x = _odd_layer(x, h, mods, norm_ffn_g[layer], ssm, ssm_d[i], i,
                           glu_w, glu_b[i], moe_router_w[i], moe_router_b[i],
                           moe_w_gate, moe_w_up, moe_w_down,
                           final_norm_g, layer == depth - 1)
    return x
```

```python
import functools
import math

import jax
import jax.numpy as jnp
from jax import lax
from jax.experimental import pallas as pl
from jax.experimental.pallas import tpu as pltpu

F32 = jnp.float32
BF16 = jnp.bfloat16

ATTN_HEADS = 8
HEAD_DIM = 64
ATTN_WIDTH = ATTN_HEADS * HEAD_DIM
ROPE_DIM = HEAD_DIM // 4
ROPE_THETA = 500000.0
DILATED_PAIRS = ((128, 1), (512, 4), (2048, 16))
ATTN_BLOCK = 128
SSM_GROUP = 16
SSM_STATE = 64
SSM_CHUNK = 16
N_EXPERTS = 8
RMS_EPS = 1e-6

LANES = 128
VMEM_LIMIT_BYTES = 56 * 1024 * 1024

ROW_TILE = 512
MOE_TILE = 1024
MOE_FF_CHUNKS = 7
ATTN_GROUP = 4
SSM_CHUNKS_PER_STEP = 64
SSM_LANE_GROUPS = LANES // SSM_GROUP


def _params(n_axes, vmem=VMEM_LIMIT_BYTES):
    return pltpu.CompilerParams(
        dimension_semantics=("arbitrary",) * n_axes, vmem_limit_bytes=vmem)


def _dot(a, b):
    return jnp.dot(a, b, preferred_element_type=F32)


def _sigmoid(x):
    return 1.0 / (1.0 + jnp.exp(-x))


def _split_bf16(x):
    hi = x.astype(BF16)
    lo = (x - hi.astype(F32)).astype(BF16)
    return hi, lo


def _ada_norm(x, g, shift, scale):
    ms = jnp.mean(x * x, axis=-1, keepdims=True)
    return x * lax.rsqrt(ms + RMS_EPS) * g * (1.0 + scale) + shift


def _mod_kernel(c_ref, w_ref, b_ref, o_ref):
    c = c_ref[...]
    cond = c * _sigmoid(c)
    ch, cl = _split_bf16(cond)
    wh, wl = _split_bf16(w_ref[...])
    o_ref[...] = _dot(ch, wh) + _dot(cl, wh) + _dot(ch, wl) + b_ref[...]


def _modulation(c, mod_w, mod_b):
    depth, d, n = mod_w.shape
    bsz = c.shape[0]
    tn = 1024
    return pl.pallas_call(
        _mod_kernel,
        grid=(depth, n // tn),
        in_specs=[pl.BlockSpec((bsz, d), lambda l, j: (0, 0)),
                  pl.BlockSpec((None, d, tn), lambda l, j: (l, 0, j)),
                  pl.BlockSpec((None, 1, tn), lambda l, j: (l, 0, j))],
        out_specs=pl.BlockSpec((None, bsz, tn), lambda l, j: (l, 0, j)),
        out_shape=jax.ShapeDtypeStruct((depth, bsz, n), F32),
        compiler_params=_params(2),
        name="modulation",
    )(c, mod_w, mod_b.reshape(depth, 1, n))


def _rope_kernel(pos_ref, inv_ref, cos_ref, sin_ref):
    ang = pos_ref[...].astype(F32) * inv_ref[...]
    cos_ref[...] = jnp.cos(ang)
    sin_ref[...] = jnp.sin(ang)


def _rope_tables(positions):
    bsz, seq = positions.shape
    inv = ROPE_THETA ** (-jnp.arange(0, ROPE_DIM, 2, dtype=F32) / ROPE_DIM)
    lane = jnp.arange(LANES) % HEAD_DIM
    inv_lane = jnp.where(lane < ROPE_DIM, inv[lane % (ROPE_DIM // 2)], 0.0).reshape(1, LANES)
    tm = ROW_TILE
    spec = pl.BlockSpec((None, tm, LANES), lambda b, i: (b, i, 0))
    return pl.pallas_call(
        _rope_kernel,
        grid=(bsz, seq // tm),
        in_specs=[pl.BlockSpec((None, tm, 1), lambda b, i: (b, i, 0)),
                  pl.BlockSpec((1, LANES), lambda b, i: (0, 0))],
        out_specs=[spec, spec],
        out_shape=[jax.ShapeDtypeStruct((bsz, seq, LANES), F32)] * 2,
        compiler_params=_params(2),
        name="rope_tables",
    )(positions.reshape(bsz, seq, 1), inv_lane)


def _inproj_kernel(x_ref, g_ref, sh_ref, sc_ref, w_ref, cos_ref, sin_ref, cw_ref, *rest):
    n_pat = len(DILATED_PAIRS)
    qkv_refs = [rest[3 * p:3 * p + 3] for p in range(n_pat)]
    bo_ref, ubuf, stage = rest[3 * n_pat:]
    tm = x_ref.shape[0]
    aw = ATTN_WIDTH
    cwid = cw_ref.shape[1]
    h = _ada_norm(x_ref[...], g_ref[...], sh_ref[...], sc_ref[...]).astype(BF16)
    proj = _dot(h, w_ref[...])

    cos = cos_ref[...]
    sin = sin_ref[...]
    lane = lax.broadcasted_iota(jnp.int32, (tm, LANES), 1) % HEAD_DIM
    first_half = lane < ROPE_DIM // 2

    def rope(t):
        rot = jnp.where(first_half,
                        -pltpu.roll(t, LANES - ROPE_DIM // 2, 1),
                        pltpu.roll(t, ROPE_DIM // 2, 1))
        return t * cos + rot * sin

    n_col = aw // LANES
    for j in range(n_col):
        sl = slice(j * LANES, (j + 1) * LANES)
        stage[0, j] = rope(proj[:, sl]) * (HEAD_DIM ** -0.5)
        stage[1, j] = rope(proj[:, aw + j * LANES: aw + (j + 1) * LANES])
        stage[2, j] = proj[:, 2 * aw + j * LANES: 2 * aw + (j + 1) * LANES]
    for (_, dil), refs in zip(DILATED_PAIRS, qkv_refs):
        for which, ref in enumerate(refs):
            for j in range(n_col):
                sl = slice(j * LANES, (j + 1) * LANES)
                if dil == 1:
                    ref[:, sl] = stage[which, j].astype(BF16)
                else:
                    for r in range(dil):
                        ref[r, :, sl] = stage[which, j, pl.ds(r, tm // dil, stride=dil), :].astype(BF16)

    b_gate = proj[:, 3 * aw:3 * aw + cwid]
    c_gate = proj[:, 3 * aw + cwid:3 * aw + 2 * cwid]
    xin = proj[:, 3 * aw + 2 * cwid:]
    u = c_gate * xin

    @pl.when(pl.program_id(1) == 0)
    def _():
        ubuf[0:8, :] = jnp.zeros((8, cwid), F32)

    ubuf[8:, :] = u
    conv = (cw_ref[0:1, :] * ubuf[6:6 + tm, :] + cw_ref[1:2, :] * ubuf[7:7 + tm, :]
            + cw_ref[2:3, :] * u)
    bo_ref[...] = (b_gate * conv).astype(BF16)
    ubuf[0:8, :] = ubuf[tm:tm + 8, :]


def _even_inproj(x, g, shift, scale, w_in, li, cos, sin, conv_w):
    bsz, seq, d = x.shape
    n = w_in.shape[2]
    cwid = conv_w.shape[1]
    tm = ROW_TILE
    aw = ATTN_WIDTH
    row = lambda w: pl.BlockSpec((None, tm, w), lambda b, i: (b, i, 0))
    per_b = pl.BlockSpec((None, 1, d), lambda b, i: (b, 0, 0))
    qkv_specs, qkv_shapes = [], []
    for window, dil in DILATED_PAIRS:
        assert window // dil == ATTN_BLOCK and seq % window == 0 and (dil == 1 or window % tm == 0)
        if dil == 1:
            spec, shape = row(aw), (bsz, seq, aw)
        else:
            per_span = window // tm
            spec = pl.BlockSpec((None, None, dil, tm // dil, aw),
                                lambda b, i, per_span=per_span: (b, i // per_span, 0, i % per_span, 0))
            shape = (bsz, seq // window, dil, ATTN_BLOCK, aw)
        qkv_specs += [spec] * 3
        qkv_shapes += [jax.ShapeDtypeStruct(shape, BF16)] * 3
    outs = pl.pallas_call(
        _inproj_kernel,
        grid=(bsz, seq // tm),
        in_specs=[row(d), pl.BlockSpec((1, d), lambda b, i: (0, 0)), per_b, per_b,
                  pl.BlockSpec((None, d, n), lambda b, i: (li, 0, 0)),
                  row(LANES), row(LANES),
                  pl.BlockSpec(conv_w.shape, lambda b, i: (0, 0))],
        out_specs=qkv_specs + [row(cwid)],
        out_shape=qkv_shapes + [jax.ShapeDtypeStruct((bsz, seq, cwid), BF16)],
        scratch_shapes=[pltpu.VMEM((tm + 8, cwid), F32),
                        pltpu.VMEM((3, aw // LANES, tm, LANES), F32)],
        compiler_params=_params(2),
        name="even_inproj",
    )(x, g.reshape(1, d), shift.reshape(bsz, 1, d), scale.reshape(bsz, 1, d),
      w_in, cos, sin, conv_w)
    qkv = [outs[3 * p:3 * p + 3] for p in range(len(DILATED_PAIRS))]
    return qkv, outs[-1]


def _attn_item(q, kk, vv, valid):
    blk = ATTN_BLOCK
    lane = lax.broadcasted_iota(jnp.int32, (blk, LANES), 1)
    lane_kv = lax.broadcasted_iota(jnp.int32, (2 * blk, LANES), 1)
    neg_inf = jnp.float32(-jnp.inf)
    ml_new = jnp.zeros((blk, LANES), F32)
    n_col, per_col = ATTN_WIDTH // LANES, LANES // HEAD_DIM
    scores = []
    for j in range(n_col):
        sl = slice(j * LANES, (j + 1) * LANES)
        for hh in range(per_col):
            in_head = (lane // HEAD_DIM) == hh
            qm = jnp.where(in_head, q[:, sl], jnp.zeros_like(q[:, sl]))
            s = lax.dot_general(qm, kk[:, sl], (((1,), (1,)), ((), ())),
                                preferred_element_type=F32)
            scores.append(jnp.where(valid, s, neg_inf))
    accs = []
    for j in range(n_col):
        sl = slice(j * LANES, (j + 1) * LANES)
        vj = vv[:, sl]
        pv = []
        for hh in range(per_col):
            head = j * per_col + hh
            in_head_kv = (lane_kv // HEAD_DIM) == hh
            s = scores[head]
            m_new = jnp.max(s, axis=1, keepdims=True)
            p = jnp.exp(s - m_new)
            l_new = jnp.sum(p, axis=1, keepdims=True)
            vm = jnp.where(in_head_kv, vj, jnp.zeros_like(vj))
            pv.append(_dot(p.astype(BF16), vm))
            ml_new = jnp.where(lane == head, m_new, ml_new)
            ml_new = jnp.where(lane == ATTN_HEADS + head, l_new, ml_new)
        accs.append(pv[0] + pv[1])
    return accs, ml_new


def _head_columns(tile, first_lane, j):
    lane = lax.broadcasted_iota(jnp.int32, tile.shape, 1)
    h0 = first_lane + j * (LANES // HEAD_DIM)
    return jnp.where(lane < HEAD_DIM, tile[:, h0:h0 + 1], tile[:, h0 + 1:h0 + 2])


def _attn_merge(acc_old, ml_old, acc_loc, ml_loc):
    lane = lax.broadcasted_iota(jnp.int32, ml_old.shape, 1)
    m_new = jnp.maximum(ml_old, ml_loc)
    a_old = jnp.exp(ml_old - m_new)
    a_loc = jnp.exp(ml_loc - m_new)
    l_new = (pltpu.roll(a_old, ATTN_HEADS, 1) * ml_old
             + pltpu.roll(a_loc, ATTN_HEADS, 1) * ml_loc)
    ml_new = jnp.where(lane < ATTN_HEADS, m_new, jnp.where(lane < 2 * ATTN_HEADS, l_new, 0.0))
    accs = [_head_columns(a_old, 0, j) * acc_old[j] + _head_columns(a_loc, 0, j) * acc_loc[j]
            for j in range(len(acc_loc))]
    return accs, ml_new


def _band_mask(has_prev):
    blk = ATTN_BLOCK
    row = lax.broadcasted_iota(jnp.int32, (blk, 2 * blk), 0)
    col = lax.broadcasted_iota(jnp.int32, (blk, 2 * blk), 1)
    band = (col >= row) & (col <= row + blk)
    return band, band & ((col >= blk) | has_prev)


def _attn_wide_kernel(q_ref, kc_ref, vc_ref, kp_ref, vp_ref, acc_out, ml_out):
    _, valid = _band_mask(pl.program_id(1) > 0)
    for r in range(q_ref.shape[0]):
        kk = jnp.concatenate([kp_ref[r], kc_ref[r]], axis=0)
        vv = jnp.concatenate([vp_ref[r], vc_ref[r]], axis=0)
        accs, ml_new = _attn_item(q_ref[r], kk, vv, valid)
        for j, a in enumerate(accs):
            acc_out[r, :, j * LANES:(j + 1) * LANES] = a
        ml_out[r] = ml_new


def _attn_mid_kernel(q_ref, kc_ref, vc_ref, kp_ref, vp_ref, accw_ref, mlw_ref, acc_out, ml_out,
                     old_acc, old_ml, loc_acc, loc_ml):
    dil = q_ref.shape[0]
    ratio = accw_ref.shape[0] // dil
    sub = accw_ref.shape[1]
    n_col = ATTN_WIDTH // LANES
    blk = ATTN_BLOCK
    _, valid = _band_mask(pl.program_id(1) > 0)
    for r in range(dil):
        for qd in range(ratio):
            rows = pl.ds(r * blk + qd, sub, stride=ratio)
            for j in range(n_col):
                old_acc[j, rows, :] = accw_ref[dil * qd + r, :, j * LANES:(j + 1) * LANES]
            old_ml[rows, :] = mlw_ref[dil * qd + r]
        kk = jnp.concatenate([kp_ref[r], kc_ref[r]], axis=0)
        vv = jnp.concatenate([vp_ref[r], vc_ref[r]], axis=0)
        accs, ml = _attn_item(q_ref[r], kk, vv, valid)
        for j, a in enumerate(accs):
            loc_acc[j, r * blk:(r + 1) * blk, :] = a
        loc_ml[r * blk:(r + 1) * blk, :] = ml
    accs, ml_new = _attn_merge([old_acc[j] for j in range(n_col)], old_ml[...],
                               [loc_acc[j] for j in range(n_col)], loc_ml[...])
    for r in range(dil):
        rows = pl.ds(r, blk, stride=dil)
        for j, a in enumerate(accs):
            acc_out[j, rows, :] = a[r * blk:(r + 1) * blk]
        ml_out[rows, :] = ml_new[r * blk:(r + 1) * blk]


def _attn_last_kernel(q_ref, kc_ref, vc_ref, kp_ref, vp_ref, acc_in, ml_in,
                      b_ref, x_ref, wa_ref, wb_ref, gm_ref, g_ref, sh_ref, sc_ref,
                      x1_ref, h_ref, loc_acc, loc_ml):
    gb = q_ref.shape[0]
    n_col = ATTN_WIDTH // LANES
    blk = ATTN_BLOCK
    band, valid0 = _band_mask(pl.program_id(1) > 0)
    for i in range(gb):
        if i == 0:
            k_prev, v_prev, valid = kp_ref[...], vp_ref[...], valid0
        else:
            k_prev, v_prev, valid = kc_ref[i - 1], vc_ref[i - 1], band
        kk = jnp.concatenate([k_prev, kc_ref[i]], axis=0)
        vv = jnp.concatenate([v_prev, vc_ref[i]], axis=0)
        accs, ml = _attn_item(q_ref[i], kk, vv, valid)
        for j, a in enumerate(accs):
            loc_acc[j, i * blk:(i + 1) * blk, :] = a
        loc_ml[i * blk:(i + 1) * blk, :] = ml
    ml_old = ml_in[...].reshape(gb * blk, LANES)
    accs, ml_new = _attn_merge([acc_in[j] for j in range(n_col)], ml_old,
                               [loc_acc[j] for j in range(n_col)], loc_ml[...])
    lane = lax.broadcasted_iota(jnp.int32, ml_new.shape, 1)
    is_l = (lane >= ATTN_HEADS) & (lane < 2 * ATTN_HEADS)
    linv = 1.0 / jnp.where(is_l, ml_new, 1.0)
    attn = jnp.concatenate([(a * _head_columns(linv, ATTN_HEADS, j)).astype(BF16)
                            for j, a in enumerate(accs)], axis=1)
    conv = b_ref[...].reshape(gb * blk, b_ref.shape[2])
    mix = _dot(attn, wa_ref[...]) + _dot(conv, wb_ref[...])
    x1 = x_ref[...] + gm_ref[...] * mix
    x1_ref[...] = x1
    h_ref[...] = _ada_norm(x1, g_ref[...], sh_ref[...], sc_ref[...]).astype(BF16)


def _attention_outproj(qkv, b_out, x, w_out, li, gate_m, g, shift, scale):
    (_, d1), (_, dm), (_, dw) = DILATED_PAIRS
    assert d1 == 1 and dw % dm == 0
    blk, aw = ATTN_BLOCK, ATTN_WIDTH
    (q1, k1, v1), (qm, km, vm), (qw, kw, vw) = qkv
    bsz, seq, _ = q1.shape

    nw = qw.shape[1]
    rpg = min(ATTN_GROUP, dw)
    cur = lambda w: pl.BlockSpec((None, None, rpg, blk, w), lambda b, n, g: (b, n, g, 0, 0))
    prev = pl.BlockSpec((None, None, rpg, blk, aw),
                        lambda b, n, g: (b, jnp.maximum(n - 1, 0), g, 0, 0))
    acc_w, ml_w = pl.pallas_call(
        _attn_wide_kernel,
        grid=(bsz, nw, dw // rpg),
        in_specs=[cur(aw), cur(aw), cur(aw), prev, prev],
        out_specs=[cur(aw), cur(LANES)],
        out_shape=[jax.ShapeDtypeStruct((bsz, nw, dw, blk, aw), F32),
                   jax.ShapeDtypeStruct((bsz, nw, dw, blk, LANES), F32)],
        compiler_params=_params(3),
        name=f"attn_dil{dw}",
    )(qw, kw, vw, kw, vw)

    nm = qm.shape[1]
    ratio = dw // dm
    sub = blk // ratio
    cur = pl.BlockSpec((None, None, dm, blk, aw), lambda b, n: (b, n, 0, 0, 0))
    prev = pl.BlockSpec((None, None, dm, blk, aw),
                        lambda b, n: (b, jnp.maximum(n - 1, 0), 0, 0, 0))
    wide = lambda w: pl.BlockSpec((None, None, dw, sub, w),
                                  lambda b, n: (b, n // ratio, 0, n % ratio, 0))
    n_col = aw // LANES
    acc, ml = pl.pallas_call(
        _attn_mid_kernel,
        grid=(bsz, nm),
        in_specs=[cur, cur, cur, prev, prev, wide(aw), wide(LANES)],
        out_specs=[pl.BlockSpec((None, n_col, dm * blk, LANES), lambda b, n: (b, 0, n, 0)),
                   pl.BlockSpec((None, dm * blk, LANES), lambda b, n: (b, n, 0))],
        out_shape=[jax.ShapeDtypeStruct((bsz, n_col, seq, LANES), F32),
                   jax.ShapeDtypeStruct((bsz, seq, LANES), F32)],
        scratch_shapes=[pltpu.VMEM((n_col, dm * blk, LANES), F32), pltpu.VMEM((dm * blk, LANES), F32),
                        pltpu.VMEM((n_col, dm * blk, LANES), F32), pltpu.VMEM((dm * blk, LANES), F32)],
        compiler_params=_params(2),
        name=f"attn_dil{dm}",
    )(qm, km, vm, km, vm, acc_w, ml_w)

    nb = seq // blk
    gb = min(ATTN_GROUP, nb)
    view = lambda t: t.reshape(bsz, nb, blk, t.shape[-1])
    cur = lambda w: pl.BlockSpec((None, gb, blk, w), lambda b, n: (b, n, 0, 0))
    prev = pl.BlockSpec((None, None, blk, aw), lambda b, n: (b, jnp.maximum(n * gb - 1, 0), 0, 0))
    d = x.shape[2]
    bw = b_out.shape[2]
    assert w_out.shape[1] == aw + bw and bw == aw
    rows = pl.BlockSpec((None, gb * blk, d), lambda b, n: (b, n, 0))
    per_b = pl.BlockSpec((None, 1, d), lambda b, n: (b, 0, 0))
    pb = lambda t: t.reshape(bsz, 1, d)
    return pl.pallas_call(
        _attn_last_kernel,
        grid=(bsz, nb // gb),
        in_specs=[cur(aw), cur(aw), cur(aw), prev, prev,
                  pl.BlockSpec((None, n_col, gb * blk, LANES), lambda b, n: (b, 0, n, 0)),
                  cur(LANES), cur(bw), rows,
                  pl.BlockSpec((None, aw, d), lambda b, n: (li, 0, 0)),
                  pl.BlockSpec((None, bw, d), lambda b, n: (li, 1, 0)),
                  per_b, pl.BlockSpec((1, d), lambda b, n: (0, 0)), per_b, per_b],
        out_specs=[rows, rows],
        out_shape=[jax.ShapeDtypeStruct((bsz, seq, d), F32),
                   jax.ShapeDtypeStruct((bsz, seq, d), BF16)],
        scratch_shapes=[pltpu.VMEM((n_col, gb * blk, LANES), F32),
                        pltpu.VMEM((gb * blk, LANES), F32)],
        compiler_params=_params(2),
        name=f"attn_dil{d1}_outproj",
    )(view(q1), view(k1), view(v1), view(k1), view(v1), acc, view(ml), view(b_out), x,
      w_out, w_out, pb(gate_m), g.reshape(1, d), pb(shift), pb(scale))


def _ffn_kernel(h_ref, x_ref, wg_ref, wu_ref, wd_ref, gf_ref, g_ref, sh_ref, sc_ref,
                o_ref, hn_ref):
    h = h_ref[...]
    g = _dot(h, wg_ref[...])
    u = _dot(h, wu_ref[...])
    act = (g * _sigmoid(g) * u).astype(BF16)
    x2 = x_ref[...] + gf_ref[...] * _dot(act, wd_ref[...])
    o_ref[...] = x2
    hn_ref[...] = _ada_norm(x2, g_ref[...], sh_ref[...], sc_ref[...]).astype(BF16)


def _dense_ffn(h, x, w_gate, w_up, w_down, li, gate_f, next_g, next_shift, next_scale):
    bsz, seq, d = x.shape
    ff = w_gate.shape[2]
    tm = ROW_TILE
    row = pl.BlockSpec((None, tm, d), lambda b, i: (b, i, 0))
    per_b = pl.BlockSpec((None, 1, d), lambda b, i: (b, 0, 0))
    full = lambda s: pl.BlockSpec((None,) + s, lambda b, i: (li, 0, 0),
                                  pipeline_mode=pl.Buffered(1))
    pb = lambda t: t.reshape(bsz, 1, d)
    return pl.pallas_call(
        _ffn_kernel,
        grid=(bsz, seq // tm),
        in_specs=[row, row, full((d, ff)), full((d, ff)), full((ff, d)), per_b,
                  pl.BlockSpec((1, d), lambda b, i: (0, 0)), per_b, per_b],
        out_specs=[row, row],
        out_shape=[jax.ShapeDtypeStruct((bsz, seq, d), F32),
                   jax.ShapeDtypeStruct((bsz, seq, d), BF16)],
        compiler_params=_params(2),
        name="dense_ffn",
    )(h, x, w_gate, w_up, w_down, pb(gate_f), next_g.reshape(1, d), pb(next_shift),
      pb(next_scale))


def _ssm_operators(a_re, a_im, log_step, b_re, b_im, c_re, c_im):
    L = SSM_CHUNK
    a_re, a_im = a_re.astype(F32), a_im.astype(F32)
    dt = jnp.exp(log_step.astype(F32))[:, None]
    mag = jnp.exp(a_re * dt)
    abar_re = mag * jnp.cos(a_im * dt)
    abar_im = mag * jnp.sin(a_im * dt)
    den = a_re * a_re + a_im * a_im
    nr = abar_re - 1.0
    f_re = (nr * a_re + abar_im * a_im) / den
    f_im = (abar_im * a_re - nr * a_im) / den
    b_re, b_im = b_re.astype(F32), b_im.astype(F32)
    bb_re = f_re[..., None] * b_re - f_im[..., None] * b_im
    bb_im = f_re[..., None] * b_im + f_im[..., None] * b_re
    c_re, c_im = c_re.astype(F32), c_im.astype(F32)

    def step(carry, _):
        pr, pi = carry
        nxt = (pr * abar_re - pi * abar_im, pr * abar_im + pi * abar_re)
        return nxt, carry
    (pl_re, pl_im), (pw_re, pw_im) = lax.scan(
        step, (jnp.ones_like(abar_re), jnp.zeros_like(abar_re)), None, length=L)
    pw_re = jnp.concatenate([pw_re, pl_re[None]], axis=0)
    pw_im = jnp.concatenate([pw_im, pl_im[None]], axis=0)

    hi = lax.Precision.HIGHEST
    cp_re = c_re[None] * pw_re[:L, :, None, :] - c_im[None] * pw_im[:L, :, None, :]
    cp_im = c_re[None] * pw_im[:L, :, None, :] + c_im[None] * pw_re[:L, :, None, :]
    w = (jnp.einsum('tgcp,gpd->tgdc', cp_re, bb_re, precision=hi)
         - jnp.einsum('tgcp,gpd->tgdc', cp_im, bb_im, precision=hi))
    s_idx = jnp.arange(L)[:, None]
    t_idx = jnp.arange(L)[None, :]
    tau = t_idx - s_idx
    toep = jnp.where((tau >= 0)[:, :, None, None, None], w[jnp.clip(tau, 0, L - 1)], 0.0)
    n_g, n_c = a_re.shape[0], b_re.shape[2]
    toep = toep.transpose(2, 0, 3, 1, 4).reshape(n_g, L * n_c, L * n_c)

    rp_re, rp_im = pw_re[:L][::-1], pw_im[:L][::-1]
    so_re = rp_re[..., None] * bb_re[None] - rp_im[..., None] * bb_im[None]
    so_im = rp_re[..., None] * bb_im[None] + rp_im[..., None] * bb_re[None]
    to_rows = lambda t: t.transpose(1, 0, 3, 2).reshape(n_g, L * n_c, -1)
    s_out = jnp.concatenate([to_rows(so_re), to_rows(so_im)], axis=-1)
    s_out_sw = jnp.concatenate([to_rows(so_im), to_rows(so_re)], axis=-1)

    qp_re, qp_im = pw_re[1:], pw_im[1:]
    ci_re = c_re[None] * qp_re[:, :, None, :] - c_im[None] * qp_im[:, :, None, :]
    ci_im = c_re[None] * qp_im[:, :, None, :] + c_im[None] * qp_re[:, :, None, :]
    to_cols = lambda t: t.transpose(1, 3, 0, 2).reshape(n_g, -1, L * n_c)
    c_in = jnp.concatenate([to_cols(ci_re), -to_cols(ci_im)], axis=1)

    al_re, al_im = pw_re[L], pw_im[L]
    a1 = jnp.concatenate([al_re, al_re], axis=-1)[:, None, :]
    a2 = jnp.concatenate([-al_im, al_im], axis=-1)[:, None, :]
    return (toep.astype(BF16), s_out.astype(BF16), s_out_sw.astype(BF16),
            c_in.astype(BF16), a1, a2)


def _piece_transpose(arrs):
    n = len(arrs)
    piece = lax.broadcasted_iota(jnp.int32, (1, LANES), 1) // SSM_GROUP
    arrs = list(arrs)
    dist = n // 2
    while dist >= 1:
        keep = (piece & dist) == 0
        for i in range(n):
            if i & dist == 0:
                a, b = arrs[i], arrs[i + dist]
                arrs[i] = jnp.where(keep, a, pltpu.roll(b, dist * SSM_GROUP, 1))
                arrs[i + dist] = jnp.where(keep, pltpu.roll(a, LANES - dist * SSM_GROUP, 1), b)
        dist //= 2
    return arrs


def _ssm_kernel(u_ref, t_ref, s_ref, ssw_ref, cin_ref, a1_ref, a2_ref, y_ref,
                st_ref, stsw_ref, xs_ref, xc_ref, zc_ref):
    kt, L, bsz, _ = u_ref.shape
    rows = kt * bsz
    n_lg = SSM_LANE_GROUPS

    @pl.when(pl.program_id(1) == 0)
    def _():
        xc_ref[...] = jnp.zeros(xc_ref.shape, F32)
        zc_ref[...] = jnp.zeros(zc_ref.shape, F32)

    zs = [u_ref[:, s].reshape(rows, LANES) for s in range(L)]
    lo = _piece_transpose(zs[:n_lg])
    hi = _piece_transpose(zs[n_lg:])
    ys = []
    for gl in range(n_lg):
        v = jnp.concatenate([lo[gl], hi[gl]], axis=1)
        st_ref[...] = _dot(v, s_ref[gl])
        stsw_ref[...] = _dot(v, ssw_ref[gl])
        a1 = jnp.broadcast_to(a1_ref[gl], (bsz, 2 * SSM_STATE))
        a2 = jnp.broadcast_to(a2_ref[gl], (bsz, 2 * SSM_STATE))

        def body(k, carry):
            x, z = carry
            r = pl.ds(pl.multiple_of(k * bsz, bsz), bsz)
            xs_ref[r, :] = x
            x_new = x * a1 + z * a2 + st_ref[r, :]
            z_new = z * a1 - x * a2 + stsw_ref[r, :]
            return x_new, z_new

        x, z = lax.fori_loop(0, kt, body, (xc_ref[gl], zc_ref[gl]), unroll=8)
        xc_ref[gl] = x
        zc_ref[gl] = z
        ys.append((_dot(v, t_ref[gl]) + _dot(xs_ref[...].astype(BF16), cin_ref[gl])).astype(BF16))
    out_lo = _piece_transpose([y[:, :LANES] for y in ys])
    out_hi = _piece_transpose([y[:, LANES:] for y in ys])
    for t in range(n_lg):
        y_ref[:, t] = out_lo[t].reshape(kt, bsz, LANES)
        y_ref[:, n_lg + t] = out_hi[t].reshape(kt, bsz, LANES)


def _ssm_scan(h, ops):
    bsz, seq, d = h.shape
    toep, s_out, s_out_sw, c_in, a1, a2 = ops
    L, C, n_lg = SSM_CHUNK, SSM_GROUP, SSM_LANE_GROUPS
    assert L == 2 * n_lg
    n_k = seq // L
    kt = min(SSM_CHUNKS_PER_STEP, n_k)
    hv = h.transpose(1, 0, 2).reshape(n_k, L, bsz, d)
    act = pl.BlockSpec((kt, L, bsz, LANES), lambda o, k: (k, 0, 0, o))
    wblk = lambda s: pl.BlockSpec((n_lg,) + s, lambda o, k: (o, 0, 0))
    y = pl.pallas_call(
        _ssm_kernel,
        grid=(d // LANES, n_k // kt),
        in_specs=[act, wblk((L * C, L * C)), wblk((L * C, 2 * SSM_STATE)),
                  wblk((L * C, 2 * SSM_STATE)), wblk((2 * SSM_STATE, L * C)),
                  wblk((1, 2 * SSM_STATE)), wblk((1, 2 * SSM_STATE))],
        out_specs=act,
        out_shape=jax.ShapeDtypeStruct((n_k, L, bsz, d), BF16),
        scratch_shapes=[pltpu.VMEM((kt * bsz, 2 * SSM_STATE), F32)] * 3
        + [pltpu.VMEM((n_lg, bsz, 2 * SSM_STATE), F32)] * 2,
        compiler_params=_params(2),
        name="ssm_scan",
    )(hv, toep, s_out, s_out_sw, c_in, a1, a2)
    return y.reshape(seq, bsz, d).transpose(1, 0, 2)


def _gelu_tanh(x):
    return 0.5 * x * (1.0 + jnp.tanh(math.sqrt(2.0 / math.pi) * (x + 0.044715 * (x * x * x))))


def _ssm_out_kernel(y_ref, h_ref, x_ref, d_ref, w_ref, b_ref, gm_ref, g_ref, sh_ref, sc_ref,
                    rw_ref, rb_ref, x1_ref, h2_ref, route_ref):
    d = x_ref.shape[1]
    y = y_ref[...].astype(F32) + d_ref[...] * h_ref[...].astype(F32)
    z = _dot(_gelu_tanh(y).astype(BF16), w_ref[...]) + b_ref[...]
    mix = z[:, :d] * _sigmoid(z[:, d:])
    x1 = x_ref[...] + gm_ref[...] * mix
    x1_ref[...] = x1
    h2 = _ada_norm(x1, g_ref[...], sh_ref[...], sc_ref[...])
    h2_ref[...] = h2

    hh, hl = _split_bf16(h2)
    both = _dot(hh, rw_ref[...])
    logits = (both[:, :LANES] + both[:, LANES:] + _dot(hl, rw_ref[:, :LANES]) + rb_ref[...])
    lane = lax.broadcasted_iota(jnp.int32, logits.shape, 1)
    lane_f = lane.astype(F32)
    neg_inf = jnp.float32(-jnp.inf)
    logits = jnp.where(lane < N_EXPERTS, logits, neg_inf)
    m1 = jnp.max(logits, axis=1, keepdims=True)
    i1 = jnp.min(jnp.where(logits == m1, lane_f, float(LANES)), axis=1, keepdims=True)
    rest = jnp.where(lane_f == i1, neg_inf, logits)
    m2 = jnp.max(rest, axis=1, keepdims=True)
    i2 = jnp.min(jnp.where(rest == m2, lane_f, float(LANES)), axis=1, keepdims=True)
    e2 = jnp.exp(m2 - m1)
    g1 = 1.0 / (1.0 + e2)
    g2 = e2 / (1.0 + e2)
    route = jnp.where(lane == 0, i1, jnp.where(lane == 1, i2,
                      jnp.where(lane == 2, g1, jnp.where(lane == 3, g2, 0.0))))
    route_ref[...] = route


def _ssm_out(y, h, x, d_skip, glu_w, li, glu_b, gate_m, g, shift, scale, router_w, router_b):
    bsz, seq, d = x.shape
    tm = ROW_TILE
    row = lambda w: pl.BlockSpec((None, tm, w), lambda b, i: (b, i, 0))
    per_b = pl.BlockSpec((None, 1, d), lambda b, i: (b, 0, 0))
    full = lambda s: pl.BlockSpec(s, lambda b, i: (0, 0))
    pb = lambda t: t.reshape(bsz, 1, d)
    rw = jnp.zeros((d, LANES), F32).at[:, :N_EXPERTS].set(router_w)
    rwh = rw.astype(BF16)
    rw_cat = jnp.concatenate([rwh, (rw - rwh.astype(F32)).astype(BF16)], axis=1)
    rb = jnp.zeros((1, LANES), F32).at[0, :N_EXPERTS].set(router_b)
    return pl.pallas_call(
        _ssm_out_kernel,
        grid=(bsz, seq // tm),
        in_specs=[row(d), row(d), row(d), full((1, d)),
                  pl.BlockSpec((None, d, 2 * d), lambda b, i: (li, 0, 0)), full((1, 2 * d)),
                  per_b, full((1, d)), per_b, per_b,
                  full((d, 2 * LANES)), full((1, LANES))],
        out_specs=[row(d), row(d), row(LANES)],
        out_shape=[jax.ShapeDtypeStruct((bsz, seq, d), F32),
                   jax.ShapeDtypeStruct((bsz, seq, d), F32),
                   jax.ShapeDtypeStruct((bsz, seq, LANES), F32)],
        compiler_params=_params(2),
        name="ssm_out",
    )(y, h, x, d_skip.reshape(1, d), glu_w, glu_b.reshape(1, 2 * d), pb(gate_m),
      g.reshape(1, d), pb(shift), pb(scale), rw_cat, rb)


def _routing_tables(route, tm):
    n = route.shape[0]
    i32 = jnp.int32
    e_flat = route[:, :2].astype(i32).T.reshape(-1)
    counts = jnp.sum((e_flat[:, None] == jnp.arange(N_EXPERTS)[None, :]).astype(i32), axis=0)
    order = jnp.sort(e_flat * (2 * n) + jnp.arange(2 * n, dtype=i32)) % (2 * n)
    padded = ((counts + tm - 1) // tm) * tm
    ends = jnp.cumsum(padded)
    starts = ends - padded
    n_tiles = (2 * n) // tm + N_EXPERTS
    n_rows = n_tiles * tm
    s = jnp.arange(n_rows, dtype=i32)
    e_of_s = jnp.sum((s[:, None] >= ends[None, :]).astype(i32), axis=1)
    starts9 = jnp.concatenate([starts, ends[-1:]])
    counts9 = jnp.concatenate([counts, jnp.zeros((1,), i32)])
    ustart9 = jnp.concatenate([jnp.cumsum(counts) - counts, jnp.full((1,), 2 * n, i32)])
    within = s - starts9[e_of_s]
    real_before = ustart9[e_of_s] + jnp.minimum(within, counts9[e_of_s])
    pair = jnp.where(within < counts9[e_of_s],
                     order[jnp.minimum(ustart9[e_of_s] + within, 2 * n - 1)],
                     2 * n + s - real_before)
    src = jnp.concatenate([jnp.where(pair < 2 * n, pair % n, 0), jnp.zeros((tm,), i32)])
    dst = jnp.concatenate([n_rows + jnp.arange(tm, dtype=i32), pair])
    tile_expert = jnp.minimum(e_of_s[::tm], N_EXPERTS - 1)
    return src, dst, tile_expert


def _moe_kernel(te_ref, src_ref, src_next_ref, dst_prev_ref, dst_ref, h_hbm, wg_ref, wu_ref, wd_ref,
                y_hbm, hbuf, ybuf, gsem, ssem):
    t = pl.program_id(0)
    n_t = pl.num_programs(0)
    tm = hbuf.shape[1]
    ff = wg_ref.shape[1]
    fc = ff // MOE_FF_CHUNKS
    slot = t % 2
    other = 1 - slot

    def gather(slot_, r, row):
        return pltpu.make_async_copy(h_hbm.at[pl.ds(row, 1)], hbuf.at[slot_, pl.ds(r, 1)],
                                     gsem.at[slot_])

    def scatter(slot_, r, row):
        return pltpu.make_async_copy(ybuf.at[slot_, pl.ds(r, 1)], y_hbm.at[pl.ds(row, 1)],
                                     ssem.at[slot_])

    def for_rows(fn):
        def body(r, c):
            fn(r)
            return c
        lax.fori_loop(0, tm, body, 0, unroll=8)

    @pl.when(t == 0)
    def _():
        ybuf[...] = jnp.zeros(ybuf.shape, F32)
        for_rows(lambda r: gather(0, r, src_ref[r]).start())

    for_rows(lambda r: gather(slot, r, 0).wait())

    @pl.when(t > 0)
    def _():
        for_rows(lambda r: scatter(slot, r, 0).wait())

    h = hbuf[slot].astype(BF16)
    for r in range(tm):
        gather(other, r, src_next_ref[r]).start()
    for r in range(tm):
        scatter(other, r, dst_prev_ref[r]).start()

    acc = jnp.zeros((tm, wd_ref.shape[1]), F32)
    for c in range(MOE_FF_CHUNKS):
        g = _dot(h, wg_ref[:, c * fc:(c + 1) * fc])
        u = _dot(h, wu_ref[:, c * fc:(c + 1) * fc])
        act = (g * _sigmoid(g) * u).astype(BF16)
        acc = acc + _dot(act, wd_ref[c * fc:(c + 1) * fc, :])
    ybuf[slot] = acc

    @pl.when(t == n_t - 1)
    def _():
        for_rows(lambda r: gather(other, r, 0).wait())
        for_rows(lambda r: scatter(other, r, 0).wait())
        for_rows(lambda r: scatter(slot, r, dst_ref[r]).start())
        for_rows(lambda r: scatter(slot, r, 0).wait())


def _moe_experts(h2, src, dst, tile_expert, w_gate, w_up, w_down, li):
    n, d = h2.shape
    ff = w_gate.shape[3]
    tm = MOE_TILE
    n_tiles = tile_expert.shape[0]
    wspec = lambda s: pl.BlockSpec((None, None) + s, lambda t, te: (li, te[t], 0, 0),
                                   pipeline_mode=pl.Buffered(1))
    idx = lambda off: pl.BlockSpec((tm,), lambda t, te: (t + off,), memory_space=pltpu.SMEM)
    grid_spec = pltpu.PrefetchScalarGridSpec(
        num_scalar_prefetch=1,
        grid=(n_tiles,),
        in_specs=[idx(0), idx(1), idx(0), idx(1),
                  pl.BlockSpec(memory_space=pl.ANY),
                  wspec((d, ff)), wspec((d, ff)), wspec((ff, d))],
        out_specs=pl.BlockSpec(memory_space=pl.ANY),
        scratch_shapes=[pltpu.VMEM((2, tm, d), F32), pltpu.VMEM((2, tm, d), F32),
                        pltpu.SemaphoreType.DMA((2,)), pltpu.SemaphoreType.DMA((2,))],
    )
    return pl.pallas_call(
        _moe_kernel,
        grid_spec=grid_spec,
        out_shape=jax.ShapeDtypeStruct((n_tiles * tm + tm, d), F32),
        compiler_params=_params(1),
        name="moe_experts",
    )(tile_expert, src, src, dst, dst, h2, w_gate, w_up, w_down)


def _combine_kernel(y1_ref, y2_ref, route_ref, x_ref, gf_ref, fg_ref, o_ref, *, final):
    route = route_ref[...]
    ff = route[:, 2:3] * y1_ref[...] + route[:, 3:4] * y2_ref[...]
    x2 = x_ref[...] + gf_ref[...] * ff
    if final:
        ms = jnp.mean(x2 * x2, axis=-1, keepdims=True)
        x2 = x2 * lax.rsqrt(ms + RMS_EPS) * fg_ref[...]
    o_ref[...] = x2


def _moe_combine(y, route, x1, gate_f, final_g, final):
    bsz, seq, d = x1.shape
    n = bsz * seq
    tm = ROW_TILE
    per_batch = seq // tm
    out = pl.pallas_call(
        functools.partial(_combine_kernel, final=final),
        grid=(n // tm,),
        in_specs=[pl.BlockSpec((tm, d), lambda i: (i, 0)),
                  pl.BlockSpec((tm, d), lambda i: (n // tm + i, 0)),
                  pl.BlockSpec((tm, LANES), lambda i: (i, 0)),
                  pl.BlockSpec((tm, d), lambda i: (i, 0)),
                  pl.BlockSpec((None, 1, d), lambda i: (i // per_batch, 0, 0)),
                  pl.BlockSpec((1, d), lambda i: (0, 0))],
        out_specs=pl.BlockSpec((tm, d), lambda i: (i, 0)),
        out_shape=jax.ShapeDtypeStruct((n, d), F32),
        compiler_params=_params(1),
        name="moe_combine",
    )(y, y, route.reshape(n, LANES), x1.reshape(n, d),
      gate_f.reshape(bsz, 1, d), final_g.reshape(1, d))
    return out.reshape(bsz, seq, d)


def _even_layer(x, mods, norm_mix_g, norm_ffn_g, cos, sin, li, w_in, conv_w, w_out,
                w_gate, w_up, w_down, next_norm):
    sh_m, sc_m, g_m, sh_f, sc_f, g_f = mods
    qkv, b_out = _even_inproj(x, norm_mix_g, sh_m, sc_m, w_in, li, cos, sin, conv_w)
    x1, h2 = _attention_outproj(qkv, b_out, x, w_out, li, g_m, norm_ffn_g, sh_f, sc_f)
    return _dense_ffn(h2, x1, w_gate, w_up, w_down, li, g_f, *next_norm)


def _odd_layer(x, h, mods, norm_ffn_g, ssm, d_skip, li, glu_w, glu_b,
               router_w, router_b, w_gate, w_up, w_down, final_g, final):
    _, _, g_m, sh_f, sc_f, g_f = mods
    bsz, seq, d = x.shape
    y = _ssm_scan(h, _ssm_operators(*ssm))
    x1, h2, route = _ssm_out(y, h, x, d_skip, glu_w, li, glu_b, g_m,
                             norm_ffn_g, sh_f, sc_f, router_w, router_b)
    route = route.reshape(bsz * seq, LANES)
    src, dst, tile_expert = _routing_tables(route, MOE_TILE)
    y_pairs = _moe_experts(h2.reshape(bsz * seq, d), src, dst, tile_expert,
                           w_gate, w_up, w_down, li)
    return _moe_combine(y_pairs, route, x1, g_f, final_g, final)


def kernel(x, c, positions, mod_w, mod_b, norm_mix_g, norm_ffn_g, ev_w_in, ev_conv_w, ev_w_out, ffn_w_gate, ffn_w_up, ffn_w_down, ssm_a_re, ssm_a_im, ssm_log_step, ssm_b_re, ssm_b_im, ssm_c_re, ssm_c_im, ssm_d, glu_w, glu_b, moe_router_w, moe_router_b, moe_w_gate, moe_w_up, moe_w_down, final_norm_g):
    depth = mod_w.shape[0]
    d = x.shape[2]
    assert depth % 2 == 0
    mod = _modulation(c, mod_w, mod_b)
    cos, sin = _rope_tables(positions)
    layer_mods = [[mod[layer, :, j * d:(j + 1) * d] for j in range(6)] for layer in range(depth)]
    bf = lambda w: w.astype(BF16)
    ev_w_in, ev_w_out, glu_w = bf(ev_w_in), bf(ev_w_out), bf(glu_w)
    ffn_w_gate, ffn_w_up, ffn_w_down = bf(ffn_w_gate), bf(ffn_w_up), bf(ffn_w_down)
    moe_w_gate, moe_w_up, moe_w_down = bf(moe_w_gate), bf(moe_w_up), bf(moe_w_down)
    h = None
    for layer in range(depth):
        mods = layer_mods[layer]
        i = layer // 2
        if layer % 2 == 0:
            nxt = layer_mods[layer + 1]
            x, h = _even_layer(x, mods, norm_mix_g[layer], norm_ffn_g[layer], cos, sin, i,
                               ev_w_in, ev_conv_w[i], ev_w_out,
                               ffn_w_gate, ffn_w_up, ffn_w_down,
                               (norm_mix_g[layer + 1], nxt[0], nxt[1]))
        else:
            ssm = (ssm_a_re[i], ssm_a_im[i], ssm_log_step[i], ssm_b_re[i], ssm_b_im[i],
                   ssm_c_re[i], ssm_c_im[i])
            x = _odd_layer(x, h, mods, norm_ffn_g[layer], ssm, ssm_d[i], i,
                           glu_w, glu_b[i], moe_router_w[i], moe_router_b[i],
                           moe_w_gate, moe_w_up, moe_w_down,
                           final_norm_g, layer == depth - 1)
    return x
```

```python
import functools
import math

import jax
import jax.numpy as jnp
from jax import lax
from jax.experimental import pallas as pl
from jax.experimental.pallas import tpu as pltpu

F32 = jnp.float32
BF16 = jnp.bfloat16

ATTN_HEADS = 8
HEAD_DIM = 64
ATTN_WIDTH = ATTN_HEADS * HEAD_DIM
ROPE_DIM = HEAD_DIM // 4
ROPE_THETA = 500000.0
DILATED_PAIRS = ((128, 1), (512, 4), (2048, 16))
ATTN_BLOCK = 128
SSM_GROUP = 16
SSM_STATE = 64
SSM_CHUNK = 16
N_EXPERTS = 8
RMS_EPS = 1e-6

LANES = 128
VMEM_LIMIT_BYTES = 56 * 1024 * 1024

ROW_TILE = 512
MOE_TILE = 512
MOE_FF_CHUNKS = 2
ATTN_GROUP = 4
SSM_CHUNKS_PER_STEP = 64
SSM_LANE_GROUPS = LANES // SSM_GROUP


def _params(n_axes, vmem=VMEM_LIMIT_BYTES):
    return pltpu.CompilerParams(
        dimension_semantics=("arbitrary",) * n_axes, vmem_limit_bytes=vmem)


def _dot(a, b):
    return jnp.dot(a, b, preferred_element_type=F32)


def _sigmoid(x):
    return 1.0 / (1.0 + jnp.exp(-x))


def _split_bf16(x):
    hi = x.astype(BF16)
    lo = (x - hi.astype(F32)).astype(BF16)
    return hi, lo


def _ada_norm(x, g, shift, scale):
    ms = jnp.mean(x * x, axis=-1, keepdims=True)
    return x * lax.rsqrt(ms + RMS_EPS) * g * (1.0 + scale) + shift


def _mod_kernel(c_ref, w_ref, b_ref, o_ref):
    c = c_ref[...]
    cond = c * _sigmoid(c)
    ch, cl = _split_bf16(cond)
    wh, wl = _split_bf16(w_ref[...])
    o_ref[...] = _dot(ch, wh) + _dot(cl, wh) + _dot(ch, wl) + b_ref[...]


def _modulation(c, mod_w, mod_b):
    depth, d, n = mod_w.shape
    bsz = c.shape[0]
    tn = 1024
    return pl.pallas_call(
        _mod_kernel,
        grid=(depth, n // tn),
        in_specs=[pl.BlockSpec((bsz, d), lambda l, j: (0, 0)),
                  pl.BlockSpec((None, d, tn), lambda l, j: (l, 0, j)),
                  pl.BlockSpec((None, 1, tn), lambda l, j: (l, 0, j))],
        out_specs=pl.BlockSpec((None, bsz, tn), lambda l, j: (l, 0, j)),
        out_shape=jax.ShapeDtypeStruct((depth, bsz, n), F32),
        compiler_params=_params(2),
        name="modulation",
    )(c, mod_w, mod_b.reshape(depth, 1, n))


def _rope_kernel(pos_ref, inv_ref, cos_ref, sin_ref):
    ang = pos_ref[...].astype(F32) * inv_ref[...]
    cos_ref[...] = jnp.cos(ang)
    sin_ref[...] = jnp.sin(ang)


def _rope_tables(positions):
    bsz, seq = positions.shape
    inv = ROPE_THETA ** (-jnp.arange(0, ROPE_DIM, 2, dtype=F32) / ROPE_DIM)
    lane = jnp.arange(LANES) % HEAD_DIM
    inv_lane = jnp.where(lane < ROPE_DIM, inv[lane % (ROPE_DIM // 2)], 0.0).reshape(1, LANES)
    tm = ROW_TILE
    spec = pl.BlockSpec((None, tm, LANES), lambda b, i: (b, i, 0))
    return pl.pallas_call(
        _rope_kernel,
        grid=(bsz, seq // tm),
        in_specs=[pl.BlockSpec((None, tm, 1), lambda b, i: (b, i, 0)),
                  pl.BlockSpec((1, LANES), lambda b, i: (0, 0))],
        out_specs=[spec, spec],
        out_shape=[jax.ShapeDtypeStruct((bsz, seq, LANES), F32)] * 2,
        compiler_params=_params(2),
        name="rope_tables",
    )(positions.reshape(bsz, seq, 1), inv_lane)


def _inproj_kernel(x_ref, g_ref, sh_ref, sc_ref, w_ref, cos_ref, sin_ref, cw_ref, *rest):
    n_pat = len(DILATED_PAIRS)
    qkv_refs = [rest[3 * p:3 * p + 3] for p in range(n_pat)]
    bo_ref, ubuf, stage = rest[3 * n_pat:]
    tm = x_ref.shape[0]
    aw = ATTN_WIDTH
    cwid = cw_ref.shape[1]
    h = _ada_norm(x_ref[...], g_ref[...], sh_ref[...], sc_ref[...]).astype(BF16)
    proj = _dot(h, w_ref[...])

    cos = cos_ref[...]
    sin = sin_ref[...]
    lane = lax.broadcasted_iota(jnp.int32, (tm, LANES), 1) % HEAD_DIM
    first_half = lane < ROPE_DIM // 2

    def rope(t):
        rot = jnp.where(first_half,
                        -pltpu.roll(t, LANES - ROPE_DIM // 2, 1),
                        pltpu.roll(t, ROPE_DIM // 2, 1))
        return t * cos + rot * sin

    n_col = aw // LANES
    for j in range(n_col):
        sl = slice(j * LANES, (j + 1) * LANES)
        stage[0, j] = rope(proj[:, sl]) * (HEAD_DIM ** -0.5)
        stage[1, j] = rope(proj[:, aw + j * LANES: aw + (j + 1) * LANES])
        stage[2, j] = proj[:, 2 * aw + j * LANES: 2 * aw + (j + 1) * LANES]
    for (_, dil), refs in zip(DILATED_PAIRS, qkv_refs):
        for which, ref in enumerate(refs):
            for j in range(n_col):
                sl = slice(j * LANES, (j + 1) * LANES)
                if dil == 1:
                    ref[:, sl] = stage[which, j].astype(BF16)
                else:
                    for r in range(dil):
                        ref[r, :, sl] = stage[which, j, pl.ds(r, tm // dil, stride=dil), :].astype(BF16)

    b_gate = proj[:, 3 * aw:3 * aw + cwid]
    c_gate = proj[:, 3 * aw + cwid:3 * aw + 2 * cwid]
    xin = proj[:, 3 * aw + 2 * cwid:]
    u = c_gate * xin

    @pl.when(pl.program_id(1) == 0)
    def _():
        ubuf[0:8, :] = jnp.zeros((8, cwid), F32)

    ubuf[8:, :] = u
    conv = (cw_ref[0:1, :] * ubuf[6:6 + tm, :] + cw_ref[1:2, :] * ubuf[7:7 + tm, :]
            + cw_ref[2:3, :] * u)
    bo_ref[...] = (b_gate * conv).astype(BF16)
    ubuf[0:8, :] = ubuf[tm:tm + 8, :]


def _even_inproj(x, g, shift, scale, w_in, li, cos, sin, conv_w):
    bsz, seq, d = x.shape
    n = w_in.shape[2]
    cwid = conv_w.shape[1]
    tm = ROW_TILE
    aw = ATTN_WIDTH
    row = lambda w: pl.BlockSpec((None, tm, w), lambda b, i: (b, i, 0))
    per_b = pl.BlockSpec((None, 1, d), lambda b, i: (b, 0, 0))
    qkv_specs, qkv_shapes = [], []
    for window, dil in DILATED_PAIRS:
        assert window // dil == ATTN_BLOCK and seq % window == 0 and (dil == 1 or window % tm == 0)
        if dil == 1:
            spec, shape = row(aw), (bsz, seq, aw)
        else:
            per_span = window // tm
            spec = pl.BlockSpec((None, None, dil, tm // dil, aw),
                                lambda b, i, per_span=per_span: (b, i // per_span, 0, i % per_span, 0))
            shape = (bsz, seq // window, dil, ATTN_BLOCK, aw)
        qkv_specs += [spec] * 3
        qkv_shapes += [jax.ShapeDtypeStruct(shape, BF16)] * 3
    outs = pl.pallas_call(
        _inproj_kernel,
        grid=(bsz, seq // tm),
        in_specs=[row(d), pl.BlockSpec((1, d), lambda b, i: (0, 0)), per_b, per_b,
                  pl.BlockSpec((None, d, n), lambda b, i: (li, 0, 0)),
                  row(LANES), row(LANES),
                  pl.BlockSpec(conv_w.shape, lambda b, i: (0, 0))],
        out_specs=qkv_specs + [row(cwid)],
        out_shape=qkv_shapes + [jax.ShapeDtypeStruct((bsz, seq, cwid), BF16)],
        scratch_shapes=[pltpu.VMEM((tm + 8, cwid), F32),
                        pltpu.VMEM((3, aw // LANES, tm, LANES), F32)],
        compiler_params=_params(2),
        name="even_inproj",
    )(x, g.reshape(1, d), shift.reshape(bsz, 1, d), scale.reshape(bsz, 1, d),
      w_in, cos, sin, conv_w)
    qkv = [outs[3 * p:3 * p + 3] for p in range(len(DILATED_PAIRS))]
    return qkv, outs[-1]


def _attn_item(q, kk, vv, valid):
    blk = ATTN_BLOCK
    lane = lax.broadcasted_iota(jnp.int32, (blk, LANES), 1)
    lane_kv = lax.broadcasted_iota(jnp.int32, (2 * blk, LANES), 1)
    neg_inf = jnp.float32(-jnp.inf)
    ml_new = jnp.zeros((blk, LANES), F32)
    n_col, per_col = ATTN_WIDTH // LANES, LANES // HEAD_DIM
    scores = []
    for j in range(n_col):
        sl = slice(j * LANES, (j + 1) * LANES)
        for hh in range(per_col):
            in_head = (lane // HEAD_DIM) == hh
            qm = jnp.where(in_head, q[:, sl], jnp.zeros_like(q[:, sl]))
            s = lax.dot_general(qm, kk[:, sl], (((1,), (1,)), ((), ())),
                                preferred_element_type=F32)
            scores.append(jnp.where(valid, s, neg_inf))
    accs = []
    for j in range(n_col):
        sl = slice(j * LANES, (j + 1) * LANES)
        vj = vv[:, sl]
        pv = []
        for hh in range(per_col):
            head = j * per_col + hh
            in_head_kv = (lane_kv // HEAD_DIM) == hh
            s = scores[head]
            m_new = jnp.max(s, axis=1, keepdims=True)
            p = jnp.exp(s - m_new)
            l_new = jnp.sum(p, axis=1, keepdims=True)
            vm = jnp.where(in_head_kv, vj, jnp.zeros_like(vj))
            pv.append(_dot(p.astype(BF16), vm))
            ml_new = jnp.where(lane == head, m_new, ml_new)
            ml_new = jnp.where(lane == ATTN_HEADS + head, l_new, ml_new)
        accs.append(pv[0] + pv[1])
    return accs, ml_new


def _head_columns(tile, first_lane, j):
    lane = lax.broadcasted_iota(jnp.int32, tile.shape, 1)
    h0 = first_lane + j * (LANES // HEAD_DIM)
    return jnp.where(lane < HEAD_DIM, tile[:, h0:h0 + 1], tile[:, h0 + 1:h0 + 2])


def _attn_merge(acc_old, ml_old, acc_loc, ml_loc):
    lane = lax.broadcasted_iota(jnp.int32, ml_old.shape, 1)
    m_new = jnp.maximum(ml_old, ml_loc)
    a_old = jnp.exp(ml_old - m_new)
    a_loc = jnp.exp(ml_loc - m_new)
    l_new = (pltpu.roll(a_old, ATTN_HEADS, 1) * ml_old
             + pltpu.roll(a_loc, ATTN_HEADS, 1) * ml_loc)
    ml_new = jnp.where(lane < ATTN_HEADS, m_new, jnp.where(lane < 2 * ATTN_HEADS, l_new, 0.0))
    accs = [_head_columns(a_old, 0, j) * acc_old[j] + _head_columns(a_loc, 0, j) * acc_loc[j]
            for j in range(len(acc_loc))]
    return accs, ml_new


def _band_mask(has_prev):
    blk = ATTN_BLOCK
    row = lax.broadcasted_iota(jnp.int32, (blk, 2 * blk), 0)
    col = lax.broadcasted_iota(jnp.int32, (blk, 2 * blk), 1)
    band = (col >= row) & (col <= row + blk)
    return band, band & ((col >= blk) | has_prev)


def _attn_wide_kernel(q_ref, kc_ref, vc_ref, kp_ref, vp_ref, acc_out, ml_out):
    _, valid = _band_mask(pl.program_id(1) > 0)
    for r in range(q_ref.shape[0]):
        kk = jnp.concatenate([kp_ref[r], kc_ref[r]], axis=0)
        vv = jnp.concatenate([vp_ref[r], vc_ref[r]], axis=0)
        accs, ml_new = _attn_item(q_ref[r], kk, vv, valid)
        for j, a in enumerate(accs):
            acc_out[r, :, j * LANES:(j + 1) * LANES] = a
        ml_out[r] = ml_new


def _attn_mid_kernel(q_ref, kc_ref, vc_ref, kp_ref, vp_ref, accw_ref, mlw_ref, acc_out, ml_out,
                     old_acc, old_ml, loc_acc, loc_ml):
    dil = q_ref.shape[0]
    ratio = accw_ref.shape[0] // dil
    sub = accw_ref.shape[1]
    n_col = ATTN_WIDTH // LANES
    blk = ATTN_BLOCK
    _, valid = _band_mask(pl.program_id(1) > 0)
    for r in range(dil):
        for qd in range(ratio):
            rows = pl.ds(r * blk + qd, sub, stride=ratio)
            for j in range(n_col):
                old_acc[j, rows, :] = accw_ref[dil * qd + r, :, j * LANES:(j + 1) * LANES]
            old_ml[rows, :] = mlw_ref[dil * qd + r]
        kk = jnp.concatenate([kp_ref[r], kc_ref[r]], axis=0)
        vv = jnp.concatenate([vp_ref[r], vc_ref[r]], axis=0)
        accs, ml = _attn_item(q_ref[r], kk, vv, valid)
        for j, a in enumerate(accs):
            loc_acc[j, r * blk:(r + 1) * blk, :] = a
        loc_ml[r * blk:(r + 1) * blk, :] = ml
    accs, ml_new = _attn_merge([old_acc[j] for j in range(n_col)], old_ml[...],
                               [loc_acc[j] for j in range(n_col)], loc_ml[...])
    for r in range(dil):
        rows = pl.ds(r, blk, stride=dil)
        for j, a in enumerate(accs):
            acc_out[j, rows, :] = a[r * blk:(r + 1) * blk]
        ml_out[rows, :] = ml_new[r * blk:(r + 1) * blk]


def _attn_last_kernel(q_ref, kc_ref, vc_ref, kp_ref, vp_ref, acc_in, ml_in,
                      b_ref, x_ref, wa_ref, wb_ref, gm_ref, g_ref, sh_ref, sc_ref,
                      x1_ref, h_ref, loc_acc, loc_ml):
    gb = q_ref.shape[0]
    n_col = ATTN_WIDTH // LANES
    blk = ATTN_BLOCK
    band, valid0 = _band_mask(pl.program_id(1) > 0)
    for i in range(gb):
        if i == 0:
            k_prev, v_prev, valid = kp_ref[...], vp_ref[...], valid0
        else:
            k_prev, v_prev, valid = kc_ref[i - 1], vc_ref[i - 1], band
        kk = jnp.concatenate([k_prev, kc_ref[i]], axis=0)
        vv = jnp.concatenate([v_prev, vc_ref[i]], axis=0)
        accs, ml = _attn_item(q_ref[i], kk, vv, valid)
        for j, a in enumerate(accs):
            loc_acc[j, i * blk:(i + 1) * blk, :] = a
        loc_ml[i * blk:(i + 1) * blk, :] = ml
    ml_old = ml_in[...].reshape(gb * blk, LANES)
    accs, ml_new = _attn_merge([acc_in[j] for j in range(n_col)], ml_old,
                               [loc_acc[j] for j in range(n_col)], loc_ml[...])
    lane = lax.broadcasted_iota(jnp.int32, ml_new.shape, 1)
    is_l = (lane >= ATTN_HEADS) & (lane < 2 * ATTN_HEADS)
    linv = 1.0 / jnp.where(is_l, ml_new, 1.0)
    attn = jnp.concatenate([(a * _head_columns(linv, ATTN_HEADS, j)).astype(BF16)
                            for j, a in enumerate(accs)], axis=1)
    conv = b_ref[...].reshape(gb * blk, b_ref.shape[2])
    mix = _dot(attn, wa_ref[...]) + _dot(conv, wb_ref[...])
    x1 = x_ref[...] + gm_ref[...] * mix
    x1_ref[...] = x1
    h_ref[...] = _ada_norm(x1, g_ref[...], sh_ref[...], sc_ref[...]).astype(BF16)


def _attention_outproj(qkv, b_out, x, w_out, li, gate_m, g, shift, scale):
    (_, d1), (_, dm), (_, dw) = DILATED_PAIRS
    assert d1 == 1 and dw % dm == 0
    blk, aw = ATTN_BLOCK, ATTN_WIDTH
    (q1, k1, v1), (qm, km, vm), (qw, kw, vw) = qkv
    bsz, seq, _ = q1.shape

    nw = qw.shape[1]
    rpg = min(ATTN_GROUP, dw)
    cur = lambda w: pl.BlockSpec((None, None, rpg, blk, w), lambda b, n, g: (b, n, g, 0, 0))
    prev = pl.BlockSpec((None, None, rpg, blk, aw),
                        lambda b, n, g: (b, jnp.maximum(n - 1, 0), g, 0, 0))
    acc_w, ml_w = pl.pallas_call(
        _attn_wide_kernel,
        grid=(bsz, nw, dw // rpg),
        in_specs=[cur(aw), cur(aw), cur(aw), prev, prev],
        out_specs=[cur(aw), cur(LANES)],
        out_shape=[jax.ShapeDtypeStruct((bsz, nw, dw, blk, aw), F32),
                   jax.ShapeDtypeStruct((bsz, nw, dw, blk, LANES), F32)],
        compiler_params=_params(3),
        name=f"attn_dil{dw}",
    )(qw, kw, vw, kw, vw)

    nm = qm.shape[1]
    ratio = dw // dm
    sub = blk // ratio
    cur = pl.BlockSpec((None, None, dm, blk, aw), lambda b, n: (b, n, 0, 0, 0))
    prev = pl.BlockSpec((None, None, dm, blk, aw),
                        lambda b, n: (b, jnp.maximum(n - 1, 0), 0, 0, 0))
    wide = lambda w: pl.BlockSpec((None, None, dw, sub, w),
                                  lambda b, n: (b, n // ratio, 0, n % ratio, 0))
    n_col = aw // LANES
    acc, ml = pl.pallas_call(
        _attn_mid_kernel,
        grid=(bsz, nm),
        in_specs=[cur, cur, cur, prev, prev, wide(aw), wide(LANES)],
        out_specs=[pl.BlockSpec((None, n_col, dm * blk, LANES), lambda b, n: (b, 0, n, 0)),
                   pl.BlockSpec((None, dm * blk, LANES), lambda b, n: (b, n, 0))],
        out_shape=[jax.ShapeDtypeStruct((bsz, n_col, seq, LANES), F32),
                   jax.ShapeDtypeStruct((bsz, seq, LANES), F32)],
        scratch_shapes=[pltpu.VMEM((n_col, dm * blk, LANES), F32), pltpu.VMEM((dm * blk, LANES), F32),
                        pltpu.VMEM((n_col, dm * blk, LANES), F32), pltpu.VMEM((dm * blk, LANES), F32)],
        compiler_params=_params(2),
        name=f"attn_dil{dm}",
    )(qm, km, vm, km, vm, acc_w, ml_w)

    nb = seq // blk
    gb = min(ATTN_GROUP, nb)
    view = lambda t: t.reshape(bsz, nb, blk, t.shape[-1])
    cur = lambda w: pl.BlockSpec((None, gb, blk, w), lambda b, n: (b, n, 0, 0))
    prev = pl.BlockSpec((None, None, blk, aw), lambda b, n: (b, jnp.maximum(n * gb - 1, 0), 0, 0))
    d = x.shape[2]
    bw = b_out.shape[2]
    assert w_out.shape[1] == aw + bw and bw == aw
    rows = pl.BlockSpec((None, gb * blk, d), lambda b, n: (b, n, 0))
    per_b = pl.BlockSpec((None, 1, d), lambda b, n: (b, 0, 0))
    pb = lambda t: t.reshape(bsz, 1, d)
    return pl.pallas_call(
        _attn_last_kernel,
        grid=(bsz, nb // gb),
        in_specs=[cur(aw), cur(aw), cur(aw), prev, prev,
                  pl.BlockSpec((None, n_col, gb * blk, LANES), lambda b, n: (b, 0, n, 0)),
                  cur(LANES), cur(bw), rows,
                  pl.BlockSpec((None, aw, d), lambda b, n: (li, 0, 0)),
                  pl.BlockSpec((None, bw, d), lambda b, n: (li, 1, 0)),
                  per_b, pl.BlockSpec((1, d), lambda b, n: (0, 0)), per_b, per_b],
        out_specs=[rows, rows],
        out_shape=[jax.ShapeDtypeStruct((bsz, seq, d), F32),
                   jax.ShapeDtypeStruct((bsz, seq, d), BF16)],
        scratch_shapes=[pltpu.VMEM((n_col, gb * blk, LANES), F32),
                        pltpu.VMEM((gb * blk, LANES), F32)],
        compiler_params=_params(2),
        name=f"attn_dil{d1}_outproj",
    )(view(q1), view(k1), view(v1), view(k1), view(v1), acc, view(ml), view(b_out), x,
      w_out, w_out, pb(gate_m), g.reshape(1, d), pb(shift), pb(scale))


def _ffn_kernel(h_ref, x_ref, wg_ref, wu_ref, wd_ref, gf_ref, g_ref, sh_ref, sc_ref,
                o_ref, hn_ref):
    h = h_ref[...]
    g = _dot(h, wg_ref[...])
    u = _dot(h, wu_ref[...])
    act = (g * _sigmoid(g) * u).astype(BF16)
    x2 = x_ref[...] + gf_ref[...] * _dot(act, wd_ref[...])
    o_ref[...] = x2
    hn_ref[...] = _ada_norm(x2, g_ref[...], sh_ref[...], sc_ref[...]).astype(BF16)


def _dense_ffn(h, x, w_gate, w_up, w_down, li, gate_f, next_g, next_shift, next_scale):
    bsz, seq, d = x.shape
    ff = w_gate.shape[2]
    tm = ROW_TILE
    row = pl.BlockSpec((None, tm, d), lambda b, i: (b, i, 0))
    per_b = pl.BlockSpec((None, 1, d), lambda b, i: (b, 0, 0))
    full = lambda s: pl.BlockSpec((None,) + s, lambda b, i: (li, 0, 0),
                                  pipeline_mode=pl.Buffered(1))
    pb = lambda t: t.reshape(bsz, 1, d)
    return pl.pallas_call(
        _ffn_kernel,
        grid=(bsz, seq // tm),
        in_specs=[row, row, full((d, ff)), full((d, ff)), full((ff, d)), per_b,
                  pl.BlockSpec((1, d), lambda b, i: (0, 0)), per_b, per_b],
        out_specs=[row, row],
        out_shape=[jax.ShapeDtypeStruct((bsz, seq, d), F32),
                   jax.ShapeDtypeStruct((bsz, seq, d), BF16)],
        compiler_params=_params(2),
        name="dense_ffn",
    )(h, x, w_gate, w_up, w_down, pb(gate_f), next_g.reshape(1, d), pb(next_shift),
      pb(next_scale))


def _ssm_operators(a_re, a_im, log_step, b_re, b_im, c_re, c_im):
    L = SSM_CHUNK
    a_re, a_im = a_re.astype(F32), a_im.astype(F32)
    dt = jnp.exp(log_step.astype(F32))[:, None]
    mag = jnp.exp(a_re * dt)
    abar_re = mag * jnp.cos(a_im * dt)
    abar_im = mag * jnp.sin(a_im * dt)
    den = a_re * a_re + a_im * a_im
    nr = abar_re - 1.0
    f_re = (nr * a_re + abar_im * a_im) / den
    f_im = (abar_im * a_re - nr * a_im) / den
    b_re, b_im = b_re.astype(F32), b_im.astype(F32)
    bb_re = f_re[..., None] * b_re - f_im[..., None] * b_im
    bb_im = f_re[..., None] * b_im + f_im[..., None] * b_re
    c_re, c_im = c_re.astype(F32), c_im.astype(F32)

    def step(carry, _):
        pr, pi = carry
        nxt = (pr * abar_re - pi * abar_im, pr * abar_im + pi * abar_re)
        return nxt, carry
    (pl_re, pl_im), (pw_re, pw_im) = lax.scan(
        step, (jnp.ones_like(abar_re), jnp.zeros_like(abar_re)), None, length=L)
    pw_re = jnp.concatenate([pw_re, pl_re[None]], axis=0)
    pw_im = jnp.concatenate([pw_im, pl_im[None]], axis=0)

    hi = lax.Precision.HIGHEST
    cp_re = c_re[None] * pw_re[:L, :, None, :] - c_im[None] * pw_im[:L, :, None, :]
    cp_im = c_re[None] * pw_im[:L, :, None, :] + c_im[None] * pw_re[:L, :, None, :]
    w = (jnp.einsum('tgcp,gpd->tgdc', cp_re, bb_re, precision=hi)
         - jnp.einsum('tgcp,gpd->tgdc', cp_im, bb_im, precision=hi))
    s_idx = jnp.arange(L)[:, None]
    t_idx = jnp.arange(L)[None, :]
    tau = t_idx - s_idx
    toep = jnp.where((tau >= 0)[:, :, None, None, None], w[jnp.clip(tau, 0, L - 1)], 0.0)
    n_g, n_c = a_re.shape[0], b_re.shape[2]
    toep = toep.transpose(2, 0, 3, 1, 4).reshape(n_g, L * n_c, L * n_c)

    rp_re, rp_im = pw_re[:L][::-1], pw_im[:L][::-1]
    so_re = rp_re[..., None] * bb_re[None] - rp_im[..., None] * bb_im[None]
    so_im = rp_re[..., None] * bb_im[None] + rp_im[..., None] * bb_re[None]
    to_rows = lambda t: t.transpose(1, 0, 3, 2).reshape(n_g, L * n_c, -1)
    s_out = jnp.concatenate([to_rows(so_re), to_rows(so_im)], axis=-1)
    s_out_sw = jnp.concatenate([to_rows(so_im), to_rows(so_re)], axis=-1)

    qp_re, qp_im = pw_re[1:], pw_im[1:]
    ci_re = c_re[None] * qp_re[:, :, None, :] - c_im[None] * qp_im[:, :, None, :]
    ci_im = c_re[None] * qp_im[:, :, None, :] + c_im[None] * qp_re[:, :, None, :]
    to_cols = lambda t: t.transpose(1, 3, 0, 2).reshape(n_g, -1, L * n_c)
    c_in = jnp.concatenate([to_cols(ci_re), -to_cols(ci_im)], axis=1)

    al_re, al_im = pw_re[L], pw_im[L]
    a1 = jnp.concatenate([al_re, al_re], axis=-1)[:, None, :]
    a2 = jnp.concatenate([-al_im, al_im], axis=-1)[:, None, :]
    return (toep.astype(BF16), s_out.astype(BF16), s_out_sw.astype(BF16),
            c_in.astype(BF16), a1, a2)


def _piece_transpose(arrs):
    n = len(arrs)
    piece = lax.broadcasted_iota(jnp.int32, (1, LANES), 1) // SSM_GROUP
    arrs = list(arrs)
    dist = n // 2
    while dist >= 1:
        keep = (piece & dist) == 0
        for i in range(n):
            if i & dist == 0:
                a, b = arrs[i], arrs[i + dist]
                arrs[i] = jnp.where(keep, a, pltpu.roll(b, dist * SSM_GROUP, 1))
                arrs[i + dist] = jnp.where(keep, pltpu.roll(a, LANES - dist * SSM_GROUP, 1), b)
        dist //= 2
    return arrs


def _ssm_kernel(u_ref, t_ref, s_ref, ssw_ref, cin_ref, a1_ref, a2_ref, y_ref,
                st_ref, stsw_ref, xs_ref, xc_ref, zc_ref):
    kt, L, bsz, _ = u_ref.shape
    rows = kt * bsz
    n_lg = SSM_LANE_GROUPS

    @pl.when(pl.program_id(1) == 0)
    def _():
        xc_ref[...] = jnp.zeros(xc_ref.shape, F32)
        zc_ref[...] = jnp.zeros(zc_ref.shape, F32)

    zs = [u_ref[:, s].reshape(rows, LANES) for s in range(L)]
    lo = _piece_transpose(zs[:n_lg])
    hi = _piece_transpose(zs[n_lg:])
    ys = []
    for gl in range(n_lg):
        v = jnp.concatenate([lo[gl], hi[gl]], axis=1)
        st_ref[...] = _dot(v, s_ref[gl])
        stsw_ref[...] = _dot(v, ssw_ref[gl])
        a1 = jnp.broadcast_to(a1_ref[gl], (bsz, 2 * SSM_STATE))
        a2 = jnp.broadcast_to(a2_ref[gl], (bsz, 2 * SSM_STATE))

        def body(k, carry):
            x, z = carry
            r = pl.ds(pl.multiple_of(k * bsz, bsz), bsz)
            xs_ref[r, :] = x
            x_new = x * a1 + z * a2 + st_ref[r, :]
            z_new = z * a1 - x * a2 + stsw_ref[r, :]
            return x_new, z_new

        x, z = lax.fori_loop(0, kt, body, (xc_ref[gl], zc_ref[gl]), unroll=8)
        xc_ref[gl] = x
        zc_ref[gl] = z
        ys.append((_dot(v, t_ref[gl]) + _dot(xs_ref[...].astype(BF16), cin_ref[gl])).astype(BF16))
    out_lo = _piece_transpose([y[:, :LANES] for y in ys])
    out_hi = _piece_transpose([y[:, LANES:] for y in ys])
    for t in range(n_lg):
        y_ref[:, t] = out_lo[t].reshape(kt, bsz, LANES)
        y_ref[:, n_lg + t] = out_hi[t].reshape(kt, bsz, LANES)


def _ssm_scan(h, ops):
    bsz, seq, d = h.shape
    toep, s_out, s_out_sw, c_in, a1, a2 = ops
    L, C, n_lg = SSM_CHUNK, SSM_GROUP, SSM_LANE_GROUPS
    assert L == 2 * n_lg
    n_k = seq // L
    kt = min(SSM_CHUNKS_PER_STEP, n_k)
    hv = h.transpose(1, 0, 2).reshape(n_k, L, bsz, d)
    act = pl.BlockSpec((kt, L, bsz, LANES), lambda o, k: (k, 0, 0, o))
    wblk = lambda s: pl.BlockSpec((n_lg,) + s, lambda o, k: (o, 0, 0))
    y = pl.pallas_call(
        _ssm_kernel,
        grid=(d // LANES, n_k // kt),
        in_specs=[act, wblk((L * C, L * C)), wblk((L * C, 2 * SSM_STATE)),
                  wblk((L * C, 2 * SSM_STATE)), wblk((2 * SSM_STATE, L * C)),
                  wblk((1, 2 * SSM_STATE)), wblk((1, 2 * SSM_STATE))],
        out_specs=act,
        out_shape=jax.ShapeDtypeStruct((n_k, L, bsz, d), BF16),
        scratch_shapes=[pltpu.VMEM((kt * bsz, 2 * SSM_STATE), F32)] * 3
        + [pltpu.VMEM((n_lg, bsz, 2 * SSM_STATE), F32)] * 2,
        compiler_params=_params(2),
        name="ssm_scan",
    )(hv, toep, s_out, s_out_sw, c_in, a1, a2)
    return y.reshape(seq, bsz, d).transpose(1, 0, 2)


def _gelu_tanh(x):
    return 0.5 * x * (1.0 + jnp.tanh(math.sqrt(2.0 / math.pi) * (x + 0.044715 * (x * x * x))))


def _ssm_out_kernel(y_ref, h_ref, x_ref, d_ref, w_ref, b_ref, gm_ref, g_ref, sh_ref, sc_ref,
                    rw_ref, rb_ref, x1_ref, h2_ref, route_ref):
    d = x_ref.shape[1]
    y = y_ref[...].astype(F32) + d_ref[...] * h_ref[...].astype(F32)
    z = _dot(_gelu_tanh(y).astype(BF16), w_ref[...]) + b_ref[...]
    mix = z[:, :d] * _sigmoid(z[:, d:])
    x1 = x_ref[...] + gm_ref[...] * mix
    x1_ref[...] = x1
    h2 = _ada_norm(x1, g_ref[...], sh_ref[...], sc_ref[...])
    h2_ref[...] = h2

    hh, hl = _split_bf16(h2)
    both = _dot(hh, rw_ref[...])
    logits = (both[:, :LANES] + both[:, LANES:] + _dot(hl, rw_ref[:, :LANES]) + rb_ref[...])
    lane = lax.broadcasted_iota(jnp.int32, logits.shape, 1)
    lane_f = lane.astype(F32)
    neg_inf = jnp.float32(-jnp.inf)
    logits = jnp.where(lane < N_EXPERTS, logits, neg_inf)
    m1 = jnp.max(logits, axis=1, keepdims=True)
    i1 = jnp.min(jnp.where(logits == m1, lane_f, float(LANES)), axis=1, keepdims=True)
    rest = jnp.where(lane_f == i1, neg_inf, logits)
    m2 = jnp.max(rest, axis=1, keepdims=True)
    i2 = jnp.min(jnp.where(rest == m2, lane_f, float(LANES)), axis=1, keepdims=True)
    e2 = jnp.exp(m2 - m1)
    g1 = 1.0 / (1.0 + e2)
    g2 = e2 / (1.0 + e2)
    route = jnp.where(lane == 0, i1, jnp.where(lane == 1, i2,
                      jnp.where(lane == 2, g1, jnp.where(lane == 3, g2, 0.0))))
    route_ref[...] = route


def _ssm_out(y, h, x, d_skip, glu_w, li, glu_b, gate_m, g, shift, scale, router_w, router_b):
    bsz, seq, d = x.shape
    tm = ROW_TILE
    row = lambda w: pl.BlockSpec((None, tm, w), lambda b, i: (b, i, 0))
    per_b = pl.BlockSpec((None, 1, d), lambda b, i: (b, 0, 0))
    full = lambda s: pl.BlockSpec(s, lambda b, i: (0, 0))
    pb = lambda t: t.reshape(bsz, 1, d)
    rw = jnp.zeros((d, LANES), F32).at[:, :N_EXPERTS].set(router_w)
    rwh = rw.astype(BF16)
    rw_cat = jnp.concatenate([rwh, (rw - rwh.astype(F32)).astype(BF16)], axis=1)
    rb = jnp.zeros((1, LANES), F32).at[0, :N_EXPERTS].set(router_b)
    return pl.pallas_call(
        _ssm_out_kernel,
        grid=(bsz, seq // tm),
        in_specs=[row(d), row(d), row(d), full((1, d)),
                  pl.BlockSpec((None, d, 2 * d), lambda b, i: (li, 0, 0)), full((1, 2 * d)),
                  per_b, full((1, d)), per_b, per_b,
                  full((d, 2 * LANES)), full((1, LANES))],
        out_specs=[row(d), row(d), row(LANES)],
        out_shape=[jax.ShapeDtypeStruct((bsz, seq, d), F32),
                   jax.ShapeDtypeStruct((bsz, seq, d), F32),
                   jax.ShapeDtypeStruct((bsz, seq, LANES), F32)],
        compiler_params=_params(2),
        name="ssm_out",
    )(y, h, x, d_skip.reshape(1, d), glu_w, glu_b.reshape(1, 2 * d), pb(gate_m),
      g.reshape(1, d), pb(shift), pb(scale), rw_cat, rb)


def _routing_tables(route, tm):
    n = route.shape[0]
    i32 = jnp.int32
    e_flat = route[:, :2].astype(i32).T.reshape(-1)
    counts = jnp.sum((e_flat[:, None] == jnp.arange(N_EXPERTS)[None, :]).astype(i32), axis=0)
    order = jnp.sort(e_flat * (2 * n) + jnp.arange(2 * n, dtype=i32)) % (2 * n)
    padded = ((counts + tm - 1) // tm) * tm
    ends = jnp.cumsum(padded)
    starts = ends - padded
    n_tiles = (2 * n) // tm + N_EXPERTS
    n_rows = n_tiles * tm
    s = jnp.arange(n_rows, dtype=i32)
    e_of_s = jnp.sum((s[:, None] >= ends[None, :]).astype(i32), axis=1)
    starts9 = jnp.concatenate([starts, ends[-1:]])
    counts9 = jnp.concatenate([counts, jnp.zeros((1,), i32)])
    ustart9 = jnp.concatenate([jnp.cumsum(counts) - counts, jnp.full((1,), 2 * n, i32)])
    within = s - starts9[e_of_s]
    real_before = ustart9[e_of_s] + jnp.minimum(within, counts9[e_of_s])
    pair = jnp.where(within < counts9[e_of_s],
                     order[jnp.minimum(ustart9[e_of_s] + within, 2 * n - 1)],
                     2 * n + s - real_before)
    src = jnp.concatenate([jnp.where(pair < 2 * n, pair % n, 0), jnp.zeros((tm,), i32)])
    dst = jnp.concatenate([n_rows + jnp.arange(tm, dtype=i32), pair])
    tile_expert = jnp.minimum(e_of_s[::tm], N_EXPERTS - 1)
    return src, dst, tile_expert


def _moe_kernel(te_ref, src_ref, src_next_ref, dst_prev_ref, dst_ref, h_hbm, wg_ref, wu_ref, wd_ref,
                y_hbm, hbuf, ybuf, gsem, ssem):
    t = pl.program_id(0)
    n_t = pl.num_programs(0)
    tm = hbuf.shape[1]
    ff = wg_ref.shape[1]
    fc = ff // MOE_FF_CHUNKS
    slot = t % 2
    other = 1 - slot

    def gather(slot_, r, row):
        return pltpu.make_async_copy(h_hbm.at[pl.ds(row, 1)], hbuf.at[slot_, pl.ds(r, 1)],
                                     gsem.at[slot_])

    def scatter(slot_, r, row):
        return pltpu.make_async_copy(ybuf.at[slot_, pl.ds(r, 1)], y_hbm.at[pl.ds(row, 1)],
                                     ssem.at[slot_])

    def for_rows(fn):
        def body(r, c):
            fn(r)
            return c
        lax.fori_loop(0, tm, body, 0, unroll=8)

    @pl.when(t == 0)
    def _():
        ybuf[...] = jnp.zeros(ybuf.shape, F32)
        for_rows(lambda r: gather(0, r, src_ref[r]).start())

    for_rows(lambda r: gather(slot, r, 0).wait())

    @pl.when(t > 0)
    def _():
        for_rows(lambda r: scatter(slot, r, 0).wait())

    h = hbuf[slot].astype(BF16)
    for r in range(tm):
        pltpu.async_copy(h_hbm.at[pl.ds(src_next_ref[r], 1)], hbuf.at[other, pl.ds(r, 1)],
                         gsem.at[other], priority=r % 2)
    for r in range(tm):
        pltpu.async_copy(ybuf.at[other, pl.ds(r, 1)], y_hbm.at[pl.ds(dst_prev_ref[r], 1)],
                         ssem.at[other], priority=r % 2)

    acc = jnp.zeros((tm, wd_ref.shape[1]), F32)
    for c in range(MOE_FF_CHUNKS):
        g = _dot(h, wg_ref[:, c * fc:(c + 1) * fc])
        u = _dot(h, wu_ref[:, c * fc:(c + 1) * fc])
        act = (g * _sigmoid(g) * u).astype(BF16)
        acc = acc + _dot(act, wd_ref[c * fc:(c + 1) * fc, :])
    ybuf[slot] = acc

    @pl.when(t == n_t - 1)
    def _():
        for_rows(lambda r: gather(other, r, 0).wait())
        for_rows(lambda r: scatter(other, r, 0).wait())
        for_rows(lambda r: scatter(slot, r, dst_ref[r]).start())
        for_rows(lambda r: scatter(slot, r, 0).wait())


def _moe_experts(h2, src, dst, tile_expert, w_gate, w_up, w_down, li):
    n, d = h2.shape
    ff = w_gate.shape[3]
    tm = MOE_TILE
    n_tiles = tile_expert.shape[0]
    wspec = lambda s: pl.BlockSpec((None, None) + s, lambda t, te: (li, te[t], 0, 0),
                                   pipeline_mode=pl.Buffered(1))
    idx = lambda off: pl.BlockSpec((tm,), lambda t, te: (t + off,), memory_space=pltpu.SMEM)
    grid_spec = pltpu.PrefetchScalarGridSpec(
        num_scalar_prefetch=1,
        grid=(n_tiles,),
        in_specs=[idx(0), idx(1), idx(0), idx(1),
                  pl.BlockSpec(memory_space=pl.ANY),
                  wspec((d, ff)), wspec((d, ff)), wspec((ff, d))],
        out_specs=pl.BlockSpec(memory_space=pl.ANY),
        scratch_shapes=[pltpu.VMEM((2, tm, d), F32), pltpu.VMEM((2, tm, d), F32),
                        pltpu.SemaphoreType.DMA((2,)), pltpu.SemaphoreType.DMA((2,))],
    )
    return pl.pallas_call(
        _moe_kernel,
        grid_spec=grid_spec,
        out_shape=jax.ShapeDtypeStruct((n_tiles * tm + tm, d), F32),
        compiler_params=_params(1),
        name="moe_experts",
    )(tile_expert, src, src, dst, dst, h2, w_gate, w_up, w_down)


def _combine_kernel(y1_ref, y2_ref, route_ref, x_ref, gf_ref, fg_ref, o_ref, *, final):
    route = route_ref[...]
    ff = route[:, 2:3] * y1_ref[...] + route[:, 3:4] * y2_ref[...]
    x2 = x_ref[...] + gf_ref[...] * ff
    if final:
        ms = jnp.mean(x2 * x2, axis=-1, keepdims=True)
        x2 = x2 * lax.rsqrt(ms + RMS_EPS) * fg_ref[...]
    o_ref[...] = x2


def _moe_combine(y, route, x1, gate_f, final_g, final):
    bsz, seq, d = x1.shape
    n = bsz * seq
    tm = ROW_TILE
    per_batch = seq // tm
    out = pl.pallas_call(
        functools.partial(_combine_kernel, final=final),
        grid=(n // tm,),
        in_specs=[pl.BlockSpec((tm, d), lambda i: (i, 0)),
                  pl.BlockSpec((tm, d), lambda i: (n // tm + i, 0)),
                  pl.BlockSpec((tm, LANES), lambda i: (i, 0)),
                  pl.BlockSpec((tm, d), lambda i: (i, 0)),
                  pl.BlockSpec((None, 1, d), lambda i: (i // per_batch, 0, 0)),
                  pl.BlockSpec((1, d), lambda i: (0, 0))],
        out_specs=pl.BlockSpec((tm, d), lambda i: (i, 0)),
        out_shape=jax.ShapeDtypeStruct((n, d), F32),
        compiler_params=_params(1),
        name="moe_combine",
    )(y, y, route.reshape(n, LANES), x1.reshape(n, d),
      gate_f.reshape(bsz, 1, d), final_g.reshape(1, d))
    return out.reshape(bsz, seq, d)


def _even_layer(x, mods, norm_mix_g, norm_ffn_g, cos, sin, li, w_in, conv_w, w_out,
                w_gate, w_up, w_down, next_norm):
    sh_m, sc_m, g_m, sh_f, sc_f, g_f = mods
    qkv, b_out = _even_inproj(x, norm_mix_g, sh_m, sc_m, w_in, li, cos, sin, conv_w)
    x1, h2 = _attention_outproj(qkv, b_out, x, w_out, li, g_m, norm_ffn_g, sh_f, sc_f)
    return _dense_ffn(h2, x1, w_gate, w_up, w_down, li, g_f, *next_norm)


def _odd_layer(x, h, mods, norm_ffn_g, ssm, d_skip, li, glu_w, glu_b,
               router_w, router_b, w_gate, w_up, w_down, final_g, final):
    _, _, g_m, sh_f, sc_f, g_f = mods
    bsz, seq, d = x.shape
    y = _ssm_scan(h, _ssm_operators(*ssm))
    x1, h2, route = _ssm_out(y, h, x, d_skip, glu_w, li, glu_b, g_m,
                             norm_ffn_g, sh_f, sc_f, router_w, router_b)
    route = route.reshape(bsz * seq, LANES)
    src, dst, tile_expert = _routing_tables(route, MOE_TILE)
    y_pairs = _moe_experts(h2.reshape(bsz * seq, d), src, dst, tile_expert,
                           w_gate, w_up, w_down, li)
    return _moe_combine(y_pairs, route, x1, g_f, final_g, final)


def kernel(x, c, positions, mod_w, mod_b, norm_mix_g, norm_ffn_g, ev_w_in, ev_conv_w, ev_w_out, ffn_w_gate, ffn_w_up, ffn_w_down, ssm_a_re, ssm_a_im, ssm_log_step, ssm_b_re, ssm_b_im, ssm_c_re, ssm_c_im, ssm_d, glu_w, glu_b, moe_router_w, moe_router_b, moe_w_gate, moe_w_up, moe_w_down, final_norm_g):
    depth = mod_w.shape[0]
    d = x.shape[2]
    assert depth % 2 == 0
    mod = _modulation(c, mod_w, mod_b)
    cos, sin = _rope_tables(positions)
    layer_mods = [[mod[layer, :, j * d:(j + 1) * d] for j in range(6)] for layer in range(depth)]
    bf = lambda w: w.astype(BF16)
    ev_w_in, ev_w_out, glu_w = bf(ev_w_in), bf(ev_w_out), bf(glu_w)
    ffn_w_gate, ffn_w_up, ffn_w_down = bf(ffn_w_gate), bf(ffn_w_up), bf(ffn_w_down)
    moe_w_gate, moe_w_up, moe_w_down = bf(moe_w_gate), bf(moe_w_up), bf(moe_w_down)
    h = None
    for layer in range(depth):
        mods = layer_mods[layer]
        i = layer // 2
        if layer % 2 == 0:
            nxt = layer_mods[layer + 1]
            x, h = _even_layer(x, mods, norm_mix_g[layer], norm_ffn_g[layer], cos, sin, i,
                               ev_w_in, ev_conv_w[i], ev_w_out,
                               ffn_w_gate, ffn_w_up, ffn_w_down,
                               (norm_mix_g[layer + 1], nxt[0], nxt[1]))
        else:
            ssm = (ssm_a_re[i], ssm_a_im[i], ssm_log_step[i], ssm_b_re[i], ssm_b_im[i],
                   ssm_c_re[i], ssm_c_im[i])
            x = _odd_layer(x, h, mods, norm_ffn_g[layer], ssm, ssm_d[i], i,
                           glu_w, glu_b[i], moe_router_w[i], moe_router_b[i],
                           moe_w_gate, moe_w_up, moe_w_down,
                           final_norm_g, layer == depth - 1)
    return x
```
